```python
import jax, jax.numpy as jnp
from jax import lax
import numpy as np

D_MODEL = 1024
BATCH = 4
SEQ = 4096
DEPTH = 4

A_GROUPS = 8
A_GROUP_DIM = 64
A_WIDTH = A_GROUPS * A_GROUP_DIM
A_CHUNK = 128
B_HEADS = 8
B_HEAD_DIM = 64
B_WIDTH = B_HEADS * B_HEAD_DIM
B_BRANCHES = ((128, 1), (512, 4), (2048, 16))
B_BLOCK = 128
C_HEADS = 4
C_DK = 128
C_DV = 256
C_GATE_RANK = 16
C_TAU = 16.0
C_CHUNK = 64
D_FF = 2816
CONV_W = 3

EPS = 1e-6
NEG = -1e30
N_EVEN = (DEPTH + 1) // 2
N_ODD = DEPTH // 2
EVEN_IN = 2 * A_WIDTH + 3 * B_WIDTH
ODD_IN = 2 * C_HEADS * C_DK + 2 * C_HEADS * C_DV + C_GATE_RANK

kernel_name = "hybrid_gmlp_dilated_gla_convffn"


def rms_norm(x, g):
    xf = x.astype(jnp.float32)
    return xf * lax.rsqrt(jnp.mean(xf * xf, -1, keepdims=True) + EPS) * g.astype(jnp.float32)


def layer_norm(x, g, b):
    xf = x.astype(jnp.float32)
    mu = jnp.mean(xf, -1, keepdims=True)
    xc = xf - mu
    return xc * lax.rsqrt(jnp.mean(xc * xc, -1, keepdims=True) + EPS) * g + b


def causal_dwconv(h, w, b):
    S = h.shape[1]
    hp = jnp.pad(h, ((0, 0), (CONV_W - 1, 0), (0, 0)))
    out = b
    for k in range(CONV_W):
        out = out + hp[:, k:k + S] * w[k]
    return out


def chunked_gmlp(u, v, ln_g, ln_b, w_s, b_s):
    Bn, S, _ = u.shape
    nc = S // A_CHUNK
    vn = layer_norm(v, ln_g, ln_b).reshape(Bn, nc, A_CHUNK, A_GROUPS, A_GROUP_DIM)
    causal = jnp.tril(jnp.ones((A_CHUNK, A_CHUNK), dtype=bool))
    ws = jnp.where(causal[None], w_s.astype(jnp.float32), 0.0)
    mixed = jnp.einsum('gts,bnsgc->bntgc', ws, vn) + b_s.astype(jnp.float32).T[:, :, None]
    return u.astype(jnp.float32) * mixed.reshape(Bn, S, A_WIDTH)


def dilated_branch(q, k, v, window, dilation):
    Bn, H, S, hd = q.shape
    span = dilation * B_BLOCK
    Sp = -(-S // span) * span
    M = Sp // dilation
    nb = M // B_BLOCK
    n_back = window // dilation

    def to_blocks(t):
        t = jnp.pad(t, ((0, 0), (0, 0), (0, Sp - S), (0, 0)))
        t = t.reshape(Bn, H, M, dilation, hd)
        t = jnp.moveaxis(t, 3, 2)
        return t.reshape(Bn, H, dilation, nb, B_BLOCK, hd)

    def band(t):
        tp = jnp.pad(t, ((0, 0), (0, 0), (0, 0), (1, 0), (0, 0), (0, 0)))
        return jnp.concatenate([tp[:, :, :, :-1], tp[:, :, :, 1:]], axis=4)

    qb = to_blocks(q)
    kk = band(to_blocks(k))
    vv = band(to_blocks(v))

    i = jnp.arange(B_BLOCK)[:, None]
    j = jnp.arange(2 * B_BLOCK)[None, :]
    dist = B_BLOCK + i - j
    in_band = (dist >= 0) & (dist <= n_back)
    valid = in_band[None] & ((jnp.arange(nb) > 0)[:, None, None] | (j >= B_BLOCK)[None])

    s = jnp.einsum('bhrnic,bhrnjc->bhrnij', qb, kk) * (hd ** -0.5)
    s = jnp.where(valid, s, NEG)
    m = jnp.max(s, -1, keepdims=True)
    p = jnp.exp(s - m)
    den = jnp.sum(p, -1)
    o = jnp.einsum('bhrnij,bhrnjc->bhrnic', p, vv) / den[..., None]
    lse = m[..., 0] + jnp.log(den)

    def from_blocks(t):
        tail = t.shape[5:]
        t = t.reshape((Bn, H, dilation, M) + tail)
        t = jnp.moveaxis(t, 2, 3).reshape((Bn, H, Sp) + tail)
        return t[:, :, :S]

    return from_blocks(o), from_blocks(lse)


def even_mixer(h, w_in, a_ln_g, a_ln_b, a_ws, a_bs, q_g, k_g, w_out):
    Bn, S, _ = h.shape
    z = h @ w_in
    u, v, q, k, vb = jnp.split(
        z, [A_WIDTH, 2 * A_WIDTH, 2 * A_WIDTH + B_WIDTH, 2 * A_WIDTH + 2 * B_WIDTH], axis=-1)
    a_out = chunked_gmlp(jax.nn.gelu(u, approximate=False), jax.nn.gelu(v, approximate=False),
                         a_ln_g, a_ln_b, a_ws, a_bs)

    def heads(t):
        return t.reshape(Bn, S, B_HEADS, B_HEAD_DIM).transpose(0, 2, 1, 3)

    qh = rms_norm(heads(q), q_g)
    kh = rms_norm(heads(k), k_g)
    vh = heads(vb).astype(jnp.float32)
    outs, lses = [], []
    for window, dilation in B_BRANCHES:
        o_r, l_r = dilated_branch(qh, kh, vh, window, dilation)
        outs.append(o_r)
        lses.append(l_r)
    wts = jax.nn.softmax(jnp.stack(lses), axis=0)
    o = jnp.einsum('rbhs,rbhsc->bhsc', wts, jnp.stack(outs))
    b_out = o.transpose(0, 2, 1, 3).reshape(Bn, S, B_WIDTH)
    return jnp.concatenate([a_out, b_out], axis=-1).astype(h.dtype) @ w_out


def gla_chunked(q, k, v, log_a):
    Bn, H, S, dk = q.shape
    dv = v.shape[-1]
    nc = S // C_CHUNK

    def rs(t):
        return t.reshape(Bn, H, nc, C_CHUNK, t.shape[-1])

    q, k, v, log_a = rs(q), rs(k), rs(v), rs(log_a)
    b = jnp.cumsum(log_a, axis=3)
    b_last = b[:, :, :, -1:]
    q_t = q * jnp.exp(b)
    k_t = k * jnp.exp(-b)
    k_s = k * jnp.exp(b_last - b)
    causal = jnp.tril(jnp.ones((C_CHUNK, C_CHUNK), dtype=bool))
    attn = jnp.where(causal, jnp.einsum('bhnik,bhnjk->bhnij', q_t, k_t), 0.0)
    o_intra = jnp.einsum('bhnij,bhnjv->bhniv', attn, v)
    kv = jnp.einsum('bhnjk,bhnjv->bhnkv', k_s, v)
    decay = jnp.exp(b_last[:, :, :, 0])

    def step(state, inp):
        kv_n, d_n = inp
        return d_n[..., None] * state + kv_n, state

    init = jnp.zeros((Bn, H, dk, dv), jnp.float32)
    _, states = lax.scan(step, init, (jnp.moveaxis(kv, 2, 0), jnp.moveaxis(decay, 2, 0)))
    states = jnp.moveaxis(states, 0, 2)
    o_inter = jnp.einsum('bhnik,bhnkv->bhniv', q_t, states)
    return (o_intra + o_inter).reshape(Bn, H, S, dv)


def gla_mixer(h, w_in, w_a2, b_a, head_g, w_out):
    Bn, S, _ = h.shape
    hk, hv = C_HEADS * C_DK, C_HEADS * C_DV
    z = h @ w_in
    q, k, v, r, ga = jnp.split(z, [hk, 2 * hk, 2 * hk + hv, 2 * hk + 2 * hv], axis=-1)

    def heads(t, dh):
        return t.reshape(Bn, S, C_HEADS, dh).transpose(0, 2, 1, 3).astype(jnp.float32)

    log_a = jax.nn.log_sigmoid((ga @ w_a2 + b_a).astype(jnp.float32)) / C_TAU
    o = gla_chunked(heads(q, C_DK) * (C_DK ** -0.5), heads(k, C_DK), heads(v, C_DV),
                    heads(log_a, C_DK))
    o = rms_norm(o, head_g)
    o = o.transpose(0, 2, 1, 3).reshape(Bn, S, hv) * jax.nn.silu(r.astype(jnp.float32))
    return o.astype(h.dtype) @ w_out


def conv_ffn(h, w_gate, w_up, conv_w, conv_b, w_down):
    g = causal_dwconv(h @ w_gate, conv_w, conv_b)
    return (jax.nn.silu(g) * (h @ w_up)) @ w_down


def setup_inputs(seed: int = 0) -> dict:
    key = jax.random.key(seed)
    ks = iter(jax.random.split(key, 32))

    def nrm(shape, scale):
        return jax.random.normal(next(ks), shape, jnp.float32) * scale

    res = (2 * DEPTH) ** -0.5
    return {
        "x": nrm((BATCH, SEQ, D_MODEL), 1.0),
        "norm_mix_g": 1.0 + nrm((DEPTH, D_MODEL), 0.1),
        "norm_ffn_g": 1.0 + nrm((DEPTH, D_MODEL), 0.1),
        "ev_w_in": nrm((N_EVEN, D_MODEL, EVEN_IN), D_MODEL ** -0.5),
        "ev_a_ln_g": 1.0 + nrm((N_EVEN, A_WIDTH), 0.1),
        "ev_a_ln_b": nrm((N_EVEN, A_WIDTH), 0.02),
        "ev_a_ws": nrm((N_EVEN, A_GROUPS, A_CHUNK, A_CHUNK), A_CHUNK ** -0.5),
        "ev_a_bs": 1.0 + nrm((N_EVEN, A_GROUPS, A_CHUNK), 0.1),
        "ev_q_g": 1.0 + nrm((N_EVEN, B_HEAD_DIM), 0.1),
        "ev_k_g": 1.0 + nrm((N_EVEN, B_HEAD_DIM), 0.1),
        "ev_w_out": nrm((N_EVEN, A_WIDTH + B_WIDTH, D_MODEL), (A_WIDTH + B_WIDTH) ** -0.5 * res),
        "od_w_in": nrm((N_ODD, D_MODEL, ODD_IN), D_MODEL ** -0.5),
        "od_w_a2": nrm((N_ODD, C_GATE_RANK, C_HEADS * C_DK), C_GATE_RANK ** -0.5),
        "od_b_a": nrm((N_ODD, C_HEADS * C_DK), 0.5),
        "od_head_g": 1.0 + nrm((N_ODD, C_DV), 0.1),
        "od_w_out": nrm((N_ODD, C_HEADS * C_DV, D_MODEL), (C_HEADS * C_DV) ** -0.5 * res),
        "ffn_w_gate": nrm((DEPTH, D_MODEL, D_FF), D_MODEL ** -0.5),
        "ffn_w_up": nrm((DEPTH, D_MODEL, D_FF), D_MODEL ** -0.5),
        "ffn_conv_w": nrm((DEPTH, CONV_W, D_FF), CONV_W ** -0.5),
        "ffn_conv_b": nrm((DEPTH, D_FF), 0.02),
        "ffn_w_down": nrm((DEPTH, D_FF, D_MODEL), D_FF ** -0.5 * res),
    }


def reference(x, norm_mix_g, norm_ffn_g,
              ev_w_in, ev_a_ln_g, ev_a_ln_b, ev_a_ws, ev_a_bs, ev_q_g, ev_k_g, ev_w_out,
              od_w_in, od_w_a2, od_b_a, od_head_g, od_w_out,
              ffn_w_gate, ffn_w_up, ffn_conv_w, ffn_conv_b, ffn_w_down):
    for layer in range(DEPTH):
        h = rms_norm(x, norm_mix_g[layer]).astype(x.dtype)
        if layer % 2 == 0:
            e = layer // 2
            mix = even_mixer(h, ev_w_in[e], ev_a_ln_g[e], ev_a_ln_b[e], ev_a_ws[e], ev_a_bs[e],
                             ev_q_g[e], ev_k_g[e], ev_w_out[e])
        else:
            o = layer // 2
            mix = gla_mixer(h, od_w_in[o], od_w_a2[o], od_b_a[o], od_head_g[o], od_w_out[o])
        x = x + mix.astype(x.dtype)
        h = rms_norm(x, norm_ffn_g[layer]).astype(x.dtype)
        x = x + conv_ffn(h, ffn_w_gate[layer], ffn_w_up[layer], ffn_conv_w[layer],
                         ffn_conv_b[layer], ffn_w_down[layer]).astype(x.dtype)
    return x
```

```python
import functools

import jax
import jax.numpy as jnp
from jax import lax
from jax.experimental import pallas as pl
from jax.experimental.pallas import tpu as pltpu

A_GROUPS = 8
A_GROUP_DIM = 64
A_CHUNK = 128
B_HEAD_DIM = 64
B_DILATIONS = (1, 4, 16)
B_BLOCK = 128
C_HEADS = 4
C_DK = 128
C_DV = 256
C_GATE_RANK = 16
C_TAU = 16.0
C_CHUNK = 64
CONV_W = 3
EPS = 1e-6
NEG = -1e30

V7X_LANES = 128
V7X_SUBLANES = 8
V7X_VMEM_BUDGET = 56 * 1024 * 1024

F32 = jnp.float32
BF16 = jnp.bfloat16
NT_DIMS = (((1,), (1,)), ((), ()))
TN_DIMS = (((0,), (0,)), ((), ()))


def _params(semantics, vmem_bytes):
    return pltpu.CompilerParams(
        dimension_semantics=semantics,
        vmem_limit_bytes=min(int(vmem_bytes * 1.25) + (4 << 20), V7X_VMEM_BUDGET))


def _rms_rows(x, g):
    return x * lax.rsqrt(jnp.mean(x * x, axis=-1, keepdims=True) + EPS) * g


def _gelu(x):
    return 0.5 * x * (1.0 + lax.erf(x * (0.5 ** 0.5)))


def _silu(x):
    return x * jax.nn.sigmoid(x)


def _norm_matmul_kernel(x_ref, g_ref, w_ref, o_ref, h_ref):
    @pl.when(pl.program_id(1) == 0)
    def _():
        h_ref[...] = _rms_rows(x_ref[...], g_ref[...]).astype(BF16)

    o_ref[...] = jnp.dot(h_ref[...], w_ref[...], preferred_element_type=F32)


def _norm_matmul_side_kernel(x_ref, g_ref, w_ref, ws_ref, o_ref, os_ref, h_ref):
    @pl.when(pl.program_id(1) == 0)
    def _():
        h = _rms_rows(x_ref[...], g_ref[...]).astype(BF16)
        h_ref[...] = h
        os_ref[...] = jnp.dot(h, ws_ref[...], preferred_element_type=F32)

    o_ref[...] = jnp.dot(h_ref[...], w_ref[...], preferred_element_type=F32)


def _norm_matmul(x, g, w, w_side=None, *, tm, tn):
    n, d = x.shape
    f = w.shape[1]
    in_specs = [pl.BlockSpec((tm, d), lambda i, j: (i, 0)),
                pl.BlockSpec((1, d), lambda i, j: (0, 0)),
                pl.BlockSpec((d, tn), lambda i, j: (0, j))]
    out_specs = pl.BlockSpec((tm, tn), lambda i, j: (i, j))
    out_shape = jax.ShapeDtypeStruct((n, f), F32)
    vmem = 2 * (tm * d * 4 + d * tn * 2 + tm * tn * 4) + tm * d * 2
    args = (x, g.reshape(1, d), w)
    body = _norm_matmul_kernel
    if w_side is not None:
        fs = w_side.shape[1]
        in_specs.append(pl.BlockSpec((d, fs), lambda i, j: (0, 0)))
        out_specs = (out_specs, pl.BlockSpec((tm, fs), lambda i, j: (i, 0)))
        out_shape = (out_shape, jax.ShapeDtypeStruct((n, fs), F32))
        vmem += 2 * (d * fs * 2 + tm * fs * 4)
        args += (w_side,)
        body = _norm_matmul_side_kernel
    return pl.pallas_call(
        body,
        grid=(n // tm, f // tn),
        in_specs=in_specs,
        out_specs=out_specs,
        out_shape=out_shape,
        scratch_shapes=[pltpu.VMEM((tm, d), BF16)],
        compiler_params=_params(("parallel", "arbitrary"), vmem),
        name="norm_matmul",
    )(*args)


def _proj_residual_kernel(x_ref, a_ref, b_ref, wa_ref, wb_ref, o_ref):
    acc = jnp.dot(a_ref[...].astype(BF16), wa_ref[...], preferred_element_type=F32)
    acc += jnp.dot(b_ref[...].astype(BF16), wb_ref[...], preferred_element_type=F32)
    o_ref[...] = x_ref[...] + acc


def _proj_residual(x, a, a_blk, b, b_blk, w, *, tm):
    n, d = x.shape
    kh = w.shape[0] // 2
    vmem = 2 * (2 * tm * d * 4 + 2 * tm * kh * 4 + 2 * kh * d * 2)
    return pl.pallas_call(
        _proj_residual_kernel,
        grid=(n // tm,),
        in_specs=[pl.BlockSpec((tm, d), lambda i: (i, 0)),
                  pl.BlockSpec((tm, kh), lambda i: (i, a_blk)),
                  pl.BlockSpec((tm, kh), lambda i: (i, b_blk)),
                  pl.BlockSpec((kh, d), lambda i: (0, 0)),
                  pl.BlockSpec((kh, d), lambda i: (1, 0))],
        out_specs=pl.BlockSpec((tm, d), lambda i: (i, 0)),
        out_shape=jax.ShapeDtypeStruct((n, d), F32),
        compiler_params=_params(("parallel",), vmem),
        name="proj_residual",
    )(x, a, b, w, w)


def _gmlp_kernel(u_ref, v_ref, lng_ref, lnb_ref, ws_ref, bs_ref, o_ref):
    tm = u_ref.shape[0]
    v = _gelu(v_ref[...])
    vc = v - jnp.mean(v, axis=-1, keepdims=True)
    vn = vc * lax.rsqrt(jnp.mean(vc * vc, axis=-1, keepdims=True) + EPS) * lng_ref[...] + lnb_ref[...]
    row = lax.broadcasted_iota(jnp.int32, (A_CHUNK, 2 * A_CHUNK), 0)
    col = lax.broadcasted_iota(jnp.int32, (A_CHUNK, 2 * A_CHUNK), 1)
    causal = (col % A_CHUNK) <= row
    lane = lax.broadcasted_iota(jnp.int32, (1, 2 * V7X_LANES), 1)
    first_group = (lane % V7X_LANES) < A_GROUP_DIM
    for t in range(ws_ref.shape[0]):
        lanes = slice(t * V7X_LANES, (t + 1) * V7X_LANES)
        w_pair = jnp.where(causal, ws_ref[t], 0.0).astype(BF16)
        bias = bs_ref[:, lanes]
        for p in range(tm // (2 * A_CHUNK)):
            r0 = slice(2 * p * A_CHUNK, (2 * p + 1) * A_CHUNK)
            r1 = slice((2 * p + 1) * A_CHUNK, (2 * p + 2) * A_CHUNK)
            cc = jnp.concatenate([vn[r0, lanes], vn[r1, lanes]], axis=1)
            rhs = jnp.concatenate([jnp.where(first_group, cc, 0.0),
                                   jnp.where(first_group, 0.0, cc)], axis=0).astype(BF16)
            mixed = jnp.dot(w_pair, rhs, preferred_element_type=F32)
            o_ref[r0, lanes] = _gelu(u_ref[r0, lanes]) * (mixed[:, :V7X_LANES] + bias)
            o_ref[r1, lanes] = _gelu(u_ref[r1, lanes]) * (mixed[:, V7X_LANES:] + bias)


def _gmlp(z, ln_g, ln_b, w_s, b_s, *, tm):
    n = z.shape[0]
    aw = A_GROUPS * A_GROUP_DIM
    pairs = A_GROUPS // 2
    ws_pairs = w_s.reshape(pairs, 2, A_CHUNK, A_CHUNK).transpose(0, 2, 1, 3).reshape(pairs, A_CHUNK, 2 * A_CHUNK)
    bias = jnp.repeat(b_s.T, A_GROUP_DIM, axis=1)
    vmem = 2 * (3 * tm * aw * 4 + pairs * A_CHUNK * 2 * A_CHUNK * 4 + A_CHUNK * aw * 4) + 4 * tm * aw * 4
    return pl.pallas_call(
        _gmlp_kernel,
        grid=(n // tm,),
        in_specs=[pl.BlockSpec((tm, aw), lambda i: (i, 0)),
                  pl.BlockSpec((tm, aw), lambda i: (i, 1)),
                  pl.BlockSpec((1, aw), lambda i: (0, 0)),
                  pl.BlockSpec((1, aw), lambda i: (0, 0)),
                  pl.BlockSpec((pairs, A_CHUNK, 2 * A_CHUNK), lambda i: (0, 0, 0)),
                  pl.BlockSpec((A_CHUNK, aw), lambda i: (0, 0))],
        out_specs=pl.BlockSpec((tm, aw), lambda i: (i, 0)),
        out_shape=jax.ShapeDtypeStruct((n, aw), F32),
        compiler_params=_params(("parallel",), vmem),
        name="gmlp",
    )(z, z, ln_g.reshape(1, aw), ln_b.reshape(1, aw), ws_pairs, bias)


B_PLANES = max(B_DILATIONS)


def _dilated_kernel(q_ref, k_ref, v_ref, qg_ref, kg_ref, o_ref,
                    qs_ref, ks_ref, vs_ref, ob_ref, lb_ref, mask_ref):
    m_rows = qs_ref.shape[1]
    blk2 = 2 * B_BLOCK
    lane = lax.broadcasted_iota(jnp.int32, (1, V7X_LANES), 1)
    head0 = lane < B_HEAD_DIM

    def head_rms(x, g):
        x2 = x * x
        s0 = jnp.sum(jnp.where(head0, x2, 0.0), axis=-1, keepdims=True)
        s1 = jnp.sum(jnp.where(head0, 0.0, x2), axis=-1, keepdims=True)
        ms = jnp.where(head0, s0, s1) * (1.0 / B_HEAD_DIM)
        return x * lax.rsqrt(ms + EPS) * g

    for r in range(B_PLANES):
        rows = pl.ds(r, m_rows, stride=B_PLANES)
        qs_ref[r] = head_rms(q_ref[rows, :], qg_ref[...]) * (B_HEAD_DIM ** -0.5)
        ks_ref[r] = head_rms(k_ref[rows, :], kg_ref[...])
        vs_ref[r] = v_ref[rows, :]

    rowi = lax.broadcasted_iota(jnp.int32, (blk2, blk2), 0) % B_BLOCK
    coli = lax.broadcasted_iota(jnp.int32, (blk2, blk2), 1)
    is_cur = coli >= B_BLOCK
    colj = coli % B_BLOCK
    for bi, d in enumerate(B_DILATIONS):
        planes = B_PLANES // d
        mb = B_BLOCK // planes
        i_pos = (rowi % mb) * planes + rowi // mb
        j_pos = (colj % mb) * planes + colj // mb
        band = jnp.where(jnp.where(is_cur, i_pos - j_pos, j_pos - i_pos) >= 0, 1.0, 0.0)
        mask_ref[2 * bi] = jnp.where(is_cur, band, 0.0)
        mask_ref[2 * bi + 1] = band

    for bi, d in enumerate(B_DILATIONS):
        planes = B_PLANES // d
        mb = B_BLOCK // planes
        nb = m_rows // mb

        def block(blk, carry, bi=bi, d=d, planes=planes, mb=mb, nb=nb):
            r = blk // nb
            n = blk % nb
            off = pl.multiple_of(n * mb, mb)
            off_prev = pl.multiple_of(jnp.maximum(n - 1, 0) * mb, mb)

            def gather(ref, o):
                return jnp.concatenate([ref[r + d * a, pl.ds(o, mb), :] for a in range(planes)], axis=0)

            qb = gather(qs_ref, off)
            q2 = jnp.concatenate([jnp.where(head0, qb, 0.0), jnp.where(head0, 0.0, qb)], axis=0).astype(BF16)
            kcat = jnp.concatenate([gather(ks_ref, off_prev), gather(ks_ref, off)], axis=0).astype(BF16)
            vcat = jnp.concatenate([gather(vs_ref, off_prev), gather(vs_ref, off)], axis=0).astype(BF16)
            s = lax.dot_general(q2, kcat, NT_DIMS, preferred_element_type=F32)
            valid = mask_ref[2 * bi + jnp.minimum(n, 1)] > 0.0
            s = jnp.where(valid, s, NEG)
            m = jnp.max(s, axis=-1, keepdims=True)
            p = jnp.exp(s - m)
            den = jnp.sum(p, axis=-1, keepdims=True)
            pv = jnp.dot(p.astype(BF16), vcat, preferred_element_type=F32)
            o2 = pv / den
            lse = m + jnp.log(den)
            o = jnp.where(head0, o2[:B_BLOCK], o2[B_BLOCK:])
            l = jnp.where(head0, lse[:B_BLOCK], lse[B_BLOCK:])
            for a in range(planes):
                ob_ref[bi, r + d * a, pl.ds(off, mb), :] = o[a * mb:(a + 1) * mb]
                lb_ref[bi, r + d * a, pl.ds(off, mb), :] = l[a * mb:(a + 1) * mb]
            return carry

        lax.fori_loop(0, d * nb, block, 0)

    for r in range(B_PLANES):
        ls = [lb_ref[bi, r] for bi in range(len(B_DILATIONS))]
        mx = functools.reduce(jnp.maximum, ls)
        es = [jnp.exp(l - mx) for l in ls]
        tot = functools.reduce(lambda a, b: a + b, es)
        out = functools.reduce(lambda a, b: a + b,
                               [(es[bi] / tot) * ob_ref[bi, r] for bi in range(len(B_DILATIONS))])
        o_ref[pl.ds(r, m_rows, stride=B_PLANES), :] = out


def _dilated(z3, q_g, k_g):
    bsz, s, _ = z3.shape
    bw = z3.shape[2] // 5
    pairs = bw // V7X_LANES
    m_rows = s // B_PLANES
    nbr = len(B_DILATIONS)
    qg2 = jnp.tile(q_g, 2).reshape(1, V7X_LANES)
    kg2 = jnp.tile(k_g, 2).reshape(1, V7X_LANES)
    tile_bytes = s * V7X_LANES * 4
    vmem = 2 * 4 * tile_bytes + 3 * tile_bytes + 2 * nbr * tile_bytes + 2 * nbr * 4 * B_BLOCK * B_BLOCK * 4
    qkv_spec = lambda base: pl.BlockSpec((None, s, V7X_LANES), lambda b, p: (b, 0, base + p))
    return pl.pallas_call(
        _dilated_kernel,
        grid=(bsz, pairs),
        in_specs=[qkv_spec(2 * pairs), qkv_spec(3 * pairs), qkv_spec(4 * pairs),
                  pl.BlockSpec((1, V7X_LANES), lambda b, p: (0, 0)),
                  pl.BlockSpec((1, V7X_LANES), lambda b, p: (0, 0))],
        out_specs=pl.BlockSpec((None, s, V7X_LANES), lambda b, p: (b, 0, p)),
        out_shape=jax.ShapeDtypeStruct((bsz, s, bw), F32),
        scratch_shapes=[pltpu.VMEM((B_PLANES, m_rows, V7X_LANES), F32)] * 3
                       + [pltpu.VMEM((nbr, B_PLANES, m_rows, V7X_LANES), F32)] * 2
                       + [pltpu.VMEM((2 * nbr, 2 * B_BLOCK, 2 * B_BLOCK), F32)],
        compiler_params=_params(("parallel", "parallel"), vmem),
        name="dilated_attention",
    )(z3, z3, z3, qg2, kg2)


def _gla_kernel(q_ref, k_ref, v_ref, r_ref, ga_ref, wa_ref, ba_ref, hg_ref, o_ref, st_ref):
    tc = q_ref.shape[0]

    @pl.when(pl.program_id(2) == 0)
    def _():
        st_ref[...] = jnp.zeros_like(st_ref)

    gate = jnp.dot(ga_ref[...].astype(BF16), wa_ref[...], preferred_element_type=F32) + ba_ref[...]
    log_a = jax.nn.log_sigmoid(gate) / C_TAU
    row = lax.broadcasted_iota(jnp.int32, (tc, tc), 0)
    col = lax.broadcasted_iota(jnp.int32, (tc, tc), 1)
    tri = jnp.where((row // C_CHUNK == col // C_CHUNK) & (row >= col), 1.0, 0.0).astype(BF16)
    hi = log_a.astype(BF16)
    rest = log_a - hi.astype(F32)
    mid = rest.astype(BF16)
    lo = (rest - mid.astype(F32)).astype(BF16)
    b = (jnp.dot(tri, hi, preferred_element_type=F32) + jnp.dot(tri, mid, preferred_element_type=F32)
         + jnp.dot(tri, lo, preferred_element_type=F32))
    k = k_ref[...]
    q_t = ((q_ref[...] * (C_DK ** -0.5)) * jnp.exp(b)).astype(BF16)
    k_t = (k * jnp.exp(-b)).astype(BF16)
    ci = lax.broadcasted_iota(jnp.int32, (C_CHUNK, C_CHUNK), 0)
    cj = lax.broadcasted_iota(jnp.int32, (C_CHUNK, C_CHUNK), 1)
    causal = ci >= cj
    for c in range(tc // C_CHUNK):
        sl = slice(c * C_CHUNK, (c + 1) * C_CHUNK)
        b_last = b[(c + 1) * C_CHUNK - 1:(c + 1) * C_CHUNK]
        k_s = (k[sl] * jnp.exp(b_last - b[sl])).astype(BF16)
        v_c = v_ref[sl, :].astype(BF16)
        attn = lax.dot_general(q_t[sl], k_t[sl], NT_DIMS, preferred_element_type=F32)
        attn = jnp.where(causal, attn, 0.0).astype(BF16)
        o = jnp.dot(attn, v_c, preferred_element_type=F32)
        st = st_ref[...]
        o += lax.dot_general(q_t[sl], st.astype(BF16), NT_DIMS, preferred_element_type=F32)
        kv_t = lax.dot_general(v_c, k_s, TN_DIMS, preferred_element_type=F32)
        st_ref[...] = st * jnp.exp(b_last) + kv_t
        o_ref[sl, :] = _rms_rows(o, hg_ref[...]) * _silu(r_ref[sl, :])


def _gla(z3, ga3, w_a2, b_a, head_g, *, tc):
    bsz, s, _ = z3.shape
    wa = jnp.pad(w_a2, ((0, V7X_LANES - C_GATE_RANK), (0, 0))).astype(BF16)
    kq = C_HEADS
    vmem = 2 * (3 * tc * C_DK * 4 + 3 * tc * C_DV * 4 + V7X_LANES * C_DK * 2) + C_DV * C_DK * 4 + 8 * tc * C_DK * 4
    return pl.pallas_call(
        _gla_kernel,
        grid=(bsz, C_HEADS, s // tc),
        in_specs=[pl.BlockSpec((None, tc, C_DK), lambda b, h, t: (b, t, h)),
                  pl.BlockSpec((None, tc, C_DK), lambda b, h, t: (b, t, kq + h)),
                  pl.BlockSpec((None, tc, C_DV), lambda b, h, t: (b, t, kq + h)),
                  pl.BlockSpec((None, tc, C_DV), lambda b, h, t: (b, t, 2 * kq + h)),
                  pl.BlockSpec((None, tc, V7X_LANES), lambda b, h, t: (b, t, 0)),
                  pl.BlockSpec((V7X_LANES, C_DK), lambda b, h, t: (0, h)),
                  pl.BlockSpec((1, C_DK), lambda b, h, t: (0, h)),
                  pl.BlockSpec((1, C_DV), lambda b, h, t: (0, 0))],
        out_specs=pl.BlockSpec((None, tc, C_DV), lambda b, h, t: (b, t, h)),
        out_shape=jax.ShapeDtypeStruct((bsz, s, C_HEADS * C_DV), F32),
        scratch_shapes=[pltpu.VMEM((C_DV, C_DK), F32)],
        compiler_params=_params(("parallel", "parallel", "arbitrary"), vmem),
        name="gla",
    )(z3, z3, z3, z3, ga3, wa, b_a.reshape(1, -1), head_g.reshape(1, C_DV))


def _ffn_kernel(x_ref, g_ref, wg_ref, wu_ref, cw_ref, cb_ref, wd_ref, o_ref,
                h_ref, acc_ref, gbuf_ref, halo_ref, *, tiles_per_seq):
    tm = x_ref.shape[0]
    halo_rows = V7X_SUBLANES
    x = x_ref[...]
    h_ref[...] = _rms_rows(x, g_ref[...]).astype(BF16)
    acc_ref[...] = jnp.zeros_like(acc_ref)
    seq_start = pl.program_id(0) % tiles_per_seq == 0

    def chunk(j, carry):
        h = h_ref[...]
        gate = jnp.dot(h, wg_ref[j], preferred_element_type=F32)
        up = jnp.dot(h, wu_ref[j], preferred_element_type=F32)

        @pl.when(seq_start)
        def _():
            gbuf_ref[0:halo_rows, :] = jnp.zeros((halo_rows, gate.shape[1]), F32)

        @pl.when(jnp.logical_not(seq_start))
        def _():
            gbuf_ref[0:halo_rows, :] = halo_ref[j]

        gbuf_ref[halo_rows:, :] = gate
        halo_ref[j] = gate[tm - halo_rows:, :]
        cw = cw_ref[j]
        conv = cb_ref[j]
        for tap in range(CONV_W - 1):
            conv = conv + gbuf_ref[pl.ds(halo_rows - (CONV_W - 1) + tap, tm), :] * cw[tap:tap + 1]
        conv = conv + gate * cw[CONV_W - 1:CONV_W]
        act = (_silu(conv) * up).astype(BF16)
        acc_ref[...] += jnp.dot(act, wd_ref[j], preferred_element_type=F32)
        return carry

    lax.fori_loop(0, wg_ref.shape[0], chunk, 0)
    o_ref[...] = x + acc_ref[...]


def _ffn(x, g, w_gate, w_up, conv_w, conv_b, w_down, *, seq, tm, tf):
    n, d = x.shape
    dff = w_gate.shape[1]
    chunks = dff // tf
    by_chunk = lambda w: w.reshape(d, chunks, tf).transpose(1, 0, 2).astype(BF16)
    wg, wu = by_chunk(w_gate), by_chunk(w_up)
    wd = w_down.reshape(chunks, tf, d).astype(BF16)
    cw = conv_w.reshape(CONV_W, chunks, tf).transpose(1, 0, 2)
    cb = conv_b.reshape(chunks, 1, tf)
    vmem = (2 * (2 * tm * d * 4 + 3 * d * dff * 2 + 4 * dff * 4) + tm * d * 2 + tm * d * 4
            + (tm + 8) * tf * 4 + chunks * 8 * tf * 4 + 6 * tm * tf * 4)
    return pl.pallas_call(
        functools.partial(_ffn_kernel, tiles_per_seq=seq // tm),
        grid=(n // tm,),
        in_specs=[pl.BlockSpec((tm, d), lambda i: (i, 0)),
                  pl.BlockSpec((1, d), lambda i: (0, 0)),
                  pl.BlockSpec((chunks, d, tf), lambda i: (0, 0, 0)),
                  pl.BlockSpec((chunks, d, tf), lambda i: (0, 0, 0)),
                  pl.BlockSpec((chunks, CONV_W, tf), lambda i: (0, 0, 0)),
                  pl.BlockSpec((chunks, 1, tf), lambda i: (0, 0, 0)),
                  pl.BlockSpec((chunks, tf, d), lambda i: (0, 0, 0))],
        out_specs=pl.BlockSpec((tm, d), lambda i: (i, 0)),
        out_shape=jax.ShapeDtypeStruct((n, d), F32),
        scratch_shapes=[pltpu.VMEM((tm, d), BF16), pltpu.VMEM((tm, d), F32),
                        pltpu.VMEM((V7X_SUBLANES + tm, tf), F32),
                        pltpu.VMEM((chunks, V7X_SUBLANES, tf), F32)],
        compiler_params=_params(("arbitrary",), vmem),
        name="conv_ffn",
    )(x, g.reshape(1, d), wg, wu, cw, cb, wd)


def kernel(x, norm_mix_g, norm_ffn_g, ev_w_in, ev_a_ln_g, ev_a_ln_b, ev_a_ws, ev_a_bs, ev_q_g, ev_k_g,
           ev_w_out, od_w_in, od_w_a2, od_b_a, od_head_g, od_w_out, ffn_w_gate, ffn_w_up, ffn_conv_w,
           ffn_conv_b, ffn_w_down):
    bsz, seq, d = x.shape
    n = bsz * seq
    depth = norm_mix_g.shape[0]
    xf = x.reshape(n, d)
    for layer in range(depth):
        if layer % 2 == 0:
            e = layer // 2
            z = _norm_matmul(xf, norm_mix_g[layer], ev_w_in[e].astype(BF16), tm=1024, tn=1280)
            a_out = _gmlp(z, ev_a_ln_g[e], ev_a_ln_b[e], ev_a_ws[e], ev_a_bs[e], tm=512)
            b_out = _dilated(z.reshape(bsz, seq, -1), ev_q_g[e], ev_k_g[e]).reshape(n, -1)
            xf = _proj_residual(xf, a_out, 0, b_out, 0, ev_w_out[e].astype(BF16), tm=1024)
        else:
            o = layer // 2
            main = 2 * C_HEADS * (C_DK + C_DV)
            w_in = od_w_in[o]
            w_gate_in = jnp.pad(w_in[:, main:], ((0, 0), (0, V7X_LANES - C_GATE_RANK))).astype(BF16)
            z, ga = _norm_matmul(xf, norm_mix_g[layer], w_in[:, :main].astype(BF16), w_gate_in,
                                 tm=1024, tn=1024)
            mixed = _gla(z.reshape(bsz, seq, -1), ga.reshape(bsz, seq, -1), od_w_a2[o], od_b_a[o],
                         od_head_g[o], tc=512).reshape(n, -1)
            xf = _proj_residual(xf, mixed, 0, mixed, 1, od_w_out[o].astype(BF16), tm=1024)
        xf = _ffn(xf, norm_ffn_g[layer], ffn_w_gate[layer], ffn_w_up[layer], ffn_conv_w[layer],
                  ffn_conv_b[layer], ffn_w_down[layer], seq=seq, tm=512, tf=256)
    return xf.reshape(bsz, seq, d)
```

```python
import functools

import jax
import jax.numpy as jnp
from jax import lax
from jax.experimental import pallas as pl
from jax.experimental.pallas import tpu as pltpu

A_GROUPS = 8
A_GROUP_DIM = 64
A_CHUNK = 128
B_HEAD_DIM = 64
B_DILATIONS = (1, 4, 16)
B_BLOCK = 128
C_HEADS = 4
C_DK = 128
C_DV = 256
C_GATE_RANK = 16
C_TAU = 16.0
C_CHUNK = 64
CONV_W = 3
EPS = 1e-6
NEG = -1e30

V7X_LANES = 128
V7X_SUBLANES = 8
V7X_VMEM_BUDGET = 56 * 1024 * 1024

F32 = jnp.float32
BF16 = jnp.bfloat16
NT_DIMS = (((1,), (1,)), ((), ()))
TN_DIMS = (((0,), (0,)), ((), ()))


def _params(semantics, vmem_bytes):
    return pltpu.CompilerParams(
        dimension_semantics=semantics,
        vmem_limit_bytes=min(int(vmem_bytes * 1.25) + (4 << 20), V7X_VMEM_BUDGET))


def _rms_rows(x, g):
    return x * lax.rsqrt(jnp.mean(x * x, axis=-1, keepdims=True) + EPS) * g


def _gelu(x):
    return 0.5 * x * (1.0 + lax.erf(x * (0.5 ** 0.5)))


def _silu(x):
    return x * jax.nn.sigmoid(x)


def _norm_matmul_kernel(x_ref, g_ref, w_ref, o_ref, h_ref):
    @pl.when(pl.program_id(1) == 0)
    def _():
        h_ref[...] = _rms_rows(x_ref[...], g_ref[...]).astype(BF16)

    o_ref[...] = jnp.dot(h_ref[...], w_ref[...], preferred_element_type=F32)


def _norm_matmul_side_kernel(x_ref, g_ref, w_ref, ws_ref, o_ref, os_ref, h_ref):
    @pl.when(pl.program_id(1) == 0)
    def _():
        h = _rms_rows(x_ref[...], g_ref[...]).astype(BF16)
        h_ref[...] = h
        os_ref[...] = jnp.dot(h, ws_ref[...], preferred_element_type=F32)

    o_ref[...] = jnp.dot(h_ref[...], w_ref[...], preferred_element_type=F32)


def _norm_matmul(x, g, w, w_side=None, *, tm, tn):
    n, d = x.shape
    f = w.shape[1]
    in_specs = [pl.BlockSpec((tm, d), lambda i, j: (i, 0)),
                pl.BlockSpec((1, d), lambda i, j: (0, 0)),
                pl.BlockSpec((d, tn), lambda i, j: (0, j))]
    out_specs = pl.BlockSpec((tm, tn), lambda i, j: (i, j))
    out_shape = jax.ShapeDtypeStruct((n, f), F32)
    vmem = 2 * (tm * d * 4 + d * tn * 2 + tm * tn * 4) + tm * d * 2
    args = (x, g.reshape(1, d), w)
    body = _norm_matmul_kernel
    if w_side is not None:
        fs = w_side.shape[1]
        in_specs.append(pl.BlockSpec((d, fs), lambda i, j: (0, 0)))
        out_specs = (out_specs, pl.BlockSpec((tm, fs), lambda i, j: (i, 0)))
        out_shape = (out_shape, jax.ShapeDtypeStruct((n, fs), F32))
        vmem += 2 * (d * fs * 2 + tm * fs * 4)
        args += (w_side,)
        body = _norm_matmul_side_kernel
    return pl.pallas_call(
        body,
        grid=(n // tm, f // tn),
        in_specs=in_specs,
        out_specs=out_specs,
        out_shape=out_shape,
        scratch_shapes=[pltpu.VMEM((tm, d), BF16)],
        compiler_params=_params(("parallel", "arbitrary"), vmem),
        name="norm_matmul",
    )(*args)


def _proj_residual_kernel(x_ref, a_ref, b_ref, wa_ref, wb_ref, o_ref):
    acc = jnp.dot(a_ref[...].astype(BF16), wa_ref[...], preferred_element_type=F32)
    acc += jnp.dot(b_ref[...].astype(BF16), wb_ref[...], preferred_element_type=F32)
    o_ref[...] = x_ref[...] + acc


def _proj_residual(x, a, a_blk, b, b_blk, w, *, tm):
    n, d = x.shape
    kh = w.shape[0] // 2
    vmem = 2 * (2 * tm * d * 4 + 2 * tm * kh * 4 + 2 * kh * d * 2)
    return pl.pallas_call(
        _proj_residual_kernel,
        grid=(n // tm,),
        in_specs=[pl.BlockSpec((tm, d), lambda i: (i, 0)),
                  pl.BlockSpec((tm, kh), lambda i: (i, a_blk)),
                  pl.BlockSpec((tm, kh), lambda i: (i, b_blk)),
                  pl.BlockSpec((kh, d), lambda i: (0, 0)),
                  pl.BlockSpec((kh, d), lambda i: (1, 0))],
        out_specs=pl.BlockSpec((tm, d), lambda i: (i, 0)),
        out_shape=jax.ShapeDtypeStruct((n, d), F32),
        compiler_params=_params(("parallel",), vmem),
        name="proj_residual",
    )(x, a, b, w, w)


def _gmlp_kernel(u_ref, v_ref, lng_ref, lnb_ref, ws_ref, bs_ref, o_ref):
    tm = u_ref.shape[0]
    v = _gelu(v_ref[...])
    vc = v - jnp.mean(v, axis=-1, keepdims=True)
    vn = vc * lax.rsqrt(jnp.mean(vc * vc, axis=-1, keepdims=True) + EPS) * lng_ref[...] + lnb_ref[...]
    row = lax.broadcasted_iota(jnp.int32, (A_CHUNK, 2 * A_CHUNK), 0)
    col = lax.broadcasted_iota(jnp.int32, (A_CHUNK, 2 * A_CHUNK), 1)
    causal = (col % A_CHUNK) <= row
    lane = lax.broadcasted_iota(jnp.int32, (1, 2 * V7X_LANES), 1)
    first_group = (lane % V7X_LANES) < A_GROUP_DIM
    for t in range(ws_ref.shape[0]):
        lanes = slice(t * V7X_LANES, (t + 1) * V7X_LANES)
        w_pair = jnp.where(causal, ws_ref[t], 0.0).astype(BF16)
        bias = bs_ref[:, lanes]
        for p in range(tm // (2 * A_CHUNK)):
            r0 = slice(2 * p * A_CHUNK, (2 * p + 1) * A_CHUNK)
            r1 = slice((2 * p + 1) * A_CHUNK, (2 * p + 2) * A_CHUNK)
            cc = jnp.concatenate([vn[r0, lanes], vn[r1, lanes]], axis=1)
            rhs = jnp.concatenate([jnp.where(first_group, cc, 0.0),
                                   jnp.where(first_group, 0.0, cc)], axis=0).astype(BF16)
            mixed = jnp.dot(w_pair, rhs, preferred_element_type=F32)
            o_ref[r0, lanes] = _gelu(u_ref[r0, lanes]) * (mixed[:, :V7X_LANES] + bias)
            o_ref[r1, lanes] = _gelu(u_ref[r1, lanes]) * (mixed[:, V7X_LANES:] + bias)


def _gmlp(z, ln_g, ln_b, w_s, b_s, *, tm):
    n = z.shape[0]
    aw = A_GROUPS * A_GROUP_DIM
    pairs = A_GROUPS // 2
    ws_pairs = w_s.reshape(pairs, 2, A_CHUNK, A_CHUNK).transpose(0, 2, 1, 3).reshape(pairs, A_CHUNK, 2 * A_CHUNK)
    bias = jnp.repeat(b_s.T, A_GROUP_DIM, axis=1)
    vmem = 2 * (3 * tm * aw * 4 + pairs * A_CHUNK * 2 * A_CHUNK * 4 + A_CHUNK * aw * 4) + 4 * tm * aw * 4
    return pl.pallas_call(
        _gmlp_kernel,
        grid=(n // tm,),
        in_specs=[pl.BlockSpec((tm, aw), lambda i: (i, 0)),
                  pl.BlockSpec((tm, aw), lambda i: (i, 1)),
                  pl.BlockSpec((1, aw), lambda i: (0, 0)),
                  pl.BlockSpec((1, aw), lambda i: (0, 0)),
                  pl.BlockSpec((pairs, A_CHUNK, 2 * A_CHUNK), lambda i: (0, 0, 0)),
                  pl.BlockSpec((A_CHUNK, aw), lambda i: (0, 0))],
        out_specs=pl.BlockSpec((tm, aw), lambda i: (i, 0)),
        out_shape=jax.ShapeDtypeStruct((n, aw), F32),
        compiler_params=_params(("parallel",), vmem),
        name="gmlp",
    )(z, z, ln_g.reshape(1, aw), ln_b.reshape(1, aw), ws_pairs, bias)


B_PLANES = max(B_DILATIONS)


def _dilated_kernel(q_ref, k_ref, v_ref, qg_ref, kg_ref, o_ref,
                    qs_ref, ks_ref, vs_ref, ob_ref, lb_ref, mask_ref):
    m_rows = qs_ref.shape[1]
    blk2 = 2 * B_BLOCK
    lane = lax.broadcasted_iota(jnp.int32, (1, V7X_LANES), 1)
    head0 = lane < B_HEAD_DIM

    def head_rms(x, g):
        x2 = x * x
        s0 = jnp.sum(jnp.where(head0, x2, 0.0), axis=-1, keepdims=True)
        s1 = jnp.sum(jnp.where(head0, 0.0, x2), axis=-1, keepdims=True)
        ms = jnp.where(head0, s0, s1) * (1.0 / B_HEAD_DIM)
        return x * lax.rsqrt(ms + EPS) * g

    for r in range(B_PLANES):
        rows = pl.ds(r, m_rows, stride=B_PLANES)
        qs_ref[r] = head_rms(q_ref[rows, :], qg_ref[...]) * (B_HEAD_DIM ** -0.5)
        ks_ref[r] = head_rms(k_ref[rows, :], kg_ref[...])
        vs_ref[r] = v_ref[rows, :]

    rowi = lax.broadcasted_iota(jnp.int32, (blk2, blk2), 0) % B_BLOCK
    coli = lax.broadcasted_iota(jnp.int32, (blk2, blk2), 1)
    is_cur = coli >= B_BLOCK
    colj = coli % B_BLOCK
    for bi, d in enumerate(B_DILATIONS):
        planes = B_PLANES // d
        mb = B_BLOCK // planes
        i_pos = (rowi % mb) * planes + rowi // mb
        j_pos = (colj % mb) * planes + colj // mb
        band = jnp.where(jnp.where(is_cur, i_pos - j_pos, j_pos - i_pos) >= 0, 1.0, 0.0)
        mask_ref[2 * bi] = jnp.where(is_cur, band, 0.0)
        mask_ref[2 * bi + 1] = band

    for bi, d in enumerate(B_DILATIONS):
        planes = B_PLANES // d
        mb = B_BLOCK // planes
        nb = m_rows // mb

        def block(blk, carry, bi=bi, d=d, planes=planes, mb=mb, nb=nb):
            r = blk // nb
            n = blk % nb
            off = pl.multiple_of(n * mb, mb)
            off_prev = pl.multiple_of(jnp.maximum(n - 1, 0) * mb, mb)

            def gather(ref, o):
                return jnp.concatenate([ref[r + d * a, pl.ds(o, mb), :] for a in range(planes)], axis=0)

            qb = gather(qs_ref, off)
            q2 = jnp.concatenate([jnp.where(head0, qb, 0.0), jnp.where(head0, 0.0, qb)], axis=0).astype(BF16)
            kcat = jnp.concatenate([gather(ks_ref, off_prev), gather(ks_ref, off)], axis=0).astype(BF16)
            vcat = jnp.concatenate([gather(vs_ref, off_prev), gather(vs_ref, off)], axis=0).astype(BF16)
            s = lax.dot_general(q2, kcat, NT_DIMS, preferred_element_type=F32)
            valid = mask_ref[2 * bi + jnp.minimum(n, 1)] > 0.0
            s = jnp.where(valid, s, NEG)
            m = jnp.max(s, axis=-1, keepdims=True)
            p = jnp.exp(s - m)
            den = jnp.sum(p, axis=-1, keepdims=True)
            pv = jnp.dot(p.astype(BF16), vcat, preferred_element_type=F32)
            o2 = pv / den
            lse = m + jnp.log(den)
            o = jnp.where(head0, o2[:B_BLOCK], o2[B_BLOCK:])
            l = jnp.where(head0, lse[:B_BLOCK], lse[B_BLOCK:])
            for a in range(planes):
                ob_ref[bi, r + d * a, pl.ds(off, mb), :] = o[a * mb:(a + 1) * mb]
                lb_ref[bi, r + d * a, pl.ds(off, mb), :] = l[a * mb:(a + 1) * mb]
            return carry

        lax.fori_loop(0, d * nb, block, 0, unroll=8)

    for r in range(B_PLANES):
        ls = [lb_ref[bi, r] for bi in range(len(B_DILATIONS))]
        mx = functools.reduce(jnp.maximum, ls)
        es = [jnp.exp(l - mx) for l in ls]
        tot = functools.reduce(lambda a, b: a + b, es)
        out = functools.reduce(lambda a, b: a + b,
                               [(es[bi] / tot) * ob_ref[bi, r] for bi in range(len(B_DILATIONS))])
        o_ref[pl.ds(r, m_rows, stride=B_PLANES), :] = out


def _dilated(z3, q_g, k_g):
    bsz, s, _ = z3.shape
    bw = z3.shape[2] // 5
    pairs = bw // V7X_LANES
    m_rows = s // B_PLANES
    nbr = len(B_DILATIONS)
    qg2 = jnp.tile(q_g, 2).reshape(1, V7X_LANES)
    kg2 = jnp.tile(k_g, 2).reshape(1, V7X_LANES)
    tile_bytes = s * V7X_LANES * 4
    vmem = 2 * 4 * tile_bytes + 3 * tile_bytes + 2 * nbr * tile_bytes + 2 * nbr * 4 * B_BLOCK * B_BLOCK * 4
    qkv_spec = lambda base: pl.BlockSpec((None, s, V7X_LANES), lambda b, p: (b, 0, base + p))
    return pl.pallas_call(
        _dilated_kernel,
        grid=(bsz, pairs),
        in_specs=[qkv_spec(2 * pairs), qkv_spec(3 * pairs), qkv_spec(4 * pairs),
                  pl.BlockSpec((1, V7X_LANES), lambda b, p: (0, 0)),
                  pl.BlockSpec((1, V7X_LANES), lambda b, p: (0, 0))],
        out_specs=pl.BlockSpec((None, s, V7X_LANES), lambda b, p: (b, 0, p)),
        out_shape=jax.ShapeDtypeStruct((bsz, s, bw), F32),
        scratch_shapes=[pltpu.VMEM((B_PLANES, m_rows, V7X_LANES), F32)] * 3
                       + [pltpu.VMEM((nbr, B_PLANES, m_rows, V7X_LANES), F32)] * 2
                       + [pltpu.VMEM((2 * nbr, 2 * B_BLOCK, 2 * B_BLOCK), F32)],
        compiler_params=_params(("parallel", "parallel"), vmem),
        name="dilated_attention",
    )(z3, z3, z3, qg2, kg2)


def _gla_kernel(q_ref, k_ref, v_ref, r_ref, ga_ref, wa_ref, ba_ref, hg_ref, o_ref, st_ref):
    tc = q_ref.shape[0]
    chunks = tc // C_CHUNK

    @pl.when(pl.program_id(1) == 0)
    def _():
        st_ref[...] = jnp.zeros_like(st_ref)

    gate = jnp.dot(ga_ref[...].astype(BF16), wa_ref[...], preferred_element_type=F32) + ba_ref[...]
    log_a = jax.nn.log_sigmoid(gate) / C_TAU
    ci = lax.broadcasted_iota(jnp.int32, (C_CHUNK, C_CHUNK), 0)
    cj = lax.broadcasted_iota(jnp.int32, (C_CHUNK, C_CHUNK), 1)
    causal = ci >= cj
    tri = jnp.where(causal, 1.0, 0.0).astype(BF16)
    rows = [slice(c * C_CHUNK, (c + 1) * C_CHUNK) for c in range(chunks)]
    for hd in range(C_HEADS):
        kl = slice(hd * C_DK, (hd + 1) * C_DK)
        vl = slice(hd * C_DV, (hd + 1) * C_DV)
        la = log_a[:, kl]
        hi = la.astype(BF16)
        rest = la - hi.astype(F32)
        mid = rest.astype(BF16)
        lo = (rest - mid.astype(F32)).astype(BF16)
        pieces = jnp.concatenate([hi, mid, lo], axis=1)
        sums = [jnp.dot(tri, pieces[sl], preferred_element_type=F32) for sl in rows]
        b = jnp.concatenate([s3[:, :C_DK] + s3[:, C_DK:2 * C_DK] + s3[:, 2 * C_DK:] for s3 in sums], axis=0)
        b3 = b.reshape(chunks, C_CHUNK, C_DK)
        b_last = b3[:, C_CHUNK - 1:C_CHUNK, :]
        k = k_ref[:, kl]
        q_t = ((q_ref[:, kl] * (C_DK ** -0.5)) * jnp.exp(b)).astype(BF16)
        k_t = (k * jnp.exp(-b)).astype(BF16)
        k_s = (k.reshape(chunks, C_CHUNK, C_DK) * jnp.exp(b_last - b3)).reshape(tc, C_DK).astype(BF16)
        decay = jnp.exp(b_last)
        o_intra, kv_t = [], []
        for sl in rows:
            v_c = v_ref[sl, vl].astype(BF16)
            attn = lax.dot_general(q_t[sl], k_t[sl], NT_DIMS, preferred_element_type=F32)
            attn = jnp.where(causal, attn, 0.0).astype(BF16)
            o_intra.append(jnp.dot(attn, v_c, preferred_element_type=F32))
            kv_t.append(lax.dot_general(v_c, k_s[sl], TN_DIMS, preferred_element_type=F32))
        st = st_ref[hd]
        entering = []
        for c in range(chunks):
            entering.append(st.astype(BF16))
            st = st * decay[c] + kv_t[c]
        st_ref[hd] = st
        for c, sl in enumerate(rows):
            o = o_intra[c] + lax.dot_general(q_t[sl], entering[c], NT_DIMS, preferred_element_type=F32)
            o_ref[sl, vl] = _rms_rows(o, hg_ref[...]) * _silu(r_ref[sl, vl])


def _gla(z3, ga3, w_a2, b_a, head_g, *, tc):
    bsz, s, _ = z3.shape
    hk, hv = C_HEADS * C_DK, C_HEADS * C_DV
    wa = jnp.pad(w_a2, ((0, V7X_LANES - C_GATE_RANK), (0, 0))).astype(BF16)
    vmem = (2 * (2 * tc * hk * 4 + 3 * tc * hv * 4 + tc * V7X_LANES * 4 + V7X_LANES * hk * 2)
            + hv * C_DK * 4 + 12 * tc * C_DK * 4 + 2 * tc * hk * 4)
    return pl.pallas_call(
        _gla_kernel,
        grid=(bsz, s // tc),
        in_specs=[pl.BlockSpec((None, tc, hk), lambda b, t: (b, t, 0)),
                  pl.BlockSpec((None, tc, hk), lambda b, t: (b, t, 1)),
                  pl.BlockSpec((None, tc, hv), lambda b, t: (b, t, 1)),
                  pl.BlockSpec((None, tc, hv), lambda b, t: (b, t, 2)),
                  pl.BlockSpec((None, tc, V7X_LANES), lambda b, t: (b, t, 0)),
                  pl.BlockSpec((V7X_LANES, hk), lambda b, t: (0, 0)),
                  pl.BlockSpec((1, hk), lambda b, t: (0, 0)),
                  pl.BlockSpec((1, C_DV), lambda b, t: (0, 0))],
        out_specs=pl.BlockSpec((None, tc, hv), lambda b, t: (b, t, 0)),
        out_shape=jax.ShapeDtypeStruct((bsz, s, hv), F32),
        scratch_shapes=[pltpu.VMEM((C_HEADS, C_DV, C_DK), F32)],
        compiler_params=_params(("parallel", "arbitrary"), vmem),
        name="gla",
    )(z3, z3, z3, z3, ga3, wa, b_a.reshape(1, -1), head_g.reshape(1, C_DV))


def _ffn_kernel(x_ref, g_ref, wg_ref, wu_ref, cw_ref, cb_ref, wd_ref, o_ref,
                h_ref, act_ref, gbuf_ref, halo_ref, *, tiles_per_seq):
    tm = x_ref.shape[0]
    tf = wg_ref.shape[2]
    halo_rows = V7X_SUBLANES
    h_ref[...] = _rms_rows(x_ref[...], g_ref[...]).astype(BF16)

    @pl.when(pl.program_id(0) % tiles_per_seq == 0)
    def _():
        halo_ref[...] = jnp.zeros_like(halo_ref)

    for j in range(wg_ref.shape[0]):
        h = h_ref[...]
        gate = jnp.dot(h, wg_ref[j], preferred_element_type=F32)
        up = jnp.dot(h, wu_ref[j], preferred_element_type=F32)
        gbuf = gbuf_ref.at[j % 2]
        gbuf[0:halo_rows, :] = halo_ref[j]
        gbuf[halo_rows:, :] = gate
        halo_ref[j] = gate[tm - halo_rows:, :]
        cw = cw_ref[j]
        conv = cb_ref[j]
        for tap in range(CONV_W - 1):
            conv = conv + gbuf[pl.ds(halo_rows - (CONV_W - 1) + tap, tm), :] * cw[tap:tap + 1]
        conv = conv + gate * cw[CONV_W - 1:CONV_W]
        act_ref[:, j * tf:(j + 1) * tf] = (_silu(conv) * up).astype(BF16)
    o_ref[...] = x_ref[...] + jnp.dot(act_ref[...], wd_ref[...], preferred_element_type=F32)


def _ffn(x, g, w_gate, w_up, conv_w, conv_b, w_down, *, seq, tm, tf):
    n, d = x.shape
    dff = w_gate.shape[1]
    chunks = dff // tf
    by_chunk = lambda w: w.reshape(d, chunks, tf).transpose(1, 0, 2).astype(BF16)
    wg, wu = by_chunk(w_gate), by_chunk(w_up)
    wd = w_down.astype(BF16)
    cw = conv_w.reshape(CONV_W, chunks, tf).transpose(1, 0, 2)
    cb = conv_b.reshape(chunks, 1, tf)
    vmem = (2 * (2 * tm * d * 4 + 3 * d * dff * 2 + 4 * dff * 4) + tm * d * 2 + tm * dff * 2
            + 2 * (tm + 8) * tf * 4 + chunks * 8 * tf * 4 + 8 * tm * tf * 4)
    return pl.pallas_call(
        functools.partial(_ffn_kernel, tiles_per_seq=seq // tm),
        grid=(n // tm,),
        in_specs=[pl.BlockSpec((tm, d), lambda i: (i, 0)),
                  pl.BlockSpec((1, d), lambda i: (0, 0)),
                  pl.BlockSpec((chunks, d, tf), lambda i: (0, 0, 0)),
                  pl.BlockSpec((chunks, d, tf), lambda i: (0, 0, 0)),
                  pl.BlockSpec((chunks, CONV_W, tf), lambda i: (0, 0, 0)),
                  pl.BlockSpec((chunks, 1, tf), lambda i: (0, 0, 0)),
                  pl.BlockSpec((dff, d), lambda i: (0, 0))],
        out_specs=pl.BlockSpec((tm, d), lambda i: (i, 0)),
        out_shape=jax.ShapeDtypeStruct((n, d), F32),
        scratch_shapes=[pltpu.VMEM((tm, d), BF16), pltpu.VMEM((tm, dff), BF16),
                        pltpu.VMEM((2, V7X_SUBLANES + tm, tf), F32),
                        pltpu.VMEM((chunks, V7X_SUBLANES, tf), F32)],
        compiler_params=_params(("arbitrary",), vmem),
        name="conv_ffn",
    )(x, g.reshape(1, d), wg, wu, cw, cb, wd)


def kernel(x, norm_mix_g, norm_ffn_g, ev_w_in, ev_a_ln_g, ev_a_ln_b, ev_a_ws, ev_a_bs, ev_q_g, ev_k_g,
           ev_w_out, od_w_in, od_w_a2, od_b_a, od_head_g, od_w_out, ffn_w_gate, ffn_w_up, ffn_conv_w,
           ffn_conv_b, ffn_w_down):
    bsz, seq, d = x.shape
    n = bsz * seq
    depth = norm_mix_g.shape[0]
    xf = x.reshape(n, d)
    for layer in range(depth):
        if layer % 2 == 0:
            e = layer // 2
            z = _norm_matmul(xf, norm_mix_g[layer], ev_w_in[e].astype(BF16), tm=1024, tn=1280)
            a_out = _gmlp(z, ev_a_ln_g[e], ev_a_ln_b[e], ev_a_ws[e], ev_a_bs[e], tm=512)
            b_out = _dilated(z.reshape(bsz, seq, -1), ev_q_g[e], ev_k_g[e]).reshape(n, -1)
            xf = _proj_residual(xf, a_out, 0, b_out, 0, ev_w_out[e].astype(BF16), tm=1024)
        else:
            o = layer // 2
            main = 2 * C_HEADS * (C_DK + C_DV)
            w_in = od_w_in[o]
            w_gate_in = jnp.pad(w_in[:, main:], ((0, 0), (0, V7X_LANES - C_GATE_RANK))).astype(BF16)
            z, ga = _norm_matmul(xf, norm_mix_g[layer], w_in[:, :main].astype(BF16), w_gate_in,
                                 tm=1024, tn=1024)
            mixed = _gla(z.reshape(bsz, seq, -1), ga.reshape(bsz, seq, -1), od_w_a2[o], od_b_a[o],
                         od_head_g[o], tc=512).reshape(n, -1)
            xf = _proj_residual(xf, mixed, 0, mixed, 1, od_w_out[o].astype(BF16), tm=1024)
        xf = _ffn(xf, norm_ffn_g[layer], ffn_w_gate[layer], ffn_w_up[layer], ffn_conv_w[layer],
                  ffn_conv_b[layer], ffn_w_down[layer], seq=seq, tm=512, tf=256)
    return xf.reshape(bsz, seq, d)
```

```python
import functools

import jax
import jax.numpy as jnp
from jax import lax
from jax.experimental import pallas as pl
from jax.experimental.pallas import tpu as pltpu

A_GROUPS = 8
A_GROUP_DIM = 64
A_CHUNK = 128
B_HEAD_DIM = 64
B_DILATIONS = (1, 4, 16)
B_BLOCK = 128
C_HEADS = 4
C_DK = 128
C_DV = 256
C_GATE_RANK = 16
C_TAU = 16.0
C_CHUNK = 64
CONV_W = 3
EPS = 1e-6
NEG = -1e30

V7X_LANES = 128
V7X_SUBLANES = 8
V7X_VMEM_BUDGET = 56 * 1024 * 1024

F32 = jnp.float32
BF16 = jnp.bfloat16
NT_DIMS = (((1,), (1,)), ((), ()))
TN_DIMS = (((0,), (0,)), ((), ()))


def _params(semantics, vmem_bytes):
    return pltpu.CompilerParams(
        dimension_semantics=semantics,
        vmem_limit_bytes=min(int(vmem_bytes * 1.25) + (4 << 20), V7X_VMEM_BUDGET))


def _rms_rows(x, g):
    return x * lax.rsqrt(jnp.mean(x * x, axis=-1, keepdims=True) + EPS) * g


def _gelu(x):
    return 0.5 * x * (1.0 + lax.erf(x * (0.5 ** 0.5)))


def _silu(x):
    return x * jax.nn.sigmoid(x)


ROW_SUB = 256


def _norm_matmul_kernel(x_ref, g_ref, w_ref, *rest):
    ws_ref, o_ref, os_ref = (rest[0], rest[1], rest[2]) if len(rest) == 3 else (None, rest[0], None)
    for r in range(x_ref.shape[0] // ROW_SUB):
        rows = slice(r * ROW_SUB, (r + 1) * ROW_SUB)
        h = _rms_rows(x_ref[rows, :], g_ref[...]).astype(BF16)
        o_ref[rows, :] = jnp.dot(h, w_ref[...], preferred_element_type=F32)
        if ws_ref is not None:
            os_ref[rows, :] = jnp.dot(h, ws_ref[...], preferred_element_type=F32)


def _norm_matmul(x, g, w, w_side=None, *, tm):
    n, d = x.shape
    f = w.shape[1]
    in_specs = [pl.BlockSpec((tm, d), lambda i: (i, 0)),
                pl.BlockSpec((1, d), lambda i: (0, 0)),
                pl.BlockSpec((d, f), lambda i: (0, 0))]
    out_specs = pl.BlockSpec((tm, f), lambda i: (i, 0))
    out_shape = jax.ShapeDtypeStruct((n, f), F32)
    vmem = 2 * (tm * d * 4 + d * f * 2 + tm * f * 4) + 2 * ROW_SUB * f * 4
    args = (x, g.reshape(1, d), w)
    if w_side is not None:
        fs = w_side.shape[1]
        in_specs.append(pl.BlockSpec((d, fs), lambda i: (0, 0)))
        out_specs = (out_specs, pl.BlockSpec((tm, fs), lambda i: (i, 0)))
        out_shape = (out_shape, jax.ShapeDtypeStruct((n, fs), F32))
        vmem += 2 * (d * fs * 2 + tm * fs * 4)
        args += (w_side,)
    return pl.pallas_call(
        _norm_matmul_kernel,
        grid=(n // tm,),
        in_specs=in_specs,
        out_specs=out_specs,
        out_shape=out_shape,
        compiler_params=_params(("parallel",), vmem),
        name="norm_matmul",
    )(*args)


def _gmlp_kernel(u_ref, v_ref, lng_ref, lnb_ref, ws_ref, bs_ref, o_ref):
    tm = u_ref.shape[0]
    v = _gelu(v_ref[...])
    vc = v - jnp.mean(v, axis=-1, keepdims=True)
    vn = vc * lax.rsqrt(jnp.mean(vc * vc, axis=-1, keepdims=True) + EPS) * lng_ref[...] + lnb_ref[...]
    row = lax.broadcasted_iota(jnp.int32, (A_CHUNK, 2 * A_CHUNK), 0)
    col = lax.broadcasted_iota(jnp.int32, (A_CHUNK, 2 * A_CHUNK), 1)
    causal = (col % A_CHUNK) <= row
    lane = lax.broadcasted_iota(jnp.int32, (1, 2 * V7X_LANES), 1)
    first_group = (lane % V7X_LANES) < A_GROUP_DIM
    for t in range(ws_ref.shape[0]):
        lanes = slice(t * V7X_LANES, (t + 1) * V7X_LANES)
        w_pair = jnp.where(causal, ws_ref[t], 0.0).astype(BF16)
        bias = bs_ref[:, lanes]
        for p in range(tm // (2 * A_CHUNK)):
            r0 = slice(2 * p * A_CHUNK, (2 * p + 1) * A_CHUNK)
            r1 = slice((2 * p + 1) * A_CHUNK, (2 * p + 2) * A_CHUNK)
            cc = jnp.concatenate([vn[r0, lanes], vn[r1, lanes]], axis=1)
            rhs = jnp.concatenate([jnp.where(first_group, cc, 0.0),
                                   jnp.where(first_group, 0.0, cc)], axis=0).astype(BF16)
            mixed = jnp.dot(w_pair, rhs, preferred_element_type=F32)
            o_ref[r0, lanes] = _gelu(u_ref[r0, lanes]) * (mixed[:, :V7X_LANES] + bias)
            o_ref[r1, lanes] = _gelu(u_ref[r1, lanes]) * (mixed[:, V7X_LANES:] + bias)


def _gmlp(z, ln_g, ln_b, w_s, b_s, *, tm):
    n = z.shape[0]
    aw = A_GROUPS * A_GROUP_DIM
    pairs = A_GROUPS // 2
    ws_pairs = w_s.reshape(pairs, 2, A_CHUNK, A_CHUNK).transpose(0, 2, 1, 3).reshape(pairs, A_CHUNK, 2 * A_CHUNK)
    bias = jnp.repeat(b_s.T, A_GROUP_DIM, axis=1)
    vmem = 2 * (3 * tm * aw * 4 + pairs * A_CHUNK * 2 * A_CHUNK * 4 + A_CHUNK * aw * 4) + 4 * tm * aw * 4
    return pl.pallas_call(
        _gmlp_kernel,
        grid=(n // tm,),
        in_specs=[pl.BlockSpec((tm, aw), lambda i: (i, 0)),
                  pl.BlockSpec((tm, aw), lambda i: (i, 1)),
                  pl.BlockSpec((1, aw), lambda i: (0, 0)),
                  pl.BlockSpec((1, aw), lambda i: (0, 0)),
                  pl.BlockSpec((pairs, A_CHUNK, 2 * A_CHUNK), lambda i: (0, 0, 0)),
                  pl.BlockSpec((A_CHUNK, aw), lambda i: (0, 0))],
        out_specs=pl.BlockSpec((tm, aw), lambda i: (i, 0)),
        out_shape=jax.ShapeDtypeStruct((n, aw), F32),
        compiler_params=_params(("parallel",), vmem),
        name="gmlp",
    )(z, z, ln_g.reshape(1, aw), ln_b.reshape(1, aw), ws_pairs, bias)


B_PLANES = max(B_DILATIONS)


def _dilated_kernel(q_ref, k_ref, v_ref, qg_ref, kg_ref, o_ref,
                    qs_ref, ks_ref, vs_ref, ob_ref, lb_ref, mask_ref):
    m_rows = qs_ref.shape[1]
    blk2 = 2 * B_BLOCK
    lane = lax.broadcasted_iota(jnp.int32, (1, V7X_LANES), 1)
    head0 = lane < B_HEAD_DIM

    def head_rms(x, g):
        x2 = x * x
        s0 = jnp.sum(jnp.where(head0, x2, 0.0), axis=-1, keepdims=True)
        s1 = jnp.sum(jnp.where(head0, 0.0, x2), axis=-1, keepdims=True)
        ms = jnp.where(head0, s0, s1) * (1.0 / B_HEAD_DIM)
        return x * lax.rsqrt(ms + EPS) * g

    for r in range(B_PLANES):
        rows = pl.ds(r, m_rows, stride=B_PLANES)
        qs_ref[r] = head_rms(q_ref[rows, :], qg_ref[...]) * (B_HEAD_DIM ** -0.5)
        ks_ref[r] = head_rms(k_ref[rows, :], kg_ref[...])
        vs_ref[r] = v_ref[rows, :]

    rowi = lax.broadcasted_iota(jnp.int32, (blk2, blk2), 0) % B_BLOCK
    coli = lax.broadcasted_iota(jnp.int32, (blk2, blk2), 1)
    is_cur = coli >= B_BLOCK
    colj = coli % B_BLOCK
    for bi, d in enumerate(B_DILATIONS):
        planes = B_PLANES // d
        mb = B_BLOCK // planes
        i_pos = (rowi % mb) * planes + rowi // mb
        j_pos = (colj % mb) * planes + colj // mb
        band = jnp.where(jnp.where(is_cur, i_pos - j_pos, j_pos - i_pos) >= 0, 1.0, 0.0)
        mask_ref[2 * bi] = jnp.where(is_cur, band, 0.0)
        mask_ref[2 * bi + 1] = band

    for bi, d in enumerate(B_DILATIONS):
        planes = B_PLANES // d
        mb = B_BLOCK // planes
        nb = m_rows // mb

        def block(blk, carry, bi=bi, d=d, planes=planes, mb=mb, nb=nb):
            r = blk // nb
            n = blk % nb
            off = pl.multiple_of(n * mb, mb)
            off_prev = pl.multiple_of(jnp.maximum(n - 1, 0) * mb, mb)

            def gather(ref, o):
                return jnp.concatenate([ref[r + d * a, pl.ds(o, mb), :] for a in range(planes)], axis=0)

            qb = gather(qs_ref, off)
            q2 = jnp.concatenate([jnp.where(head0, qb, 0.0), jnp.where(head0, 0.0, qb)], axis=0).astype(BF16)
            kcat = jnp.concatenate([gather(ks_ref, off_prev), gather(ks_ref, off)], axis=0).astype(BF16)
            vcat = jnp.concatenate([gather(vs_ref, off_prev), gather(vs_ref, off)], axis=0).astype(BF16)
            s = lax.dot_general(q2, kcat, NT_DIMS, preferred_element_type=F32)
            valid = mask_ref[2 * bi + jnp.minimum(n, 1)] > 0.0
            s = jnp.where(valid, s, NEG)
            m = jnp.max(s, axis=-1, keepdims=True)
            p = jnp.exp(s - m)
            den = jnp.sum(p, axis=-1, keepdims=True)
            pv = jnp.dot(p.astype(BF16), vcat, preferred_element_type=F32)
            o2 = pv / den
            lse = m + jnp.log(den)
            o = jnp.where(head0, o2[:B_BLOCK], o2[B_BLOCK:])
            l = jnp.where(head0, lse[:B_BLOCK], lse[B_BLOCK:])
            for a in range(planes):
                ob_ref[bi, r + d * a, pl.ds(off, mb), :] = o[a * mb:(a + 1) * mb]
                lb_ref[bi, r + d * a, pl.ds(off, mb), :] = l[a * mb:(a + 1) * mb]
            return carry

        lax.fori_loop(0, d * nb, block, 0, unroll=8)

    for r in range(B_PLANES):
        ls = [lb_ref[bi, r] for bi in range(len(B_DILATIONS))]
        mx = functools.reduce(jnp.maximum, ls)
        es = [jnp.exp(l - mx) for l in ls]
        tot = functools.reduce(lambda a, b: a + b, es)
        out = functools.reduce(lambda a, b: a + b,
                               [(es[bi] / tot) * ob_ref[bi, r] for bi in range(len(B_DILATIONS))])
        o_ref[pl.ds(r, m_rows, stride=B_PLANES), :] = out


def _dilated(z3, q_g, k_g):
    bsz, s, _ = z3.shape
    bw = z3.shape[2] // 5
    pairs = bw // V7X_LANES
    m_rows = s // B_PLANES
    nbr = len(B_DILATIONS)
    qg2 = jnp.tile(q_g, 2).reshape(1, V7X_LANES)
    kg2 = jnp.tile(k_g, 2).reshape(1, V7X_LANES)
    tile_bytes = s * V7X_LANES * 4
    vmem = 2 * 4 * tile_bytes + 3 * tile_bytes + 2 * nbr * tile_bytes + 2 * nbr * 4 * B_BLOCK * B_BLOCK * 4
    qkv_spec = lambda base: pl.BlockSpec((None, s, V7X_LANES), lambda b, p: (b, 0, base + p))
    return pl.pallas_call(
        _dilated_kernel,
        grid=(bsz, pairs),
        in_specs=[qkv_spec(2 * pairs), qkv_spec(3 * pairs), qkv_spec(4 * pairs),
                  pl.BlockSpec((1, V7X_LANES), lambda b, p: (0, 0)),
                  pl.BlockSpec((1, V7X_LANES), lambda b, p: (0, 0))],
        out_specs=pl.BlockSpec((None, s, V7X_LANES), lambda b, p: (b, 0, p)),
        out_shape=jax.ShapeDtypeStruct((bsz, s, bw), F32),
        scratch_shapes=[pltpu.VMEM((B_PLANES, m_rows, V7X_LANES), F32)] * 3
                       + [pltpu.VMEM((nbr, B_PLANES, m_rows, V7X_LANES), F32)] * 2
                       + [pltpu.VMEM((2 * nbr, 2 * B_BLOCK, 2 * B_BLOCK), F32)],
        compiler_params=_params(("parallel", "parallel"), vmem),
        name="dilated_attention",
    )(z3, z3, z3, qg2, kg2)


def _gla_kernel(q_ref, k_ref, v_ref, r_ref, ga_ref, wa_ref, ba_ref, hg_ref, o_ref, st_ref):
    tc = q_ref.shape[0]
    chunks = tc // C_CHUNK

    @pl.when(pl.program_id(1) == 0)
    def _():
        st_ref[...] = jnp.zeros_like(st_ref)

    gate = jnp.dot(ga_ref[...].astype(BF16), wa_ref[...], preferred_element_type=F32) + ba_ref[...]
    log_a = jax.nn.log_sigmoid(gate) / C_TAU
    ci = lax.broadcasted_iota(jnp.int32, (C_CHUNK, C_CHUNK), 0)
    cj = lax.broadcasted_iota(jnp.int32, (C_CHUNK, C_CHUNK), 1)
    causal = ci >= cj
    tri = jnp.where(causal, 1.0, 0.0).astype(BF16)
    rows = [slice(c * C_CHUNK, (c + 1) * C_CHUNK) for c in range(chunks)]
    for hd in range(C_HEADS):
        kl = slice(hd * C_DK, (hd + 1) * C_DK)
        vl = slice(hd * C_DV, (hd + 1) * C_DV)
        la = log_a[:, kl]
        hi = la.astype(BF16)
        rest = la - hi.astype(F32)
        mid = rest.astype(BF16)
        lo = (rest - mid.astype(F32)).astype(BF16)
        pieces = jnp.concatenate([hi, mid, lo], axis=1)
        sums = [jnp.dot(tri, pieces[sl], preferred_element_type=F32) for sl in rows]
        b = jnp.concatenate([s3[:, :C_DK] + s3[:, C_DK:2 * C_DK] + s3[:, 2 * C_DK:] for s3 in sums], axis=0)
        b3 = b.reshape(chunks, C_CHUNK, C_DK)
        b_last = b3[:, C_CHUNK - 1:C_CHUNK, :]
        k = k_ref[:, kl]
        q_t = ((q_ref[:, kl] * (C_DK ** -0.5)) * jnp.exp(b)).astype(BF16)
        k_t = (k * jnp.exp(-b)).astype(BF16)
        k_s = (k.reshape(chunks, C_CHUNK, C_DK) * jnp.exp(b_last - b3)).reshape(tc, C_DK).astype(BF16)
        decay = jnp.exp(b_last)
        o_intra, kv_t = [], []
        for sl in rows:
            v_c = v_ref[sl, vl].astype(BF16)
            attn = lax.dot_general(q_t[sl], k_t[sl], NT_DIMS, preferred_element_type=F32)
            attn = jnp.where(causal, attn, 0.0).astype(BF16)
            o_intra.append(jnp.dot(attn, v_c, preferred_element_type=F32))
            kv_t.append(lax.dot_general(v_c, k_s[sl], TN_DIMS, preferred_element_type=F32))
        st = st_ref[hd]
        entering = []
        for c in range(chunks):
            entering.append(st.astype(BF16))
            st = st * decay[c] + kv_t[c]
        st_ref[hd] = st
        for c, sl in enumerate(rows):
            o = o_intra[c] + lax.dot_general(q_t[sl], entering[c], NT_DIMS, preferred_element_type=F32)
            o_ref[sl, vl] = _rms_rows(o, hg_ref[...]) * _silu(r_ref[sl, vl])


def _gla(z3, ga3, w_a2, b_a, head_g, *, tc):
    bsz, s, _ = z3.shape
    hk, hv = C_HEADS * C_DK, C_HEADS * C_DV
    wa = jnp.pad(w_a2, ((0, V7X_LANES - C_GATE_RANK), (0, 0))).astype(BF16)
    vmem = (2 * (2 * tc * hk * 4 + 3 * tc * hv * 4 + tc * V7X_LANES * 4 + V7X_LANES * hk * 2)
            + hv * C_DK * 4 + 12 * tc * C_DK * 4 + 2 * tc * hk * 4)
    return pl.pallas_call(
        _gla_kernel,
        grid=(bsz, s // tc),
        in_specs=[pl.BlockSpec((None, tc, hk), lambda b, t: (b, t, 0)),
                  pl.BlockSpec((None, tc, hk), lambda b, t: (b, t, 1)),
                  pl.BlockSpec((None, tc, hv), lambda b, t: (b, t, 1)),
                  pl.BlockSpec((None, tc, hv), lambda b, t: (b, t, 2)),
                  pl.BlockSpec((None, tc, V7X_LANES), lambda b, t: (b, t, 0)),
                  pl.BlockSpec((V7X_LANES, hk), lambda b, t: (0, 0)),
                  pl.BlockSpec((1, hk), lambda b, t: (0, 0)),
                  pl.BlockSpec((1, C_DV), lambda b, t: (0, 0))],
        out_specs=pl.BlockSpec((None, tc, hv), lambda b, t: (b, t, 0)),
        out_shape=jax.ShapeDtypeStruct((bsz, s, hv), F32),
        scratch_shapes=[pltpu.VMEM((C_HEADS, C_DV, C_DK), F32)],
        compiler_params=_params(("parallel", "arbitrary"), vmem),
        name="gla",
    )(z3, z3, z3, z3, ga3, wa, b_a.reshape(1, -1), head_g.reshape(1, C_DV))


def _ffn_kernel(x_ref, a_ref, b_ref, wa_ref, wb_ref, g_ref, wg_ref, wu_ref, cw_ref, cb_ref, wd_ref, o_ref,
                h_ref, act_ref, gbuf_ref, halo_ref, *, tiles_per_seq, tf):
    tm = x_ref.shape[0]
    halo_rows = V7X_SUBLANES
    for r in range(tm // ROW_SUB):
        rows = slice(r * ROW_SUB, (r + 1) * ROW_SUB)
        x1 = x_ref[rows, :] + jnp.dot(a_ref[rows, :].astype(BF16), wa_ref[...], preferred_element_type=F32)
        x1 = x1 + jnp.dot(b_ref[rows, :].astype(BF16), wb_ref[...], preferred_element_type=F32)
        o_ref[rows, :] = x1
        h_ref[rows, :] = _rms_rows(x1, g_ref[...]).astype(BF16)

    @pl.when(pl.program_id(0) % tiles_per_seq == 0)
    def _():
        halo_ref[...] = jnp.zeros_like(halo_ref)

    for j in range(wg_ref.shape[1] // tf):
        cols = slice(j * tf, (j + 1) * tf)
        h = h_ref[...]
        gate = jnp.dot(h, wg_ref[:, cols], preferred_element_type=F32)
        up = jnp.dot(h, wu_ref[:, cols], preferred_element_type=F32)
        gbuf = gbuf_ref.at[j % 2]
        gbuf[0:halo_rows, :] = halo_ref[:, cols]
        gbuf[halo_rows:, :] = gate
        halo_ref[:, cols] = gate[tm - halo_rows:, :]
        conv = cb_ref[:, cols]
        for tap in range(CONV_W - 1):
            conv = conv + gbuf[pl.ds(halo_rows - (CONV_W - 1) + tap, tm), :] * cw_ref[tap:tap + 1, cols]
        conv = conv + gate * cw_ref[CONV_W - 1:CONV_W, cols]
        act_ref[:, cols] = (_silu(conv) * up).astype(BF16)
    o_ref[...] += jnp.dot(act_ref[...], wd_ref[...], preferred_element_type=F32)


def _ffn(x, a, a_blk, b, b_blk, w_out, g, w_gate, w_up, conv_w, conv_b, w_down, *, seq, tm, tf):
    n, d = x.shape
    dff = w_gate.shape[1]
    kh = w_out.shape[0] // 2
    vmem = (2 * (2 * tm * d * 4 + 2 * tm * kh * 4 + 2 * kh * d * 2 + 3 * d * dff * 2 + 4 * dff * 4)
            + tm * d * 2 + tm * dff * 2 + 2 * (tm + 8) * tf * 4 + 8 * dff * 4 + 8 * tm * tf * 4)
    return pl.pallas_call(
        functools.partial(_ffn_kernel, tiles_per_seq=seq // tm, tf=tf),
        grid=(n // tm,),
        in_specs=[pl.BlockSpec((tm, d), lambda i: (i, 0)),
                  pl.BlockSpec((tm, kh), lambda i: (i, a_blk)),
                  pl.BlockSpec((tm, kh), lambda i: (i, b_blk)),
                  pl.BlockSpec((kh, d), lambda i: (0, 0)),
                  pl.BlockSpec((kh, d), lambda i: (1, 0)),
                  pl.BlockSpec((1, d), lambda i: (0, 0)),
                  pl.BlockSpec((d, dff), lambda i: (0, 0)),
                  pl.BlockSpec((d, dff), lambda i: (0, 0)),
                  pl.BlockSpec((CONV_W, dff), lambda i: (0, 0)),
                  pl.BlockSpec((1, dff), lambda i: (0, 0)),
                  pl.BlockSpec((dff, d), lambda i: (0, 0))],
        out_specs=pl.BlockSpec((tm, d), lambda i: (i, 0)),
        out_shape=jax.ShapeDtypeStruct((n, d), F32),
        scratch_shapes=[pltpu.VMEM((tm, d), BF16), pltpu.VMEM((tm, dff), BF16),
                        pltpu.VMEM((2, V7X_SUBLANES + tm, tf), F32),
                        pltpu.VMEM((V7X_SUBLANES, dff), F32)],
        compiler_params=_params(("arbitrary",), vmem),
        name="conv_ffn",
    )(x, a, b, w_out, w_out, g.reshape(1, d), w_gate, w_up, conv_w, conv_b.reshape(1, dff), w_down)


def kernel(x, norm_mix_g, norm_ffn_g, ev_w_in, ev_a_ln_g, ev_a_ln_b, ev_a_ws, ev_a_bs, ev_q_g, ev_k_g,
           ev_w_out, od_w_in, od_w_a2, od_b_a, od_head_g, od_w_out, ffn_w_gate, ffn_w_up, ffn_conv_w,
           ffn_conv_b, ffn_w_down):
    bsz, seq, d = x.shape
    n = bsz * seq
    depth = norm_mix_g.shape[0]
    main = 2 * C_HEADS * (C_DK + C_DV)
    ev_w_in, ev_w_out, od_w_out = ev_w_in.astype(BF16), ev_w_out.astype(BF16), od_w_out.astype(BF16)
    od_w_main = od_w_in[:, :, :main].astype(BF16)
    od_w_side = jnp.pad(od_w_in[:, :, main:], ((0, 0), (0, 0), (0, V7X_LANES - C_GATE_RANK))).astype(BF16)
    ffn_w_gate, ffn_w_up, ffn_w_down = ffn_w_gate.astype(BF16), ffn_w_up.astype(BF16), ffn_w_down.astype(BF16)
    xf = x.reshape(n, d)
    for layer in range(depth):
        if layer % 2 == 0:
            e = layer // 2
            z = _norm_matmul(xf, norm_mix_g[layer], ev_w_in[e], tm=512)
            a_out = _gmlp(z, ev_a_ln_g[e], ev_a_ln_b[e], ev_a_ws[e], ev_a_bs[e], tm=512)
            b_out = _dilated(z.reshape(bsz, seq, -1), ev_q_g[e], ev_k_g[e]).reshape(n, -1)
            mix = (a_out, 0, b_out, 0, ev_w_out[e])
        else:
            o = layer // 2
            z, ga = _norm_matmul(xf, norm_mix_g[layer], od_w_main[o], od_w_side[o], tm=512)
            mixed = _gla(z.reshape(bsz, seq, -1), ga.reshape(bsz, seq, -1), od_w_a2[o], od_b_a[o],
                         od_head_g[o], tc=512).reshape(n, -1)
            mix = (mixed, 0, mixed, 1, od_w_out[o])
        xf = _ffn(xf, *mix, norm_ffn_g[layer], ffn_w_gate[layer], ffn_w_up[layer], ffn_conv_w[layer],
                  ffn_conv_b[layer], ffn_w_down[layer], seq=seq, tm=512, tf=256)
    return xf.reshape(bsz, seq, d)
```

```python
import functools

import jax
import jax.numpy as jnp
from jax import lax
from jax.experimental import pallas as pl
from jax.experimental.pallas import tpu as pltpu

A_GROUPS = 8
A_GROUP_DIM = 64
A_CHUNK = 128
B_HEAD_DIM = 64
B_DILATIONS = (1, 4, 16)
B_BLOCK = 128
C_HEADS = 4
C_DK = 128
C_DV = 256
C_GATE_RANK = 16
C_TAU = 16.0
C_CHUNK = 64
CONV_W = 3
EPS = 1e-6
NEG = -1e30

V7X_LANES = 128
V7X_SUBLANES = 8
V7X_VMEM_BUDGET = 56 * 1024 * 1024

F32 = jnp.float32
BF16 = jnp.bfloat16
NT_DIMS = (((1,), (1,)), ((), ()))
TN_DIMS = (((0,), (0,)), ((), ()))

ROW_SUB = 2 * A_CHUNK
B_PLANES = max(B_DILATIONS)
STAGE_PITCH = 24
B_SCORE_SCALE = B_HEAD_DIM ** -0.5 * 1.4426950408889634


def _params(semantics, vmem_bytes):
    return pltpu.CompilerParams(
        dimension_semantics=semantics,
        vmem_limit_bytes=min(int(vmem_bytes * 1.25) + (4 << 20), V7X_VMEM_BUDGET))


def _rms_rows(x, g):
    return x * lax.rsqrt(jnp.mean(x * x, axis=-1, keepdims=True) + EPS) * g


def _gelu(x):
    return 0.5 * x * (1.0 + lax.erf(x * (0.5 ** 0.5)))


def _silu(x):
    return x * jax.nn.sigmoid(x)


def _head_pair_rms(x, g, head0):
    x2 = x * x
    s0 = jnp.sum(jnp.where(head0, x2, 0.0), axis=-1, keepdims=True)
    s1 = jnp.sum(jnp.where(head0, 0.0, x2), axis=-1, keepdims=True)
    ms = jnp.where(head0, s0, s1) * (1.0 / B_HEAD_DIM)
    return x * lax.rsqrt(ms + EPS) * g


def _norm_matmul_kernel(x_ref, g_ref, w_ref, ws_ref, o_ref, os_ref):
    for r in range(x_ref.shape[0] // ROW_SUB):
        rows = slice(r * ROW_SUB, (r + 1) * ROW_SUB)
        h = _rms_rows(x_ref[rows, :], g_ref[...]).astype(BF16)
        o_ref[rows, :] = jnp.dot(h, w_ref[...], preferred_element_type=F32)
        os_ref[rows, :] = jnp.dot(h, ws_ref[...], preferred_element_type=F32)


def _norm_matmul(x, g, w, w_side, *, tm):
    n, d = x.shape
    f, fs = w.shape[1], w_side.shape[1]
    vmem = 2 * (tm * d * 4 + d * (f + fs) * 2 + tm * (f + fs) * 4) + 2 * ROW_SUB * f * 4
    return pl.pallas_call(
        _norm_matmul_kernel,
        grid=(n // tm,),
        in_specs=[pl.BlockSpec((tm, d), lambda i: (i, 0)),
                  pl.BlockSpec((1, d), lambda i: (0, 0)),
                  pl.BlockSpec((d, f), lambda i: (0, 0)),
                  pl.BlockSpec((d, fs), lambda i: (0, 0))],
        out_specs=(pl.BlockSpec((tm, f), lambda i: (i, 0)), pl.BlockSpec((tm, fs), lambda i: (i, 0))),
        out_shape=(jax.ShapeDtypeStruct((n, f), F32), jax.ShapeDtypeStruct((n, fs), F32)),
        compiler_params=_params(("parallel",), vmem),
        name="norm_matmul",
    )(x, g.reshape(1, d), w, w_side)


def _even_in_kernel(x_ref, g_ref, w_ref, lng_ref, lnb_ref, ws_ref, bs_ref, qg_ref, kg_ref,
                    a_ref, qb_ref, kn_ref, vn_ref, qp_ref, kp_ref, vp_ref, stage_ref):
    tm = x_ref.shape[0]
    aw = a_ref.shape[1]
    bw = kn_ref.shape[1]
    tiles = bw // V7X_LANES
    row = lax.broadcasted_iota(jnp.int32, (A_CHUNK, 2 * A_CHUNK), 0)
    col = lax.broadcasted_iota(jnp.int32, (A_CHUNK, 2 * A_CHUNK), 1)
    causal = (col % A_CHUNK) <= row
    lane2 = lax.broadcasted_iota(jnp.int32, (1, 2 * V7X_LANES), 1)
    first_group = (lane2 % V7X_LANES) < A_GROUP_DIM
    head0 = lax.broadcasted_iota(jnp.int32, (1, V7X_LANES), 1) < B_HEAD_DIM
    w_pairs = [jnp.where(causal, ws_ref[t], 0.0).astype(BF16) for t in range(ws_ref.shape[0])]
    for r in range(tm // ROW_SUB):
        base = r * ROW_SUB
        rows = slice(base, base + ROW_SUB)
        r0 = slice(base, base + A_CHUNK)
        r1 = slice(base + A_CHUNK, base + 2 * A_CHUNK)
        h = _rms_rows(x_ref[rows, :], g_ref[...]).astype(BF16)

        uv = jnp.dot(h, w_ref[:, :2 * aw], preferred_element_type=F32)
        v = _gelu(uv[:, aw:])
        vc = v - jnp.mean(v, axis=-1, keepdims=True)
        vn = vc * lax.rsqrt(jnp.mean(vc * vc, axis=-1, keepdims=True) + EPS) * lng_ref[...] + lnb_ref[...]
        for t in range(aw // V7X_LANES):
            lanes = slice(t * V7X_LANES, (t + 1) * V7X_LANES)
            cc = jnp.concatenate([vn[:A_CHUNK, lanes], vn[A_CHUNK:, lanes]], axis=1)
            rhs = jnp.concatenate([jnp.where(first_group, cc, 0.0),
                                   jnp.where(first_group, 0.0, cc)], axis=0).astype(BF16)
            mixed = jnp.dot(w_pairs[t], rhs, preferred_element_type=F32)
            bias = bs_ref[:, lanes]
            u = _gelu(uv[:, lanes])
            a_ref[r0, lanes] = (u[:A_CHUNK] * (mixed[:, :V7X_LANES] + bias)).astype(BF16)
            a_ref[r1, lanes] = (u[A_CHUNK:] * (mixed[:, V7X_LANES:] + bias)).astype(BF16)

        qkv = jnp.dot(h, w_ref[:, 2 * aw:], preferred_element_type=F32)
        for t in range(tiles):
            lanes = slice(t * V7X_LANES, (t + 1) * V7X_LANES)
            q = _head_pair_rms(qkv[:, lanes], qg_ref[...], head0) * B_SCORE_SCALE
            k = _head_pair_rms(qkv[:, bw + t * V7X_LANES:bw + (t + 1) * V7X_LANES], kg_ref[...], head0)
            vb = qkv[:, 2 * bw + t * V7X_LANES:2 * bw + (t + 1) * V7X_LANES]
            kn_ref[rows, lanes] = k.astype(BF16)
            vn_ref[rows, lanes] = vb.astype(BF16)
            for i, val in enumerate((q, k, vb)):
                for grp in range(ROW_SUB // B_PLANES):
                    dst = (base // B_PLANES + grp) * STAGE_PITCH
                    stage_ref[i * tiles + t, dst:dst + B_PLANES, :] = val[grp * B_PLANES:(grp + 1) * B_PLANES]

    per_plane = tm // B_PLANES
    per_block = B_BLOCK // B_PLANES
    for t in range(tiles):
        lanes = slice(t * V7X_LANES, (t + 1) * V7X_LANES)
        for p in range(B_PLANES):
            sel = pl.ds(p, per_plane, stride=STAGE_PITCH)
            for i, planes in enumerate((qp_ref, kp_ref, vp_ref)):
                planes[p, :, lanes] = stage_ref[i * tiles + t, sel, :].astype(BF16)
        for blk in range(tm // B_BLOCK):
            first = blk * per_block * STAGE_PITCH
            slabs = [stage_ref[t, pl.ds(first + p, per_block, stride=STAGE_PITCH), :] for p in range(B_PLANES)]
            qb_ref[blk * B_BLOCK:(blk + 1) * B_BLOCK, lanes] = jnp.concatenate(slabs, axis=0).astype(BF16)


def _even_in(x, g, w, ln_g, ln_b, w_s, b_s, q_g, k_g, *, bsz, tm):
    n, d = x.shape
    seq = n // bsz
    aw = A_GROUPS * A_GROUP_DIM
    bw = (w.shape[1] - 2 * aw) // 3
    pairs = A_GROUPS // 2
    tiles = seq // tm
    per_plane = tm // B_PLANES
    ws_pairs = w_s.reshape(pairs, 2, A_CHUNK, A_CHUNK).transpose(0, 2, 1, 3).reshape(pairs, A_CHUNK, 2 * A_CHUNK)
    bias = jnp.repeat(b_s.T, A_GROUP_DIM, axis=1)
    qg2 = jnp.tile(q_g, 2).reshape(1, V7X_LANES)
    kg2 = jnp.tile(k_g, 2).reshape(1, V7X_LANES)
    nat_spec = pl.BlockSpec((tm, bw), lambda i: (i, 0))
    plane_spec = pl.BlockSpec((None, B_PLANES, per_plane, bw), lambda i: (i // tiles, 0, i % tiles, 0))
    nat_shape = jax.ShapeDtypeStruct((n, bw), BF16)
    plane_shape = jax.ShapeDtypeStruct((bsz, B_PLANES, seq // B_PLANES, bw), BF16)
    vmem = (2 * (tm * d * 4 + d * w.shape[1] * 2 + tm * aw * 2 + 6 * tm * bw * 2)
            + 3 * per_plane * STAGE_PITCH * bw * 4 + 4 * ROW_SUB * (2 * aw + 3 * bw) * 4)
    return pl.pallas_call(
        _even_in_kernel,
        grid=(n // tm,),
        in_specs=[pl.BlockSpec((tm, d), lambda i: (i, 0)),
                  pl.BlockSpec((1, d), lambda i: (0, 0)),
                  pl.BlockSpec(w.shape, lambda i: (0, 0)),
                  pl.BlockSpec((1, aw), lambda i: (0, 0)),
                  pl.BlockSpec((1, aw), lambda i: (0, 0)),
                  pl.BlockSpec((pairs, A_CHUNK, 2 * A_CHUNK), lambda i: (0, 0, 0)),
                  pl.BlockSpec((A_CHUNK, aw), lambda i: (0, 0)),
                  pl.BlockSpec((1, V7X_LANES), lambda i: (0, 0)),
                  pl.BlockSpec((1, V7X_LANES), lambda i: (0, 0))],
        out_specs=(pl.BlockSpec((tm, aw), lambda i: (i, 0)), nat_spec, nat_spec, nat_spec,
                   plane_spec, plane_spec, plane_spec),
        out_shape=(jax.ShapeDtypeStruct((n, aw), BF16), nat_shape, nat_shape, nat_shape,
                   plane_shape, plane_shape, plane_shape),
        scratch_shapes=[pltpu.VMEM((3 * (bw // V7X_LANES), per_plane * STAGE_PITCH, V7X_LANES), F32)],
        compiler_params=_params(("parallel",), vmem),
        name="even_in",
    )(x, g.reshape(1, d), w, ln_g.reshape(1, aw), ln_b.reshape(1, aw), ws_pairs, bias, qg2, kg2)


def _dilated_kernel(qb_ref, kn_ref, vn_ref, qp_ref, kp_ref, vp_ref, o_ref, ob_ref, mb_ref, db_ref, mask_ref):
    m_rows = qp_ref.shape[1]
    blk2 = 2 * B_BLOCK
    head0 = lax.broadcasted_iota(jnp.int32, (1, V7X_LANES), 1) < B_HEAD_DIM

    rowi = lax.broadcasted_iota(jnp.int32, (blk2, blk2), 0) % B_BLOCK
    coli = lax.broadcasted_iota(jnp.int32, (blk2, blk2), 1)
    is_cur = coli >= B_BLOCK
    colj = coli % B_BLOCK
    for bi, d in enumerate(B_DILATIONS):
        planes = B_PLANES // d
        mb = B_BLOCK // planes
        i_pos = (rowi % mb) * planes + rowi // mb
        j_pos = colj if d == 1 else (colj % mb) * planes + colj // mb
        band = jnp.where(jnp.where(is_cur, i_pos - j_pos, j_pos - i_pos) >= 0, 1.0, 0.0)
        mask_ref[2 * bi] = jnp.where(is_cur, band, 0.0)
        mask_ref[2 * bi + 1] = band

    ones = jnp.ones((blk2, V7X_LANES), BF16)

    def attend(qb, kcat, vcat, valid):
        zero = jnp.zeros_like(qb)
        q2 = jnp.concatenate([jnp.where(head0, qb, zero), jnp.where(head0, zero, qb)], axis=0)
        s = lax.dot_general(q2, kcat, NT_DIMS, preferred_element_type=F32)
        s = jnp.where(valid, s, NEG)
        m = jnp.max(s, axis=-1, keepdims=True)
        p = jnp.exp2(s - m).astype(BF16)
        pv = jnp.dot(p, jnp.concatenate([vcat, ones], axis=1), preferred_element_type=F32)
        top, bot = pv[:B_BLOCK], pv[B_BLOCK:]
        return (jnp.where(head0, top[:, :V7X_LANES], bot[:, :V7X_LANES]),
                jnp.where(head0, m[:B_BLOCK], m[B_BLOCK:]),
                jnp.where(head0, top[:, V7X_LANES:], bot[:, V7X_LANES:]))

    def token_block(n, carry):
        per_block = B_BLOCK // B_PLANES
        off = pl.multiple_of(n * B_BLOCK, B_BLOCK)
        off_prev = pl.multiple_of(jnp.maximum(n - 1, 0) * B_BLOCK, B_BLOCK)
        off_plane = pl.multiple_of(n * per_block, per_block)
        kcat = jnp.concatenate([kn_ref[pl.ds(off_prev, B_BLOCK), :], kn_ref[pl.ds(off, B_BLOCK), :]], axis=0)
        vcat = jnp.concatenate([vn_ref[pl.ds(off_prev, B_BLOCK), :], vn_ref[pl.ds(off, B_BLOCK), :]], axis=0)
        o, m, den = attend(qb_ref[pl.ds(off, B_BLOCK), :], kcat, vcat, mask_ref[jnp.minimum(n, 1)] > 0.0)
        for p in range(B_PLANES):
            for ref, val in ((ob_ref, o), (mb_ref, m), (db_ref, den)):
                ref[0, p, pl.ds(off_plane, per_block), :] = val[p * per_block:(p + 1) * per_block]
        return carry

    lax.fori_loop(0, kn_ref.shape[0] // B_BLOCK, token_block, 0, unroll=8)

    for bi, d in enumerate(B_DILATIONS):
        if d == 1:
            continue
        planes = B_PLANES // d
        mb = B_BLOCK // planes
        nb = m_rows // mb

        def plane_block(blk, carry, bi=bi, d=d, planes=planes, mb=mb, nb=nb):
            r = blk // nb
            n = blk % nb
            off = pl.multiple_of(n * mb, mb)
            off_prev = pl.multiple_of(jnp.maximum(n - 1, 0) * mb, mb)

            def gather(ref, offs):
                return jnp.concatenate([ref[r + d * a, pl.ds(o, mb), :] for o in offs for a in range(planes)],
                                       axis=0)

            o, m, den = attend(gather(qp_ref, (off,)), gather(kp_ref, (off_prev, off)),
                               gather(vp_ref, (off_prev, off)), mask_ref[2 * bi + jnp.minimum(n, 1)] > 0.0)
            for a in range(planes):
                for ref, val in ((ob_ref, o), (mb_ref, m), (db_ref, den)):
                    ref[bi, r + d * a, pl.ds(off, mb), :] = val[a * mb:(a + 1) * mb]
            return carry

        lax.fori_loop(0, d * nb, plane_block, 0, unroll=8)

    branches = range(len(B_DILATIONS))
    for r in range(B_PLANES):
        mx = functools.reduce(jnp.maximum, [mb_ref[bi, r] for bi in branches])
        es = [jnp.exp2(mb_ref[bi, r] - mx) for bi in branches]
        num = functools.reduce(lambda a, b: a + b, [es[bi] * ob_ref[bi, r] for bi in branches])
        den = functools.reduce(lambda a, b: a + b, [es[bi] * db_ref[bi, r] for bi in branches])
        o_ref[pl.ds(r, m_rows, stride=B_PLANES), :] = num / den


def _dilated(qb, kn, vn, qp, kp, vp):
    bsz, s, bw = kn.shape
    pairs = bw // V7X_LANES
    m_rows = s // B_PLANES
    tile_f32 = s * V7X_LANES * 4
    vmem = 2 * (6 * tile_f32 // 2 + tile_f32) + 9 * tile_f32 + 6 * 4 * B_BLOCK * B_BLOCK * 4 + 8 * tile_f32 // 16
    nat_spec = pl.BlockSpec((None, s, V7X_LANES), lambda b, p: (b, 0, p))
    plane_spec = pl.BlockSpec((None, B_PLANES, m_rows, V7X_LANES), lambda b, p: (b, 0, 0, p))
    return pl.pallas_call(
        _dilated_kernel,
        grid=(bsz, pairs),
        in_specs=[nat_spec, nat_spec, nat_spec, plane_spec, plane_spec, plane_spec],
        out_specs=pl.BlockSpec((None, s, V7X_LANES), lambda b, p: (b, 0, p)),
        out_shape=jax.ShapeDtypeStruct((bsz, s, bw), F32),
        scratch_shapes=[pltpu.VMEM((len(B_DILATIONS), B_PLANES, m_rows, V7X_LANES), F32)] * 3
                       + [pltpu.VMEM((2 * len(B_DILATIONS), 2 * B_BLOCK, 2 * B_BLOCK), F32)],
        compiler_params=_params(("parallel", "parallel"), vmem),
        name="dilated_attention",
    )(qb, kn, vn, qp, kp, vp)


def _gla_kernel(q_ref, k_ref, v_ref, r_ref, ga_ref, wa_ref, ba_ref, hg_ref, o_ref, st_ref):
    tc = q_ref.shape[0]
    chunks = tc // C_CHUNK

    @pl.when(pl.program_id(1) == 0)
    def _():
        st_ref[...] = jnp.zeros_like(st_ref)

    gate = jnp.dot(ga_ref[...].astype(BF16), wa_ref[...], preferred_element_type=F32) + ba_ref[...]
    log_a = jax.nn.log_sigmoid(gate) / C_TAU
    ci = lax.broadcasted_iota(jnp.int32, (C_CHUNK, C_CHUNK), 0)
    cj = lax.broadcasted_iota(jnp.int32, (C_CHUNK, C_CHUNK), 1)
    causal = ci >= cj
    tri = jnp.where(causal, 1.0, 0.0).astype(BF16)
    rows = [slice(c * C_CHUNK, (c + 1) * C_CHUNK) for c in range(chunks)]
    for hd in range(C_HEADS):
        kl = slice(hd * C_DK, (hd + 1) * C_DK)
        vl = slice(hd * C_DV, (hd + 1) * C_DV)
        la = log_a[:, kl]
        hi = la.astype(BF16)
        rest = la - hi.astype(F32)
        mid = rest.astype(BF16)
        lo = (rest - mid.astype(F32)).astype(BF16)
        pieces = jnp.concatenate([hi, mid, lo], axis=1)
        sums = [jnp.dot(tri, pieces[sl], preferred_element_type=F32) for sl in rows]
        b = jnp.concatenate([s3[:, :C_DK] + s3[:, C_DK:2 * C_DK] + s3[:, 2 * C_DK:] for s3 in sums], axis=0)
        b3 = b.reshape(chunks, C_CHUNK, C_DK)
        b_last = b3[:, C_CHUNK - 1:C_CHUNK, :]
        k = k_ref[:, kl]
        q_t = ((q_ref[:, kl] * (C_DK ** -0.5)) * jnp.exp(b)).astype(BF16)
        k_t = (k * jnp.exp(-b)).astype(BF16)
        k_s = (k.reshape(chunks, C_CHUNK, C_DK) * jnp.exp(b_last - b3)).reshape(tc, C_DK).astype(BF16)
        decay = jnp.exp(b_last)
        o_intra, kv_t = [], []
        for sl in rows:
            v_c = v_ref[sl, vl].astype(BF16)
            attn = lax.dot_general(q_t[sl], k_t[sl], NT_DIMS, preferred_element_type=F32)
            attn = jnp.where(causal, attn, 0.0).astype(BF16)
            o_intra.append(jnp.dot(attn, v_c, preferred_element_type=F32))
            kv_t.append(lax.dot_general(v_c, k_s[sl], TN_DIMS, preferred_element_type=F32))
        st = st_ref[hd]
        entering = []
        for c in range(chunks):
            entering.append(st.astype(BF16))
            st = st * decay[c] + kv_t[c]
        st_ref[hd] = st
        for c, sl in enumerate(rows):
            o = o_intra[c] + lax.dot_general(q_t[sl], entering[c], NT_DIMS, preferred_element_type=F32)
            o_ref[sl, vl] = (_rms_rows(o, hg_ref[...]) * _silu(r_ref[sl, vl])).astype(BF16)


def _gla(z3, ga3, w_a2, b_a, head_g, *, tc):
    bsz, s, _ = z3.shape
    hk, hv = C_HEADS * C_DK, C_HEADS * C_DV
    wa = jnp.pad(w_a2, ((0, V7X_LANES - C_GATE_RANK), (0, 0))).astype(BF16)
    vmem = (2 * (2 * tc * hk * 4 + 3 * tc * hv * 4 + tc * V7X_LANES * 4 + V7X_LANES * hk * 2)
            + hv * C_DK * 4 + 12 * tc * C_DK * 4 + 2 * tc * hk * 4)
    return pl.pallas_call(
        _gla_kernel,
        grid=(bsz, s // tc),
        in_specs=[pl.BlockSpec((None, tc, hk), lambda b, t: (b, t, 0)),
                  pl.BlockSpec((None, tc, hk), lambda b, t: (b, t, 1)),
                  pl.BlockSpec((None, tc, hv), lambda b, t: (b, t, 1)),
                  pl.BlockSpec((None, tc, hv), lambda b, t: (b, t, 2)),
                  pl.BlockSpec((None, tc, V7X_LANES), lambda b, t: (b, t, 0)),
                  pl.BlockSpec((V7X_LANES, hk), lambda b, t: (0, 0)),
                  pl.BlockSpec((1, hk), lambda b, t: (0, 0)),
                  pl.BlockSpec((1, C_DV), lambda b, t: (0, 0))],
        out_specs=pl.BlockSpec((None, tc, hv), lambda b, t: (b, t, 0)),
        out_shape=jax.ShapeDtypeStruct((bsz, s, hv), BF16),
        scratch_shapes=[pltpu.VMEM((C_HEADS, C_DV, C_DK), F32)],
        compiler_params=_params(("parallel", "arbitrary"), vmem),
        name="gla",
    )(z3, z3, z3, z3, ga3, wa, b_a.reshape(1, -1), head_g.reshape(1, C_DV))


def _ffn_kernel(x_ref, a_ref, b_ref, wa_ref, wb_ref, g_ref, wg_ref, wu_ref, cw_ref, cb_ref, wd_ref, o_ref,
                h_ref, act_ref, gbuf_ref, halo_ref, *, tiles_per_seq, tf):
    tm = x_ref.shape[0]
    halo_rows = V7X_SUBLANES
    for r in range(tm // ROW_SUB):
        rows = slice(r * ROW_SUB, (r + 1) * ROW_SUB)
        x1 = x_ref[rows, :] + jnp.dot(a_ref[rows, :].astype(BF16), wa_ref[...], preferred_element_type=F32)
        x1 = x1 + jnp.dot(b_ref[rows, :].astype(BF16), wb_ref[...], preferred_element_type=F32)
        o_ref[rows, :] = x1
        h_ref[rows, :] = _rms_rows(x1, g_ref[...]).astype(BF16)

    @pl.when(pl.program_id(0) % tiles_per_seq == 0)
    def _():
        halo_ref[...] = jnp.zeros_like(halo_ref)

    for j in range(wg_ref.shape[1] // tf):
        cols = slice(j * tf, (j + 1) * tf)
        h = h_ref[...]
        gate = jnp.dot(h, wg_ref[:, cols], preferred_element_type=F32)
        up = jnp.dot(h, wu_ref[:, cols], preferred_element_type=F32)
        gbuf = gbuf_ref.at[j % 2]
        gbuf[0:halo_rows, :] = halo_ref[:, cols]
        gbuf[halo_rows:, :] = gate
        halo_ref[:, cols] = gate[tm - halo_rows:, :]
        conv = cb_ref[:, cols]
        for tap in range(CONV_W - 1):
            conv = conv + gbuf[pl.ds(halo_rows - (CONV_W - 1) + tap, tm), :] * cw_ref[tap:tap + 1, cols]
        conv = conv + gate * cw_ref[CONV_W - 1:CONV_W, cols]
        act_ref[:, cols] = (_silu(conv) * up).astype(BF16)
    o_ref[...] += jnp.dot(act_ref[...], wd_ref[...], preferred_element_type=F32)


def _ffn(x, a, a_blk, b, b_blk, w_out, g, w_gate, w_up, conv_w, conv_b, w_down, *, seq, tm, tf):
    n, d = x.shape
    dff = w_gate.shape[1]
    kh = w_out.shape[0] // 2
    vmem = (2 * (2 * tm * d * 4 + 2 * tm * kh * 4 + 2 * kh * d * 2 + 3 * d * dff * 2 + 4 * dff * 4)
            + tm * d * 2 + tm * dff * 2 + 2 * (tm + 8) * tf * 4 + 8 * dff * 4 + 8 * tm * tf * 4)
    return pl.pallas_call(
        functools.partial(_ffn_kernel, tiles_per_seq=seq // tm, tf=tf),
        grid=(n // tm,),
        in_specs=[pl.BlockSpec((tm, d), lambda i: (i, 0)),
                  pl.BlockSpec((tm, kh), lambda i: (i, a_blk)),
                  pl.BlockSpec((tm, kh), lambda i: (i, b_blk)),
                  pl.BlockSpec((kh, d), lambda i: (0, 0)),
                  pl.BlockSpec((kh, d), lambda i: (1, 0)),
                  pl.BlockSpec((1, d), lambda i: (0, 0)),
                  pl.BlockSpec((d, dff), lambda i: (0, 0)),
                  pl.BlockSpec((d, dff), lambda i: (0, 0)),
                  pl.BlockSpec((CONV_W, dff), lambda i: (0, 0)),
                  pl.BlockSpec((1, dff), lambda i: (0, 0)),
                  pl.BlockSpec((dff, d), lambda i: (0, 0))],
        out_specs=pl.BlockSpec((tm, d), lambda i: (i, 0)),
        out_shape=jax.ShapeDtypeStruct((n, d), F32),
        scratch_shapes=[pltpu.VMEM((tm, d), BF16), pltpu.VMEM((tm, dff), BF16),
                        pltpu.VMEM((2, V7X_SUBLANES + tm, tf), F32),
                        pltpu.VMEM((V7X_SUBLANES, dff), F32)],
        compiler_params=_params(("arbitrary",), vmem),
        name="conv_ffn",
    )(x, a, b, w_out, w_out, g.reshape(1, d), w_gate, w_up, conv_w, conv_b.reshape(1, dff), w_down)


def kernel(x, norm_mix_g, norm_ffn_g, ev_w_in, ev_a_ln_g, ev_a_ln_b, ev_a_ws, ev_a_bs, ev_q_g, ev_k_g,
           ev_w_out, od_w_in, od_w_a2, od_b_a, od_head_g, od_w_out, ffn_w_gate, ffn_w_up, ffn_conv_w,
           ffn_conv_b, ffn_w_down):
    bsz, seq, d = x.shape
    n = bsz * seq
    depth = norm_mix_g.shape[0]
    main = 2 * C_HEADS * (C_DK + C_DV)
    ev_w_in, ev_w_out, od_w_out = ev_w_in.astype(BF16), ev_w_out.astype(BF16), od_w_out.astype(BF16)
    od_w_main = od_w_in[:, :, :main].astype(BF16)
    od_w_side = jnp.pad(od_w_in[:, :, main:], ((0, 0), (0, 0), (0, V7X_LANES - C_GATE_RANK))).astype(BF16)
    ffn_w_gate, ffn_w_up, ffn_w_down = ffn_w_gate.astype(BF16), ffn_w_up.astype(BF16), ffn_w_down.astype(BF16)
    xf = x.reshape(n, d)
    for layer in range(depth):
        if layer % 2 == 0:
            e = layer // 2
            a_out, qb, kn, vn, qp, kp, vp = _even_in(
                xf, norm_mix_g[layer], ev_w_in[e], ev_a_ln_g[e], ev_a_ln_b[e], ev_a_ws[e], ev_a_bs[e],
                ev_q_g[e], ev_k_g[e], bsz=bsz, tm=512)
            in_seq = lambda t: t.reshape(bsz, seq, -1)
            b_out = _dilated(in_seq(qb), in_seq(kn), in_seq(vn), qp, kp, vp).reshape(n, -1)
            mix = (a_out, 0, b_out, 0, ev_w_out[e])
        else:
            o = layer // 2
            z, ga = _norm_matmul(xf, norm_mix_g[layer], od_w_main[o], od_w_side[o], tm=512)
            mixed = _gla(z.reshape(bsz, seq, -1), ga.reshape(bsz, seq, -1), od_w_a2[o], od_b_a[o],
                         od_head_g[o], tc=512).reshape(n, -1)
            mix = (mixed, 0, mixed, 1, od_w_out[o])
        xf = _ffn(xf, *mix, norm_ffn_g[layer], ffn_w_gate[layer], ffn_w_up[layer], ffn_conv_w[layer],
                  ffn_conv_b[layer], ffn_w_down[layer], seq=seq, tm=512, tf=256)
    return xf.reshape(bsz, seq, d)
```

```python
import functools

import jax
import jax.numpy as jnp
from jax import lax
from jax.experimental import pallas as pl
from jax.experimental.pallas import tpu as pltpu

A_GROUPS = 8
A_GROUP_DIM = 64
A_CHUNK = 128
B_HEAD_DIM = 64
B_DILATIONS = (1, 4, 16)
B_BLOCK = 128
C_HEADS = 4
C_DK = 128
C_DV = 256
C_GATE_RANK = 16
C_TAU = 16.0
C_CHUNK = 64
CONV_W = 3
EPS = 1e-6
NEG = -1e30

V7X_LANES = 128
V7X_SUBLANES = 8
V7X_VMEM_BUDGET = 56 * 1024 * 1024

F32 = jnp.float32
BF16 = jnp.bfloat16
NT_DIMS = (((1,), (1,)), ((), ()))
TN_DIMS = (((0,), (0,)), ((), ()))

ROW_SUB = 2 * A_CHUNK
B_PLANES = max(B_DILATIONS)
STAGE_PITCH = 24
B_SCORE_SCALE = B_HEAD_DIM ** -0.5 * 1.4426950408889634


def _params(semantics, vmem_bytes):
    return pltpu.CompilerParams(
        dimension_semantics=semantics,
        vmem_limit_bytes=min(int(vmem_bytes * 1.25) + (4 << 20), V7X_VMEM_BUDGET))


def _rms_rows(x, g):
    return x * lax.rsqrt(jnp.mean(x * x, axis=-1, keepdims=True) + EPS) * g


def _gelu(x):
    return 0.5 * x * (1.0 + lax.erf(x * (0.5 ** 0.5)))


def _silu(x):
    return x * jax.nn.sigmoid(x)


def _head_pair_rms(x, g, head0):
    x2 = x * x
    s0 = jnp.sum(jnp.where(head0, x2, 0.0), axis=-1, keepdims=True)
    s1 = jnp.sum(jnp.where(head0, 0.0, x2), axis=-1, keepdims=True)
    ms = jnp.where(head0, s0, s1) * (1.0 / B_HEAD_DIM)
    return x * lax.rsqrt(ms + EPS) * g


def _norm_matmul_kernel(x_ref, g_ref, w_ref, ws_ref, o_ref, os_ref):
    for r in range(x_ref.shape[0] // ROW_SUB):
        rows = slice(r * ROW_SUB, (r + 1) * ROW_SUB)
        h = _rms_rows(x_ref[rows, :], g_ref[...]).astype(BF16)
        o_ref[rows, :] = jnp.dot(h, w_ref[:, :o_ref.shape[1]], preferred_element_type=F32)
        os_ref[rows, :] = jnp.dot(h, ws_ref[...], preferred_element_type=F32)


def _norm_matmul(x, g, w, f, w_side, *, tm):
    n, d = x.shape
    fs = w_side.shape[1]
    vmem = 2 * (tm * d * 4 + d * (w.shape[1] + fs) * 2 + tm * (f + fs) * 4) + 2 * ROW_SUB * f * 4
    return pl.pallas_call(
        _norm_matmul_kernel,
        grid=(n // tm,),
        in_specs=[pl.BlockSpec((tm, d), lambda i: (i, 0)),
                  pl.BlockSpec((1, d), lambda i: (0, 0)),
                  pl.BlockSpec(w.shape, lambda i: (0, 0)),
                  pl.BlockSpec((d, fs), lambda i: (0, 0))],
        out_specs=(pl.BlockSpec((tm, f), lambda i: (i, 0)), pl.BlockSpec((tm, fs), lambda i: (i, 0))),
        out_shape=(jax.ShapeDtypeStruct((n, f), F32), jax.ShapeDtypeStruct((n, fs), F32)),
        compiler_params=_params(("parallel",), vmem),
        name="norm_matmul",
    )(x, g.reshape(1, d), w, w_side)


def _even_in_kernel(x_ref, g_ref, w_ref, lng_ref, lnb_ref, ws_ref, bs_ref, qg_ref, kg_ref,
                    a_ref, qb_ref, kn_ref, vn_ref, qp_ref, kp_ref, vp_ref, h_ref, z_ref, stage_ref):
    tm = x_ref.shape[0]
    aw = a_ref.shape[1]
    bw = kn_ref.shape[1]
    tiles = bw // V7X_LANES
    groups = ROW_SUB // B_PLANES
    per_block = B_BLOCK // B_PLANES
    row = lax.broadcasted_iota(jnp.int32, (A_CHUNK, 2 * A_CHUNK), 0)
    col = lax.broadcasted_iota(jnp.int32, (A_CHUNK, 2 * A_CHUNK), 1)
    causal = (col % A_CHUNK) <= row
    lane2 = lax.broadcasted_iota(jnp.int32, (1, 2 * V7X_LANES), 1)
    first_group = (lane2 % V7X_LANES) < A_GROUP_DIM
    head0 = lax.broadcasted_iota(jnp.int32, (1, V7X_LANES), 1) < B_HEAD_DIM
    w_pairs = [jnp.where(causal, ws_ref[t], 0.0).astype(BF16) for t in range(ws_ref.shape[0])]
    for r in range(tm // ROW_SUB):
        base = r * ROW_SUB
        rows = slice(base, base + ROW_SUB)
        h, z, stage = h_ref.at[r % 2], z_ref.at[r % 2], stage_ref.at[r % 2]
        h[...] = _rms_rows(x_ref[rows, :], g_ref[...]).astype(BF16)
        z[...] = jnp.dot(h[...], w_ref[...], preferred_element_type=F32)

        v = _gelu(z[:, aw:2 * aw])
        vc = v - jnp.mean(v, axis=-1, keepdims=True)
        vn = vc * lax.rsqrt(jnp.mean(vc * vc, axis=-1, keepdims=True) + EPS) * lng_ref[...] + lnb_ref[...]
        for t in range(aw // V7X_LANES):
            lanes = slice(t * V7X_LANES, (t + 1) * V7X_LANES)
            cc = jnp.concatenate([vn[:A_CHUNK, lanes], vn[A_CHUNK:, lanes]], axis=1)
            rhs = jnp.concatenate([jnp.where(first_group, cc, 0.0),
                                   jnp.where(first_group, 0.0, cc)], axis=0).astype(BF16)
            mixed = jnp.dot(w_pairs[t], rhs, preferred_element_type=F32)
            bias = bs_ref[:, lanes]
            for c in range(2):
                chunk = slice(c * A_CHUNK, (c + 1) * A_CHUNK)
                u = _gelu(z[chunk, lanes])
                a_ref[base + c * A_CHUNK:base + (c + 1) * A_CHUNK, lanes] = (
                    u * (mixed[:, c * V7X_LANES:(c + 1) * V7X_LANES] + bias)).astype(BF16)

        for t in range(tiles):
            lanes = slice(t * V7X_LANES, (t + 1) * V7X_LANES)
            q = _head_pair_rms(z[:, 2 * aw + t * V7X_LANES:2 * aw + (t + 1) * V7X_LANES], qg_ref[...], head0)
            q = q * B_SCORE_SCALE
            k = _head_pair_rms(z[:, 2 * aw + bw + t * V7X_LANES:2 * aw + bw + (t + 1) * V7X_LANES],
                               kg_ref[...], head0)
            vb = z[:, 2 * aw + 2 * bw + t * V7X_LANES:2 * aw + 2 * bw + (t + 1) * V7X_LANES]
            kn_ref[rows, lanes] = k.astype(BF16)
            vn_ref[rows, lanes] = vb.astype(BF16)
            for i, val in enumerate((q, k, vb)):
                for grp in range(groups):
                    stage[i * tiles + t, grp * STAGE_PITCH:grp * STAGE_PITCH + B_PLANES, :] = (
                        val[grp * B_PLANES:(grp + 1) * B_PLANES])
            for p in range(B_PLANES):
                sel = pl.ds(p, groups, stride=STAGE_PITCH)
                for i, planes in enumerate((qp_ref, kp_ref, vp_ref)):
                    planes[p, r * groups:(r + 1) * groups, lanes] = stage[i * tiles + t, sel, :].astype(BF16)
            for blk in range(ROW_SUB // B_BLOCK):
                first = blk * per_block * STAGE_PITCH
                slabs = [stage[t, pl.ds(first + p, per_block, stride=STAGE_PITCH), :] for p in range(B_PLANES)]
                qb_ref[base + blk * B_BLOCK:base + (blk + 1) * B_BLOCK, lanes] = (
                    jnp.concatenate(slabs, axis=0).astype(BF16))


def _even_in(x, g, w, ln_g, ln_b, w_s, b_s, q_g, k_g, *, bsz, tm):
    n, d = x.shape
    seq = n // bsz
    aw = A_GROUPS * A_GROUP_DIM
    bw = (w.shape[1] - 2 * aw) // 3
    pairs = A_GROUPS // 2
    tiles = seq // tm
    per_plane = tm // B_PLANES
    ws_pairs = w_s.reshape(pairs, 2, A_CHUNK, A_CHUNK).transpose(0, 2, 1, 3).reshape(pairs, A_CHUNK, 2 * A_CHUNK)
    bias = jnp.repeat(b_s.T, A_GROUP_DIM, axis=1)
    qg2 = jnp.tile(q_g, 2).reshape(1, V7X_LANES)
    kg2 = jnp.tile(k_g, 2).reshape(1, V7X_LANES)
    nat_spec = pl.BlockSpec((tm, bw), lambda i: (i, 0))
    plane_spec = pl.BlockSpec((None, B_PLANES, per_plane, bw), lambda i: (i // tiles, 0, i % tiles, 0))
    nat_shape = jax.ShapeDtypeStruct((n, bw), BF16)
    plane_shape = jax.ShapeDtypeStruct((bsz, B_PLANES, seq // B_PLANES, bw), BF16)
    stage_rows = ROW_SUB // B_PLANES * STAGE_PITCH
    vmem = (2 * (tm * d * 4 + d * w.shape[1] * 2 + tm * aw * 2 + 6 * tm * bw * 2)
            + 2 * (ROW_SUB * d * 2 + ROW_SUB * w.shape[1] * 4 + 3 * stage_rows * bw * 4)
            + 2 * ROW_SUB * (2 * aw + 3 * bw) * 4)
    return pl.pallas_call(
        _even_in_kernel,
        grid=(n // tm,),
        in_specs=[pl.BlockSpec((tm, d), lambda i: (i, 0)),
                  pl.BlockSpec((1, d), lambda i: (0, 0)),
                  pl.BlockSpec(w.shape, lambda i: (0, 0)),
                  pl.BlockSpec((1, aw), lambda i: (0, 0)),
                  pl.BlockSpec((1, aw), lambda i: (0, 0)),
                  pl.BlockSpec((pairs, A_CHUNK, 2 * A_CHUNK), lambda i: (0, 0, 0)),
                  pl.BlockSpec((A_CHUNK, aw), lambda i: (0, 0)),
                  pl.BlockSpec((1, V7X_LANES), lambda i: (0, 0)),
                  pl.BlockSpec((1, V7X_LANES), lambda i: (0, 0))],
        out_specs=(pl.BlockSpec((tm, aw), lambda i: (i, 0)), nat_spec, nat_spec, nat_spec,
                   plane_spec, plane_spec, plane_spec),
        out_shape=(jax.ShapeDtypeStruct((n, aw), BF16), nat_shape, nat_shape, nat_shape,
                   plane_shape, plane_shape, plane_shape),
        scratch_shapes=[pltpu.VMEM((2, ROW_SUB, d), BF16), pltpu.VMEM((2, ROW_SUB, w.shape[1]), F32),
                        pltpu.VMEM((2, 3 * (bw // V7X_LANES), stage_rows, V7X_LANES), F32)],
        compiler_params=_params(("parallel",), vmem),
        name="even_in",
    )(x, g.reshape(1, d), w, ln_g.reshape(1, aw), ln_b.reshape(1, aw), ws_pairs, bias, qg2, kg2)


def _dilated_kernel(qb_ref, kn_ref, vn_ref, qp_ref, kp_ref, vp_ref, o_ref, ob_ref, mb_ref, db_ref, mask_ref):
    m_rows = qp_ref.shape[1]
    blk2 = 2 * B_BLOCK
    head0 = lax.broadcasted_iota(jnp.int32, (1, V7X_LANES), 1) < B_HEAD_DIM

    rowi = lax.broadcasted_iota(jnp.int32, (blk2, blk2), 0) % B_BLOCK
    coli = lax.broadcasted_iota(jnp.int32, (blk2, blk2), 1)
    is_cur = coli >= B_BLOCK
    colj = coli % B_BLOCK
    for bi, d in enumerate(B_DILATIONS):
        planes = B_PLANES // d
        mb = B_BLOCK // planes
        i_pos = (rowi % mb) * planes + rowi // mb
        j_pos = colj if d == 1 else (colj % mb) * planes + colj // mb
        band = jnp.where(jnp.where(is_cur, i_pos - j_pos, j_pos - i_pos) >= 0, 1.0, 0.0)
        mask_ref[2 * bi] = jnp.where(is_cur, band, 0.0)
        mask_ref[2 * bi + 1] = band

    ones = jnp.ones((blk2, V7X_LANES), BF16)

    def attend(qb, kcat, vcat, valid):
        zero = jnp.zeros_like(qb)
        q2 = jnp.concatenate([jnp.where(head0, qb, zero), jnp.where(head0, zero, qb)], axis=0)
        s = lax.dot_general(q2, kcat, NT_DIMS, preferred_element_type=F32)
        s = jnp.where(valid, s, NEG)
        m = jnp.max(s, axis=-1, keepdims=True)
        p = jnp.exp2(s - m).astype(BF16)
        pv = jnp.dot(p, jnp.concatenate([vcat, ones], axis=1), preferred_element_type=F32)
        top, bot = pv[:B_BLOCK], pv[B_BLOCK:]
        return (jnp.where(head0, top[:, :V7X_LANES], bot[:, :V7X_LANES]),
                jnp.where(head0, m[:B_BLOCK], m[B_BLOCK:]),
                jnp.where(head0, top[:, V7X_LANES:], bot[:, V7X_LANES:]))

    def token_block(n, carry):
        per_block = B_BLOCK // B_PLANES
        off = pl.multiple_of(n * B_BLOCK, B_BLOCK)
        off_prev = pl.multiple_of(jnp.maximum(n - 1, 0) * B_BLOCK, B_BLOCK)
        off_plane = pl.multiple_of(n * per_block, per_block)
        kcat = jnp.concatenate([kn_ref[pl.ds(off_prev, B_BLOCK), :], kn_ref[pl.ds(off, B_BLOCK), :]], axis=0)
        vcat = jnp.concatenate([vn_ref[pl.ds(off_prev, B_BLOCK), :], vn_ref[pl.ds(off, B_BLOCK), :]], axis=0)
        o, m, den = attend(qb_ref[pl.ds(off, B_BLOCK), :], kcat, vcat, mask_ref[jnp.minimum(n, 1)] > 0.0)
        for p in range(B_PLANES):
            for ref, val in ((ob_ref, o), (mb_ref, m), (db_ref, den)):
                ref[0, p, pl.ds(off_plane, per_block), :] = val[p * per_block:(p + 1) * per_block]
        return carry

    lax.fori_loop(0, kn_ref.shape[0] // B_BLOCK, token_block, 0, unroll=32)

    for bi, d in enumerate(B_DILATIONS):
        if d == 1:
            continue
        planes = B_PLANES // d
        mb = B_BLOCK // planes
        nb = m_rows // mb

        def plane_block(blk, carry, bi=bi, d=d, planes=planes, mb=mb, nb=nb):
            r = blk // nb
            n = blk % nb
            off = pl.multiple_of(n * mb, mb)
            off_prev = pl.multiple_of(jnp.maximum(n - 1, 0) * mb, mb)

            def gather(ref, offs):
                return jnp.concatenate([ref[r + d * a, pl.ds(o, mb), :] for o in offs for a in range(planes)],
                                       axis=0)

            o, m, den = attend(gather(qp_ref, (off,)), gather(kp_ref, (off_prev, off)),
                               gather(vp_ref, (off_prev, off)), mask_ref[2 * bi + jnp.minimum(n, 1)] > 0.0)
            for a in range(planes):
                for ref, val in ((ob_ref, o), (mb_ref, m), (db_ref, den)):
                    ref[bi, r + d * a, pl.ds(off, mb), :] = val[a * mb:(a + 1) * mb]
            return carry

        lax.fori_loop(0, d * nb, plane_block, 0, unroll=32)

    branches = range(len(B_DILATIONS))
    for r in range(B_PLANES):
        mx = functools.reduce(jnp.maximum, [mb_ref[bi, r] for bi in branches])
        es = [jnp.exp2(mb_ref[bi, r] - mx) for bi in branches]
        num = functools.reduce(lambda a, b: a + b, [es[bi] * ob_ref[bi, r] for bi in branches])
        den = functools.reduce(lambda a, b: a + b, [es[bi] * db_ref[bi, r] for bi in branches])
        o_ref[pl.ds(r, m_rows, stride=B_PLANES), :] = num / den


def _dilated(qb, kn, vn, qp, kp, vp):
    bsz, s, bw = kn.shape
    pairs = bw // V7X_LANES
    m_rows = s // B_PLANES
    tile_f32 = s * V7X_LANES * 4
    vmem = 2 * (6 * tile_f32 // 2 + tile_f32) + 9 * tile_f32 + 6 * 4 * B_BLOCK * B_BLOCK * 4 + 8 * tile_f32 // 16
    nat_spec = pl.BlockSpec((None, s, V7X_LANES), lambda b, p: (b, 0, p))
    plane_spec = pl.BlockSpec((None, B_PLANES, m_rows, V7X_LANES), lambda b, p: (b, 0, 0, p))
    return pl.pallas_call(
        _dilated_kernel,
        grid=(bsz, pairs),
        in_specs=[nat_spec, nat_spec, nat_spec, plane_spec, plane_spec, plane_spec],
        out_specs=pl.BlockSpec((None, s, V7X_LANES), lambda b, p: (b, 0, p)),
        out_shape=jax.ShapeDtypeStruct((bsz, s, bw), F32),
        scratch_shapes=[pltpu.VMEM((len(B_DILATIONS), B_PLANES, m_rows, V7X_LANES), F32)] * 3
                       + [pltpu.VMEM((2 * len(B_DILATIONS), 2 * B_BLOCK, 2 * B_BLOCK), F32)],
        compiler_params=_params(("parallel", "parallel"), vmem),
        name="dilated_attention",
    )(qb, kn, vn, qp, kp, vp)


def _gla_kernel(q_ref, k_ref, v_ref, r_ref, ga_ref, wa_ref, ba_ref, hg_ref, o_ref, st_ref):
    tc = q_ref.shape[0]
    chunks = tc // C_CHUNK

    @pl.when(pl.program_id(1) == 0)
    def _():
        st_ref[...] = jnp.zeros_like(st_ref)

    gate = jnp.dot(ga_ref[...].astype(BF16), wa_ref[...], preferred_element_type=F32) + ba_ref[...]
    log_a = jax.nn.log_sigmoid(gate) / C_TAU
    ci = lax.broadcasted_iota(jnp.int32, (C_CHUNK, C_CHUNK), 0)
    cj = lax.broadcasted_iota(jnp.int32, (C_CHUNK, C_CHUNK), 1)
    causal = ci >= cj
    tri = jnp.where(causal, 1.0, 0.0).astype(BF16)
    rows = [slice(c * C_CHUNK, (c + 1) * C_CHUNK) for c in range(chunks)]
    for hd in range(C_HEADS):
        kl = slice(hd * C_DK, (hd + 1) * C_DK)
        vl = slice(hd * C_DV, (hd + 1) * C_DV)
        la = log_a[:, kl]
        hi = la.astype(BF16)
        rest = la - hi.astype(F32)
        mid = rest.astype(BF16)
        lo = (rest - mid.astype(F32)).astype(BF16)
        pieces = jnp.concatenate([hi, mid, lo], axis=1)
        sums = [jnp.dot(tri, pieces[sl], preferred_element_type=F32) for sl in rows]
        b = jnp.concatenate([s3[:, :C_DK] + s3[:, C_DK:2 * C_DK] + s3[:, 2 * C_DK:] for s3 in sums], axis=0)
        b3 = b.reshape(chunks, C_CHUNK, C_DK)
        b_last = b3[:, C_CHUNK - 1:C_CHUNK, :]
        k = k_ref[:, kl]
        q_t = ((q_ref[:, kl] * (C_DK ** -0.5)) * jnp.exp(b)).astype(BF16)
        k_t = (k * jnp.exp(-b)).astype(BF16)
        k_s = (k.reshape(chunks, C_CHUNK, C_DK) * jnp.exp(b_last - b3)).reshape(tc, C_DK).astype(BF16)
        decay = jnp.exp(b_last)
        o_intra, kv_t = [], []
        for sl in rows:
            v_c = v_ref[sl, vl].astype(BF16)
            attn = lax.dot_general(q_t[sl], k_t[sl], NT_DIMS, preferred_element_type=F32)
            attn = jnp.where(causal, attn, 0.0).astype(BF16)
            o_intra.append(jnp.dot(attn, v_c, preferred_element_type=F32))
            kv_t.append(lax.dot_general(v_c, k_s[sl], TN_DIMS, preferred_element_type=F32))
        st = st_ref[hd]
        entering = []
        for c in range(chunks):
            entering.append(st.astype(BF16))
            st = st * decay[c] + kv_t[c]
        st_ref[hd] = st
        for c, sl in enumerate(rows):
            o = o_intra[c] + lax.dot_general(q_t[sl], entering[c], NT_DIMS, preferred_element_type=F32)
            o_ref[sl, vl] = (_rms_rows(o, hg_ref[...]) * _silu(r_ref[sl, vl])).astype(BF16)


def _gla(z3, ga3, w_a2, b_a, head_g, *, tc):
    bsz, s, _ = z3.shape
    hk, hv = C_HEADS * C_DK, C_HEADS * C_DV
    wa = jnp.pad(w_a2, ((0, V7X_LANES - C_GATE_RANK), (0, 0))).astype(BF16)
    vmem = (2 * (2 * tc * hk * 4 + 3 * tc * hv * 4 + tc * V7X_LANES * 4 + V7X_LANES * hk * 2)
            + hv * C_DK * 4 + 12 * tc * C_DK * 4 + 2 * tc * hk * 4)
    return pl.pallas_call(
        _gla_kernel,
        grid=(bsz, s // tc),
        in_specs=[pl.BlockSpec((None, tc, hk), lambda b, t: (b, t, 0)),
                  pl.BlockSpec((None, tc, hk), lambda b, t: (b, t, 1)),
                  pl.BlockSpec((None, tc, hv), lambda b, t: (b, t, 1)),
                  pl.BlockSpec((None, tc, hv), lambda b, t: (b, t, 2)),
                  pl.BlockSpec((None, tc, V7X_LANES), lambda b, t: (b, t, 0)),
                  pl.BlockSpec((V7X_LANES, hk), lambda b, t: (0, 0)),
                  pl.BlockSpec((1, hk), lambda b, t: (0, 0)),
                  pl.BlockSpec((1, C_DV), lambda b, t: (0, 0))],
        out_specs=pl.BlockSpec((None, tc, hv), lambda b, t: (b, t, 0)),
        out_shape=jax.ShapeDtypeStruct((bsz, s, hv), BF16),
        scratch_shapes=[pltpu.VMEM((C_HEADS, C_DV, C_DK), F32)],
        compiler_params=_params(("parallel", "arbitrary"), vmem),
        name="gla",
    )(z3, z3, z3, z3, ga3, wa, b_a.reshape(1, -1), head_g.reshape(1, C_DV))


def _ffn_kernel(x_ref, a_ref, b_ref, wa_ref, wb_ref, g_ref, wg_ref, wu_ref, cw_ref, cb_ref, wd_ref, o_ref,
                h_ref, act_ref, gbuf_ref, halo_ref, *, tiles_per_seq, tf):
    tm = x_ref.shape[0]
    halo_rows = V7X_SUBLANES
    for r in range(tm // ROW_SUB):
        rows = slice(r * ROW_SUB, (r + 1) * ROW_SUB)
        x1 = x_ref[rows, :] + jnp.dot(a_ref[rows, :].astype(BF16), wa_ref[...], preferred_element_type=F32)
        x1 = x1 + jnp.dot(b_ref[rows, :].astype(BF16), wb_ref[...], preferred_element_type=F32)
        o_ref[rows, :] = x1
        h_ref[rows, :] = _rms_rows(x1, g_ref[...]).astype(BF16)

    @pl.when(pl.program_id(0) % tiles_per_seq == 0)
    def _():
        halo_ref[...] = jnp.zeros_like(halo_ref)

    for j in range(wg_ref.shape[1] // tf):
        cols = slice(j * tf, (j + 1) * tf)
        h = h_ref[...]
        gate = jnp.dot(h, wg_ref[:, cols], preferred_element_type=F32)
        up = jnp.dot(h, wu_ref[:, cols], preferred_element_type=F32)
        gbuf = gbuf_ref.at[j % 2]
        gbuf[0:halo_rows, :] = halo_ref[:, cols]
        gbuf[halo_rows:, :] = gate
        halo_ref[:, cols] = gate[tm - halo_rows:, :]
        conv = cb_ref[:, cols]
        for tap in range(CONV_W - 1):
            conv = conv + gbuf[pl.ds(halo_rows - (CONV_W - 1) + tap, tm), :] * cw_ref[tap:tap + 1, cols]
        conv = conv + gate * cw_ref[CONV_W - 1:CONV_W, cols]
        act_ref[:, cols] = (_silu(conv) * up).astype(BF16)
    o_ref[...] += jnp.dot(act_ref[...], wd_ref[...], preferred_element_type=F32)


def _ffn(x, a, a_blk, b, b_blk, w_out, g, w_gate, w_up, conv_w, conv_b, w_down, *, seq, tm, tf):
    n, d = x.shape
    dff = w_gate.shape[1]
    kh = w_out.shape[0] // 2
    vmem = (2 * (2 * tm * d * 4 + 2 * tm * kh * 4 + 2 * kh * d * 2 + 3 * d * dff * 2 + 4 * dff * 4)
            + tm * d * 2 + tm * dff * 2 + 2 * (tm + 8) * tf * 4 + 8 * dff * 4 + 8 * tm * tf * 4)
    return pl.pallas_call(
        functools.partial(_ffn_kernel, tiles_per_seq=seq // tm, tf=tf),
        grid=(n // tm,),
        in_specs=[pl.BlockSpec((tm, d), lambda i: (i, 0)),
                  pl.BlockSpec((tm, kh), lambda i: (i, a_blk)),
                  pl.BlockSpec((tm, kh), lambda i: (i, b_blk)),
                  pl.BlockSpec((kh, d), lambda i: (0, 0)),
                  pl.BlockSpec((kh, d), lambda i: (1, 0)),
                  pl.BlockSpec((1, d), lambda i: (0, 0)),
                  pl.BlockSpec((d, dff), lambda i: (0, 0)),
                  pl.BlockSpec((d, dff), lambda i: (0, 0)),
                  pl.BlockSpec((CONV_W, dff), lambda i: (0, 0)),
                  pl.BlockSpec((1, dff), lambda i: (0, 0)),
                  pl.BlockSpec((dff, d), lambda i: (0, 0))],
        out_specs=pl.BlockSpec((tm, d), lambda i: (i, 0)),
        out_shape=jax.ShapeDtypeStruct((n, d), F32),
        scratch_shapes=[pltpu.VMEM((tm, d), BF16), pltpu.VMEM((tm, dff), BF16),
                        pltpu.VMEM((2, V7X_SUBLANES + tm, tf), F32),
                        pltpu.VMEM((V7X_SUBLANES, dff), F32)],
        compiler_params=_params(("arbitrary",), vmem),
        name="conv_ffn",
    )(x, a, b, w_out, w_out, g.reshape(1, d), w_gate, w_up, conv_w, conv_b.reshape(1, dff), w_down)


def kernel(x, norm_mix_g, norm_ffn_g, ev_w_in, ev_a_ln_g, ev_a_ln_b, ev_a_ws, ev_a_bs, ev_q_g, ev_k_g,
           ev_w_out, od_w_in, od_w_a2, od_b_a, od_head_g, od_w_out, ffn_w_gate, ffn_w_up, ffn_conv_w,
           ffn_conv_b, ffn_w_down):
    bsz, seq, d = x.shape
    n = bsz * seq
    depth = norm_mix_g.shape[0]
    main = 2 * C_HEADS * (C_DK + C_DV)
    ev_w_in, ev_w_out, od_w_out = ev_w_in.astype(BF16), ev_w_out.astype(BF16), od_w_out.astype(BF16)
    od_w_all = od_w_in.astype(BF16)
    od_w_side = jnp.pad(od_w_in[:, :, main:], ((0, 0), (0, 0), (0, V7X_LANES - C_GATE_RANK))).astype(BF16)
    ffn_w_gate, ffn_w_up, ffn_w_down = ffn_w_gate.astype(BF16), ffn_w_up.astype(BF16), ffn_w_down.astype(BF16)
    xf = x.reshape(n, d)
    for layer in range(depth):
        if layer % 2 == 0:
            e = layer // 2
            a_out, qb, kn, vn, qp, kp, vp = _even_in(
                xf, norm_mix_g[layer], ev_w_in[e], ev_a_ln_g[e], ev_a_ln_b[e], ev_a_ws[e], ev_a_bs[e],
                ev_q_g[e], ev_k_g[e], bsz=bsz, tm=1024)
            in_seq = lambda t: t.reshape(bsz, seq, -1)
            b_out = _dilated(in_seq(qb), in_seq(kn), in_seq(vn), qp, kp, vp).reshape(n, -1)
            mix = (a_out, 0, b_out, 0, ev_w_out[e])
        else:
            o = layer // 2
            z, ga = _norm_matmul(xf, norm_mix_g[layer], od_w_all[o], main, od_w_side[o], tm=512)
            mixed = _gla(z.reshape(bsz, seq, -1), ga.reshape(bsz, seq, -1), od_w_a2[o], od_b_a[o],
                         od_head_g[o], tc=512).reshape(n, -1)
            mix = (mixed, 0, mixed, 1, od_w_out[o])
        xf = _ffn(xf, *mix, norm_ffn_g[layer], ffn_w_gate[layer], ffn_w_up[layer], ffn_conv_w[layer],
                  ffn_conv_b[layer], ffn_w_down[layer], seq=seq, tm=512, tf=256)
    return xf.reshape(bsz, seq, d)
```

```python
import functools

import jax
import jax.numpy as jnp
from jax import lax
from jax.experimental import pallas as pl
from jax.experimental.pallas import tpu as pltpu

A_GROUPS = 8
A_GROUP_DIM = 64
A_CHUNK = 128
B_HEAD_DIM = 64
B_DILATIONS = (1, 4, 16)
B_BLOCK = 128
C_HEADS = 4
C_DK = 128
C_DV = 256
C_GATE_RANK = 16
C_TAU = 16.0
C_CHUNK = 64
CONV_W = 3
EPS = 1e-6
NEG = -1e30

V7X_LANES = 128
V7X_SUBLANES = 8
V7X_VMEM_BUDGET = 56 * 1024 * 1024

F32 = jnp.float32
BF16 = jnp.bfloat16
NT_DIMS = (((1,), (1,)), ((), ()))
TN_DIMS = (((0,), (0,)), ((), ()))

ROW_SUB = 2 * A_CHUNK
B_PLANES = max(B_DILATIONS)
STAGE_PITCH = 24
B_SCORE_SCALE = B_HEAD_DIM ** -0.5 * 1.4426950408889634


def _params(semantics, vmem_bytes):
    return pltpu.CompilerParams(
        dimension_semantics=semantics,
        vmem_limit_bytes=min(int(vmem_bytes * 1.25) + (4 << 20), V7X_VMEM_BUDGET))


def _layer_spec(w_all, li):
    return pl.BlockSpec((None,) + w_all.shape[1:], lambda i: (li, 0, 0))


def _rms_rows(x, g):
    return x * lax.rsqrt(jnp.mean(x * x, axis=-1, keepdims=True) + EPS) * g


def _gelu(x):
    return 0.5 * x * (1.0 + lax.erf(x * (0.5 ** 0.5)))


def _silu(x):
    return x * jax.nn.sigmoid(x)


def _head_pair_rms(x, g, head0):
    x2 = x * x
    s0 = jnp.sum(jnp.where(head0, x2, 0.0), axis=-1, keepdims=True)
    s1 = jnp.sum(jnp.where(head0, 0.0, x2), axis=-1, keepdims=True)
    ms = jnp.where(head0, s0, s1) * (1.0 / B_HEAD_DIM)
    return x * lax.rsqrt(ms + EPS) * g


def _norm_matmul_kernel(x_ref, g_ref, w_ref, ws_ref, o_ref, os_ref):
    for r in range(x_ref.shape[0] // ROW_SUB):
        rows = slice(r * ROW_SUB, (r + 1) * ROW_SUB)
        h = _rms_rows(x_ref[rows, :], g_ref[...]).astype(BF16)
        o_ref[rows, :] = jnp.dot(h, w_ref[:, :o_ref.shape[1]], preferred_element_type=F32)
        os_ref[rows, :] = jnp.dot(h, ws_ref[...], preferred_element_type=F32)


def _norm_matmul(x, g, w_all, li, f, w_side, *, tm):
    n, d = x.shape
    fs = w_side.shape[1]
    vmem = 2 * (tm * d * 4 + d * (w_all.shape[2] + fs) * 2 + tm * (f + fs) * 4) + 2 * ROW_SUB * f * 4
    return pl.pallas_call(
        _norm_matmul_kernel,
        grid=(n // tm,),
        in_specs=[pl.BlockSpec((tm, d), lambda i: (i, 0)),
                  pl.BlockSpec((1, d), lambda i: (0, 0)),
                  _layer_spec(w_all, li),
                  pl.BlockSpec((d, fs), lambda i: (0, 0))],
        out_specs=(pl.BlockSpec((tm, f), lambda i: (i, 0)), pl.BlockSpec((tm, fs), lambda i: (i, 0))),
        out_shape=(jax.ShapeDtypeStruct((n, f), F32), jax.ShapeDtypeStruct((n, fs), F32)),
        compiler_params=_params(("parallel",), vmem),
        name="norm_matmul",
    )(x, g.reshape(1, d), w_all, w_side)


def _even_in_kernel(x_ref, g_ref, w_ref, lng_ref, lnb_ref, ws_ref, bs_ref, qg_ref, kg_ref,
                    a_ref, qb_ref, kn_ref, vn_ref, qp_ref, kp_ref, vp_ref, h_ref, z_ref, stage_ref):
    tm = x_ref.shape[0]
    aw = a_ref.shape[1]
    bw = kn_ref.shape[1]
    tiles = bw // V7X_LANES
    groups = ROW_SUB // B_PLANES
    per_block = B_BLOCK // B_PLANES
    row = lax.broadcasted_iota(jnp.int32, (A_CHUNK, 2 * A_CHUNK), 0)
    col = lax.broadcasted_iota(jnp.int32, (A_CHUNK, 2 * A_CHUNK), 1)
    causal = (col % A_CHUNK) <= row
    lane2 = lax.broadcasted_iota(jnp.int32, (1, 2 * V7X_LANES), 1)
    first_group = (lane2 % V7X_LANES) < A_GROUP_DIM
    head0 = lax.broadcasted_iota(jnp.int32, (1, V7X_LANES), 1) < B_HEAD_DIM
    w_pairs = [jnp.where(causal, ws_ref[t], 0.0).astype(BF16) for t in range(ws_ref.shape[0])]
    for r in range(tm // ROW_SUB):
        base = r * ROW_SUB
        rows = slice(base, base + ROW_SUB)
        h, z, stage = h_ref.at[r % 2], z_ref.at[r % 2], stage_ref.at[r % 2]
        h[...] = _rms_rows(x_ref[rows, :], g_ref[...]).astype(BF16)
        z[...] = jnp.dot(h[...], w_ref[...], preferred_element_type=F32)

        v = _gelu(z[:, aw:2 * aw])
        vc = v - jnp.mean(v, axis=-1, keepdims=True)
        vn = vc * lax.rsqrt(jnp.mean(vc * vc, axis=-1, keepdims=True) + EPS) * lng_ref[...] + lnb_ref[...]
        for t in range(aw // V7X_LANES):
            lanes = slice(t * V7X_LANES, (t + 1) * V7X_LANES)
            cc = jnp.concatenate([vn[:A_CHUNK, lanes], vn[A_CHUNK:, lanes]], axis=1)
            rhs = jnp.concatenate([jnp.where(first_group, cc, 0.0),
                                   jnp.where(first_group, 0.0, cc)], axis=0).astype(BF16)
            mixed = jnp.dot(w_pairs[t], rhs, preferred_element_type=F32)
            bias = bs_ref[:, lanes]
            for c in range(2):
                chunk = slice(c * A_CHUNK, (c + 1) * A_CHUNK)
                u = _gelu(z[chunk, lanes])
                a_ref[base + c * A_CHUNK:base + (c + 1) * A_CHUNK, lanes] = (
                    u * (mixed[:, c * V7X_LANES:(c + 1) * V7X_LANES] + bias)).astype(BF16)

        for t in range(tiles):
            lanes = slice(t * V7X_LANES, (t + 1) * V7X_LANES)
            q = _head_pair_rms(z[:, 2 * aw + t * V7X_LANES:2 * aw + (t + 1) * V7X_LANES], qg_ref[...], head0)
            q = q * B_SCORE_SCALE
            k = _head_pair_rms(z[:, 2 * aw + bw + t * V7X_LANES:2 * aw + bw + (t + 1) * V7X_LANES],
                               kg_ref[...], head0)
            vb = z[:, 2 * aw + 2 * bw + t * V7X_LANES:2 * aw + 2 * bw + (t + 1) * V7X_LANES]
            kn_ref[rows, lanes] = k.astype(BF16)
            vn_ref[rows, lanes] = vb.astype(BF16)
            for i, val in enumerate((q, k, vb)):
                for grp in range(groups):
                    stage[i * tiles + t, grp * STAGE_PITCH:grp * STAGE_PITCH + B_PLANES, :] = (
                        val[grp * B_PLANES:(grp + 1) * B_PLANES])
            for p in range(B_PLANES):
                sel = pl.ds(p, groups, stride=STAGE_PITCH)
                for i, planes in enumerate((qp_ref, kp_ref, vp_ref)):
                    planes[p, r * groups:(r + 1) * groups, lanes] = stage[i * tiles + t, sel, :].astype(BF16)
            for blk in range(ROW_SUB // B_BLOCK):
                first = blk * per_block * STAGE_PITCH
                slabs = [stage[t, pl.ds(first + p, per_block, stride=STAGE_PITCH), :] for p in range(B_PLANES)]
                qb_ref[base + blk * B_BLOCK:base + (blk + 1) * B_BLOCK, lanes] = (
                    jnp.concatenate(slabs, axis=0).astype(BF16))


def _even_in(x, g, w_all, li, ln_g, ln_b, w_s, b_s, q_g, k_g, *, bsz, tm):
    n, d = x.shape
    seq = n // bsz
    aw = A_GROUPS * A_GROUP_DIM
    f = w_all.shape[2]
    bw = (f - 2 * aw) // 3
    pairs = A_GROUPS // 2
    tiles = seq // tm
    per_plane = tm // B_PLANES
    ws_pairs = w_s.reshape(pairs, 2, A_CHUNK, A_CHUNK).transpose(0, 2, 1, 3).reshape(pairs, A_CHUNK, 2 * A_CHUNK)
    bias = jnp.repeat(b_s.T, A_GROUP_DIM, axis=1)
    qg2 = jnp.tile(q_g, 2).reshape(1, V7X_LANES)
    kg2 = jnp.tile(k_g, 2).reshape(1, V7X_LANES)
    nat_spec = pl.BlockSpec((tm, bw), lambda i: (i, 0))
    plane_spec = pl.BlockSpec((None, B_PLANES, per_plane, bw), lambda i: (i // tiles, 0, i % tiles, 0))
    nat_shape = jax.ShapeDtypeStruct((n, bw), BF16)
    plane_shape = jax.ShapeDtypeStruct((bsz, B_PLANES, seq // B_PLANES, bw), BF16)
    stage_rows = ROW_SUB // B_PLANES * STAGE_PITCH
    vmem = (2 * (tm * d * 4 + d * f * 2 + tm * aw * 2 + 6 * tm * bw * 2)
            + 2 * (ROW_SUB * d * 2 + ROW_SUB * f * 4 + 3 * stage_rows * bw * 4)
            + 2 * ROW_SUB * (2 * aw + 3 * bw) * 4)
    return pl.pallas_call(
        _even_in_kernel,
        grid=(n // tm,),
        in_specs=[pl.BlockSpec((tm, d), lambda i: (i, 0)),
                  pl.BlockSpec((1, d), lambda i: (0, 0)),
                  _layer_spec(w_all, li),
                  pl.BlockSpec((1, aw), lambda i: (0, 0)),
                  pl.BlockSpec((1, aw), lambda i: (0, 0)),
                  pl.BlockSpec((pairs, A_CHUNK, 2 * A_CHUNK), lambda i: (0, 0, 0)),
                  pl.BlockSpec((A_CHUNK, aw), lambda i: (0, 0)),
                  pl.BlockSpec((1, V7X_LANES), lambda i: (0, 0)),
                  pl.BlockSpec((1, V7X_LANES), lambda i: (0, 0))],
        out_specs=(pl.BlockSpec((tm, aw), lambda i: (i, 0)), nat_spec, nat_spec, nat_spec,
                   plane_spec, plane_spec, plane_spec),
        out_shape=(jax.ShapeDtypeStruct((n, aw), BF16), nat_shape, nat_shape, nat_shape,
                   plane_shape, plane_shape, plane_shape),
        scratch_shapes=[pltpu.VMEM((2, ROW_SUB, d), BF16), pltpu.VMEM((2, ROW_SUB, f), F32),
                        pltpu.VMEM((2, 3 * (bw // V7X_LANES), stage_rows, V7X_LANES), F32)],
        compiler_params=_params(("parallel",), vmem),
        name="even_in",
    )(x, g.reshape(1, d), w_all, ln_g.reshape(1, aw), ln_b.reshape(1, aw), ws_pairs, bias, qg2, kg2)


def _dilated_kernel(qb_ref, kn_ref, vn_ref, qp_ref, kp_ref, vp_ref, o_ref, ob_ref, mb_ref, db_ref, mask_ref):
    m_rows = qp_ref.shape[1]
    blk2 = 2 * B_BLOCK
    head0 = lax.broadcasted_iota(jnp.int32, (1, V7X_LANES), 1) < B_HEAD_DIM

    rowi = lax.broadcasted_iota(jnp.int32, (blk2, blk2), 0) % B_BLOCK
    coli = lax.broadcasted_iota(jnp.int32, (blk2, blk2), 1)
    is_cur = coli >= B_BLOCK
    colj = coli % B_BLOCK
    for bi, d in enumerate(B_DILATIONS):
        planes = B_PLANES // d
        mb = B_BLOCK // planes
        i_pos = (rowi % mb) * planes + rowi // mb
        j_pos = colj if d == 1 else (colj % mb) * planes + colj // mb
        band = jnp.where(jnp.where(is_cur, i_pos - j_pos, j_pos - i_pos) >= 0, 1.0, 0.0)
        mask_ref[2 * bi] = jnp.where(is_cur, band, 0.0)
        mask_ref[2 * bi + 1] = band

    ones = jnp.ones((blk2, V7X_LANES), BF16)

    def attend(qb, kcat, vcat, valid):
        zero = jnp.zeros_like(qb)
        q2 = jnp.concatenate([jnp.where(head0, qb, zero), jnp.where(head0, zero, qb)], axis=0)
        s = lax.dot_general(q2, kcat, NT_DIMS, preferred_element_type=F32)
        s = jnp.where(valid, s, NEG)
        m = jnp.max(s, axis=-1, keepdims=True)
        p = jnp.exp2(s - m).astype(BF16)
        pv = jnp.dot(p, jnp.concatenate([vcat, ones], axis=1), preferred_element_type=F32)
        top, bot = pv[:B_BLOCK], pv[B_BLOCK:]
        return (jnp.where(head0, top[:, :V7X_LANES], bot[:, :V7X_LANES]),
                jnp.where(head0, m[:B_BLOCK], m[B_BLOCK:]),
                jnp.where(head0, top[:, V7X_LANES:], bot[:, V7X_LANES:]))

    def token_block(n, carry):
        per_block = B_BLOCK // B_PLANES
        off = pl.multiple_of(n * B_BLOCK, B_BLOCK)
        off_prev = pl.multiple_of(jnp.maximum(n - 1, 0) * B_BLOCK, B_BLOCK)
        off_plane = pl.multiple_of(n * per_block, per_block)
        kcat = jnp.concatenate([kn_ref[pl.ds(off_prev, B_BLOCK), :], kn_ref[pl.ds(off, B_BLOCK), :]], axis=0)
        vcat = jnp.concatenate([vn_ref[pl.ds(off_prev, B_BLOCK), :], vn_ref[pl.ds(off, B_BLOCK), :]], axis=0)
        o, m, den = attend(qb_ref[pl.ds(off, B_BLOCK), :], kcat, vcat, mask_ref[jnp.minimum(n, 1)] > 0.0)
        for p in range(B_PLANES):
            for ref, val in ((ob_ref, o), (mb_ref, m), (db_ref, den)):
                ref[0, p, pl.ds(off_plane, per_block), :] = val[p * per_block:(p + 1) * per_block]
        return carry

    lax.fori_loop(0, kn_ref.shape[0] // B_BLOCK, token_block, 0, unroll=32)

    for bi, d in enumerate(B_DILATIONS):
        if d == 1:
            continue
        planes = B_PLANES // d
        mb = B_BLOCK // planes
        nb = m_rows // mb

        def plane_block(blk, carry, bi=bi, d=d, planes=planes, mb=mb, nb=nb):
            r = blk // nb
            n = blk % nb
            off = pl.multiple_of(n * mb, mb)
            off_prev = pl.multiple_of(jnp.maximum(n - 1, 0) * mb, mb)

            def gather(ref, offs):
                return jnp.concatenate([ref[r + d * a, pl.ds(o, mb), :] for o in offs for a in range(planes)],
                                       axis=0)

            o, m, den = attend(gather(qp_ref, (off,)), gather(kp_ref, (off_prev, off)),
                               gather(vp_ref, (off_prev, off)), mask_ref[2 * bi + jnp.minimum(n, 1)] > 0.0)
            for a in range(planes):
                for ref, val in ((ob_ref, o), (mb_ref, m), (db_ref, den)):
                    ref[bi, r + d * a, pl.ds(off, mb), :] = val[a * mb:(a + 1) * mb]
            return carry

        lax.fori_loop(0, d * nb, plane_block, 0, unroll=32)

    branches = range(len(B_DILATIONS))
    for r in range(B_PLANES):
        mx = functools.reduce(jnp.maximum, [mb_ref[bi, r] for bi in branches])
        es = [jnp.exp2(mb_ref[bi, r] - mx) for bi in branches]
        num = functools.reduce(lambda a, b: a + b, [es[bi] * ob_ref[bi, r] for bi in branches])
        den = functools.reduce(lambda a, b: a + b, [es[bi] * db_ref[bi, r] for bi in branches])
        o_ref[pl.ds(r, m_rows, stride=B_PLANES), :] = num / den


def _dilated(qb, kn, vn, qp, kp, vp):
    bsz, s, bw = kn.shape
    pairs = bw // V7X_LANES
    m_rows = s // B_PLANES
    tile_f32 = s * V7X_LANES * 4
    vmem = 2 * (6 * tile_f32 // 2 + tile_f32) + 9 * tile_f32 + 6 * 4 * B_BLOCK * B_BLOCK * 4 + 8 * tile_f32 // 16
    nat_spec = pl.BlockSpec((None, s, V7X_LANES), lambda b, p: (b, 0, p))
    plane_spec = pl.BlockSpec((None, B_PLANES, m_rows, V7X_LANES), lambda b, p: (b, 0, 0, p))
    return pl.pallas_call(
        _dilated_kernel,
        grid=(bsz, pairs),
        in_specs=[nat_spec, nat_spec, nat_spec, plane_spec, plane_spec, plane_spec],
        out_specs=pl.BlockSpec((None, s, V7X_LANES), lambda b, p: (b, 0, p)),
        out_shape=jax.ShapeDtypeStruct((bsz, s, bw), F32),
        scratch_shapes=[pltpu.VMEM((len(B_DILATIONS), B_PLANES, m_rows, V7X_LANES), F32)] * 3
                       + [pltpu.VMEM((2 * len(B_DILATIONS), 2 * B_BLOCK, 2 * B_BLOCK), F32)],
        compiler_params=_params(("parallel", "parallel"), vmem),
        name="dilated_attention",
    )(qb, kn, vn, qp, kp, vp)


def _gla_kernel(q_ref, k_ref, v_ref, r_ref, ga_ref, wa_ref, ba_ref, hg_ref, o_ref, st_ref):
    tc = q_ref.shape[0]
    chunks = tc // C_CHUNK

    @pl.when(pl.program_id(1) == 0)
    def _():
        st_ref[...] = jnp.zeros_like(st_ref)

    gate = jnp.dot(ga_ref[...].astype(BF16), wa_ref[...], preferred_element_type=F32) + ba_ref[...]
    log_a = jax.nn.log_sigmoid(gate) / C_TAU
    ci = lax.broadcasted_iota(jnp.int32, (C_CHUNK, C_CHUNK), 0)
    cj = lax.broadcasted_iota(jnp.int32, (C_CHUNK, C_CHUNK), 1)
    causal = ci >= cj
    tri = jnp.where(causal, 1.0, 0.0).astype(BF16)
    rows = [slice(c * C_CHUNK, (c + 1) * C_CHUNK) for c in range(chunks)]
    for hd in range(C_HEADS):
        kl = slice(hd * C_DK, (hd + 1) * C_DK)
        vl = slice(hd * C_DV, (hd + 1) * C_DV)
        la = log_a[:, kl]
        hi = la.astype(BF16)
        rest = la - hi.astype(F32)
        mid = rest.astype(BF16)
        lo = (rest - mid.astype(F32)).astype(BF16)
        pieces = jnp.concatenate([hi, mid, lo], axis=1)
        sums = [jnp.dot(tri, pieces[sl], preferred_element_type=F32) for sl in rows]
        b = jnp.concatenate([s3[:, :C_DK] + s3[:, C_DK:2 * C_DK] + s3[:, 2 * C_DK:] for s3 in sums], axis=0)
        b3 = b.reshape(chunks, C_CHUNK, C_DK)
        b_last = b3[:, C_CHUNK - 1:C_CHUNK, :]
        k = k_ref[:, kl]
        q_t = ((q_ref[:, kl] * (C_DK ** -0.5)) * jnp.exp(b)).astype(BF16)
        k_t = (k * jnp.exp(-b)).astype(BF16)
        k_s = (k.reshape(chunks, C_CHUNK, C_DK) * jnp.exp(b_last - b3)).reshape(tc, C_DK).astype(BF16)
        decay = jnp.exp(b_last)
        o_intra, kv_t = [], []
        for sl in rows:
            v_c = v_ref[sl, vl].astype(BF16)
            attn = lax.dot_general(q_t[sl], k_t[sl], NT_DIMS, preferred_element_type=F32)
            attn = jnp.where(causal, attn, 0.0).astype(BF16)
            o_intra.append(jnp.dot(attn, v_c, preferred_element_type=F32))
            kv_t.append(lax.dot_general(v_c, k_s[sl], TN_DIMS, preferred_element_type=F32))
        st = st_ref[hd]
        entering = []
        for c in range(chunks):
            entering.append(st.astype(BF16))
            st = st * decay[c] + kv_t[c]
        st_ref[hd] = st
        for c, sl in enumerate(rows):
            o = o_intra[c] + lax.dot_general(q_t[sl], entering[c], NT_DIMS, preferred_element_type=F32)
            o_ref[sl, vl] = (_rms_rows(o, hg_ref[...]) * _silu(r_ref[sl, vl])).astype(BF16)


def _gla(z3, ga3, w_a2, b_a, head_g, *, tc):
    bsz, s, _ = z3.shape
    hk, hv = C_HEADS * C_DK, C_HEADS * C_DV
    wa = jnp.pad(w_a2, ((0, V7X_LANES - C_GATE_RANK), (0, 0))).astype(BF16)
    vmem = (2 * (2 * tc * hk * 4 + 3 * tc * hv * 4 + tc * V7X_LANES * 4 + V7X_LANES * hk * 2)
            + hv * C_DK * 4 + 12 * tc * C_DK * 4 + 2 * tc * hk * 4)
    return pl.pallas_call(
        _gla_kernel,
        grid=(bsz, s // tc),
        in_specs=[pl.BlockSpec((None, tc, hk), lambda b, t: (b, t, 0)),
                  pl.BlockSpec((None, tc, hk), lambda b, t: (b, t, 1)),
                  pl.BlockSpec((None, tc, hv), lambda b, t: (b, t, 1)),
                  pl.BlockSpec((None, tc, hv), lambda b, t: (b, t, 2)),
                  pl.BlockSpec((None, tc, V7X_LANES), lambda b, t: (b, t, 0)),
                  pl.BlockSpec((V7X_LANES, hk), lambda b, t: (0, 0)),
                  pl.BlockSpec((1, hk), lambda b, t: (0, 0)),
                  pl.BlockSpec((1, C_DV), lambda b, t: (0, 0))],
        out_specs=pl.BlockSpec((None, tc, hv), lambda b, t: (b, t, 0)),
        out_shape=jax.ShapeDtypeStruct((bsz, s, hv), BF16),
        scratch_shapes=[pltpu.VMEM((C_HEADS, C_DV, C_DK), F32)],
        compiler_params=_params(("parallel", "arbitrary"), vmem),
        name="gla",
    )(z3, z3, z3, z3, ga3, wa, b_a.reshape(1, -1), head_g.reshape(1, C_DV))


def _ffn_kernel(x_ref, a_ref, b_ref, wa_ref, wb_ref, g_ref, wg_ref, wu_ref, cw_ref, cb_ref, wd_ref, o_ref,
                h_ref, act_ref, gbuf_ref, halo_ref, *, tiles_per_seq, tf):
    tm = x_ref.shape[0]
    halo_rows = V7X_SUBLANES
    for r in range(tm // ROW_SUB):
        rows = slice(r * ROW_SUB, (r + 1) * ROW_SUB)
        x1 = x_ref[rows, :] + jnp.dot(a_ref[rows, :].astype(BF16), wa_ref[...], preferred_element_type=F32)
        x1 = x1 + jnp.dot(b_ref[rows, :].astype(BF16), wb_ref[...], preferred_element_type=F32)
        o_ref[rows, :] = x1
        h_ref[rows, :] = _rms_rows(x1, g_ref[...]).astype(BF16)

    @pl.when(pl.program_id(0) % tiles_per_seq == 0)
    def _():
        halo_ref[...] = jnp.zeros_like(halo_ref)

    for j in range(wg_ref.shape[1] // tf):
        cols = slice(j * tf, (j + 1) * tf)
        h = h_ref[...]
        gate = jnp.dot(h, wg_ref[:, cols], preferred_element_type=F32)
        up = jnp.dot(h, wu_ref[:, cols], preferred_element_type=F32)
        gbuf = gbuf_ref.at[j % 2]
        gbuf[0:halo_rows, :] = halo_ref[:, cols]
        gbuf[halo_rows:, :] = gate
        halo_ref[:, cols] = gate[tm - halo_rows:, :]
        conv = cb_ref[:, cols]
        for tap in range(CONV_W - 1):
            conv = conv + gbuf[pl.ds(halo_rows - (CONV_W - 1) + tap, tm), :] * cw_ref[tap:tap + 1, cols]
        conv = conv + gate * cw_ref[CONV_W - 1:CONV_W, cols]
        act_ref[:, cols] = (_silu(conv) * up).astype(BF16)
    o_ref[...] += jnp.dot(act_ref[...], wd_ref[...], preferred_element_type=F32)


def _ffn(x, a, a_blk, b, b_blk, w_out_all, lo, g, w_gate_all, w_up_all, conv_w, conv_b, w_down_all, lf,
         *, seq, tm, tf):
    n, d = x.shape
    dff = w_gate_all.shape[2]
    kh = w_out_all.shape[1] // 2
    vmem = (2 * (2 * tm * d * 4 + 2 * tm * kh * 4 + 2 * kh * d * 2 + 3 * d * dff * 2 + 4 * dff * 4)
            + tm * d * 2 + tm * dff * 2 + 2 * (tm + 8) * tf * 4 + 8 * dff * 4 + 8 * tm * tf * 4)
    return pl.pallas_call(
        functools.partial(_ffn_kernel, tiles_per_seq=seq // tm, tf=tf),
        grid=(n // tm,),
        in_specs=[pl.BlockSpec((tm, d), lambda i: (i, 0)),
                  pl.BlockSpec((tm, kh), lambda i: (i, a_blk)),
                  pl.BlockSpec((tm, kh), lambda i: (i, b_blk)),
                  pl.BlockSpec((None, kh, d), lambda i: (lo, 0, 0)),
                  pl.BlockSpec((None, kh, d), lambda i: (lo, 1, 0)),
                  pl.BlockSpec((1, d), lambda i: (0, 0)),
                  _layer_spec(w_gate_all, lf),
                  _layer_spec(w_up_all, lf),
                  pl.BlockSpec((CONV_W, dff), lambda i: (0, 0)),
                  pl.BlockSpec((1, dff), lambda i: (0, 0)),
                  _layer_spec(w_down_all, lf)],
        out_specs=pl.BlockSpec((tm, d), lambda i: (i, 0)),
        out_shape=jax.ShapeDtypeStruct((n, d), F32),
        scratch_shapes=[pltpu.VMEM((tm, d), BF16), pltpu.VMEM((tm, dff), BF16),
                        pltpu.VMEM((2, V7X_SUBLANES + tm, tf), F32),
                        pltpu.VMEM((V7X_SUBLANES, dff), F32)],
        compiler_params=_params(("arbitrary",), vmem),
        name="conv_ffn",
    )(x, a, b, w_out_all, w_out_all, g.reshape(1, d), w_gate_all, w_up_all, conv_w, conv_b.reshape(1, dff),
      w_down_all)


def kernel(x, norm_mix_g, norm_ffn_g, ev_w_in, ev_a_ln_g, ev_a_ln_b, ev_a_ws, ev_a_bs, ev_q_g, ev_k_g,
           ev_w_out, od_w_in, od_w_a2, od_b_a, od_head_g, od_w_out, ffn_w_gate, ffn_w_up, ffn_conv_w,
           ffn_conv_b, ffn_w_down):
    bsz, seq, d = x.shape
    n = bsz * seq
    depth = norm_mix_g.shape[0]
    main = 2 * C_HEADS * (C_DK + C_DV)
    ev_w_in, ev_w_out, od_w_out = ev_w_in.astype(BF16), ev_w_out.astype(BF16), od_w_out.astype(BF16)
    od_w_all = od_w_in.astype(BF16)
    od_w_side = jnp.pad(od_w_in[:, :, main:], ((0, 0), (0, 0), (0, V7X_LANES - C_GATE_RANK))).astype(BF16)
    ffn_w_gate, ffn_w_up, ffn_w_down = ffn_w_gate.astype(BF16), ffn_w_up.astype(BF16), ffn_w_down.astype(BF16)
    xf = x.reshape(n, d)
    for layer in range(depth):
        if layer % 2 == 0:
            e = layer // 2
            a_out, qb, kn, vn, qp, kp, vp = _even_in(
                xf, norm_mix_g[layer], ev_w_in, e, ev_a_ln_g[e], ev_a_ln_b[e], ev_a_ws[e], ev_a_bs[e],
                ev_q_g[e], ev_k_g[e], bsz=bsz, tm=1024)
            in_seq = lambda t: t.reshape(bsz, seq, -1)
            b_out = _dilated(in_seq(qb), in_seq(kn), in_seq(vn), qp, kp, vp).reshape(n, -1)
            mix = (a_out, 0, b_out, 0, ev_w_out, e)
        else:
            o = layer // 2
            z, ga = _norm_matmul(xf, norm_mix_g[layer], od_w_all, o, main, od_w_side[o], tm=512)
            mixed = _gla(z.reshape(bsz, seq, -1), ga.reshape(bsz, seq, -1), od_w_a2[o], od_b_a[o],
                         od_head_g[o], tc=512).reshape(n, -1)
            mix = (mixed, 0, mixed, 1, od_w_out, o)
        xf = _ffn(xf, *mix, norm_ffn_g[layer], ffn_w_gate, ffn_w_up, ffn_conv_w[layer], ffn_conv_b[layer],
                  ffn_w_down, layer, seq=seq, tm=512, tf=256)
    return xf.reshape(bsz, seq, d)
```

```python
import functools

import jax
import jax.numpy as jnp
from jax import lax
from jax.experimental import pallas as pl
from jax.experimental.pallas import tpu as pltpu

A_GROUPS = 8
A_GROUP_DIM = 64
A_CHUNK = 128
B_HEAD_DIM = 64
B_DILATIONS = (1, 4, 16)
B_BLOCK = 128
C_HEADS = 4
C_DK = 128
C_DV = 256
C_GATE_RANK = 16
C_TAU = 16.0
C_CHUNK = 64
CONV_W = 3
EPS = 1e-6
NEG = -1e30

V7X_LANES = 128
V7X_SUBLANES = 8
V7X_VMEM_BUDGET = 56 * 1024 * 1024

F32 = jnp.float32
BF16 = jnp.bfloat16
NT_DIMS = (((1,), (1,)), ((), ()))
TN_DIMS = (((0,), (0,)), ((), ()))

ROW_SUB = 2 * A_CHUNK
B_PLANES = max(B_DILATIONS)
STAGE_PITCH = 24
B_SCORE_SCALE = B_HEAD_DIM ** -0.5 * 1.4426950408889634


def _params(semantics, vmem_bytes):
    return pltpu.CompilerParams(
        dimension_semantics=semantics,
        vmem_limit_bytes=min(int(vmem_bytes * 1.25) + (4 << 20), V7X_VMEM_BUDGET))


def _layer_spec(w_all, li):
    return pl.BlockSpec((None,) + w_all.shape[1:], lambda i: (li, 0, 0))


def _rms_rows(x, g):
    return x * lax.rsqrt(jnp.mean(x * x, axis=-1, keepdims=True) + EPS) * g


def _gelu(x):
    return 0.5 * x * (1.0 + lax.erf(x * (0.5 ** 0.5)))


def _silu(x):
    return x * jax.nn.sigmoid(x)


def _head_pair_rms(x, g, head0):
    x2 = x * x
    s0 = jnp.sum(jnp.where(head0, x2, 0.0), axis=-1, keepdims=True)
    s1 = jnp.sum(jnp.where(head0, 0.0, x2), axis=-1, keepdims=True)
    ms = jnp.where(head0, s0, s1) * (1.0 / B_HEAD_DIM)
    return x * lax.rsqrt(ms + EPS) * g


def _norm_matmul_kernel(x_ref, g_ref, w_ref, ws_ref, o_ref, os_ref):
    for r in range(x_ref.shape[0] // ROW_SUB):
        rows = slice(r * ROW_SUB, (r + 1) * ROW_SUB)
        h = _rms_rows(x_ref[rows, :], g_ref[...]).astype(BF16)
        o_ref[rows, :] = jnp.dot(h, w_ref[:, :o_ref.shape[1]], preferred_element_type=F32)
        os_ref[rows, :] = jnp.dot(h, ws_ref[...], preferred_element_type=F32)


def _norm_matmul(x, g, w_all, li, f, w_side, *, tm):
    n, d = x.shape
    fs = w_side.shape[1]
    vmem = 2 * (tm * d * 4 + d * (w_all.shape[2] + fs) * 2 + tm * (f + fs) * 4) + 2 * ROW_SUB * f * 4
    return pl.pallas_call(
        _norm_matmul_kernel,
        grid=(n // tm,),
        in_specs=[pl.BlockSpec((tm, d), lambda i: (i, 0)),
                  pl.BlockSpec((1, d), lambda i: (0, 0)),
                  _layer_spec(w_all, li),
                  pl.BlockSpec((d, fs), lambda i: (0, 0))],
        out_specs=(pl.BlockSpec((tm, f), lambda i: (i, 0)), pl.BlockSpec((tm, fs), lambda i: (i, 0))),
        out_shape=(jax.ShapeDtypeStruct((n, f), F32), jax.ShapeDtypeStruct((n, fs), F32)),
        compiler_params=_params(("parallel",), vmem),
        name="norm_matmul",
    )(x, g.reshape(1, d), w_all, w_side)


def _even_in_kernel(x_ref, g_ref, w_ref, lng_ref, lnb_ref, ws_ref, bs_ref, qg_ref, kg_ref,
                    a_ref, qb_ref, kn_ref, vn_ref, qp_ref, kp_ref, vp_ref, h_ref, z_ref, stage_ref):
    tm = x_ref.shape[0]
    aw = a_ref.shape[1]
    bw = kn_ref.shape[1]
    tiles = bw // V7X_LANES
    groups = ROW_SUB // B_PLANES
    per_block = B_BLOCK // B_PLANES
    row = lax.broadcasted_iota(jnp.int32, (A_CHUNK, 2 * A_CHUNK), 0)
    col = lax.broadcasted_iota(jnp.int32, (A_CHUNK, 2 * A_CHUNK), 1)
    causal = (col % A_CHUNK) <= row
    lane2 = lax.broadcasted_iota(jnp.int32, (1, 2 * V7X_LANES), 1)
    first_group = (lane2 % V7X_LANES) < A_GROUP_DIM
    head0 = lax.broadcasted_iota(jnp.int32, (1, V7X_LANES), 1) < B_HEAD_DIM
    w_pairs = [jnp.where(causal, ws_ref[t], 0.0).astype(BF16) for t in range(ws_ref.shape[0])]
    for r in range(tm // ROW_SUB):
        base = r * ROW_SUB
        rows = slice(base, base + ROW_SUB)
        h, z, stage = h_ref.at[r % 2], z_ref.at[r % 2], stage_ref.at[r % 2]
        h[...] = _rms_rows(x_ref[rows, :], g_ref[...]).astype(BF16)
        z[...] = jnp.dot(h[...], w_ref[...], preferred_element_type=F32)

        v = _gelu(z[:, aw:2 * aw])
        vc = v - jnp.mean(v, axis=-1, keepdims=True)
        vn = vc * lax.rsqrt(jnp.mean(vc * vc, axis=-1, keepdims=True) + EPS) * lng_ref[...] + lnb_ref[...]
        for t in range(aw // V7X_LANES):
            lanes = slice(t * V7X_LANES, (t + 1) * V7X_LANES)
            cc = jnp.concatenate([vn[:A_CHUNK, lanes], vn[A_CHUNK:, lanes]], axis=1)
            rhs = jnp.concatenate([jnp.where(first_group, cc, 0.0),
                                   jnp.where(first_group, 0.0, cc)], axis=0).astype(BF16)
            mixed = jnp.dot(w_pairs[t], rhs, preferred_element_type=F32)
            bias = bs_ref[:, lanes]
            for c in range(2):
                chunk = slice(c * A_CHUNK, (c + 1) * A_CHUNK)
                u = _gelu(z[chunk, lanes])
                a_ref[base + c * A_CHUNK:base + (c + 1) * A_CHUNK, lanes] = (
                    u * (mixed[:, c * V7X_LANES:(c + 1) * V7X_LANES] + bias)).astype(BF16)

        for t in range(tiles):
            lanes = slice(t * V7X_LANES, (t + 1) * V7X_LANES)
            q = _head_pair_rms(z[:, 2 * aw + t * V7X_LANES:2 * aw + (t + 1) * V7X_LANES], qg_ref[...], head0)
            q = q * B_SCORE_SCALE
            k = _head_pair_rms(z[:, 2 * aw + bw + t * V7X_LANES:2 * aw + bw + (t + 1) * V7X_LANES],
                               kg_ref[...], head0)
            vb = z[:, 2 * aw + 2 * bw + t * V7X_LANES:2 * aw + 2 * bw + (t + 1) * V7X_LANES]
            kn_ref[rows, lanes] = k.astype(BF16)
            vn_ref[rows, lanes] = vb.astype(BF16)
            for i, val in enumerate((q, k, vb)):
                for grp in range(groups):
                    stage[i * tiles + t, grp * STAGE_PITCH:grp * STAGE_PITCH + B_PLANES, :] = (
                        val[grp * B_PLANES:(grp + 1) * B_PLANES])
            for p in range(B_PLANES):
                sel = pl.ds(p, groups, stride=STAGE_PITCH)
                for i, planes in enumerate((qp_ref, kp_ref, vp_ref)):
                    planes[p, r * groups:(r + 1) * groups, lanes] = stage[i * tiles + t, sel, :].astype(BF16)
            for blk in range(ROW_SUB // B_BLOCK):
                first = blk * per_block * STAGE_PITCH
                slabs = [stage[t, pl.ds(first + p, per_block, stride=STAGE_PITCH), :] for p in range(B_PLANES)]
                qb_ref[base + blk * B_BLOCK:base + (blk + 1) * B_BLOCK, lanes] = (
                    jnp.concatenate(slabs, axis=0).astype(BF16))


def _even_in(x, g, w_all, li, ln_g, ln_b, w_s, b_s, q_g, k_g, *, bsz, tm):
    n, d = x.shape
    seq = n // bsz
    aw = A_GROUPS * A_GROUP_DIM
    f = w_all.shape[2]
    bw = (f - 2 * aw) // 3
    pairs = A_GROUPS // 2
    tiles = seq // tm
    per_plane = tm // B_PLANES
    ws_pairs = w_s.reshape(pairs, 2, A_CHUNK, A_CHUNK).transpose(0, 2, 1, 3).reshape(pairs, A_CHUNK, 2 * A_CHUNK)
    bias = jnp.repeat(b_s.T, A_GROUP_DIM, axis=1)
    qg2 = jnp.tile(q_g, 2).reshape(1, V7X_LANES)
    kg2 = jnp.tile(k_g, 2).reshape(1, V7X_LANES)
    nat_spec = pl.BlockSpec((tm, bw), lambda i: (i, 0))
    plane_spec = pl.BlockSpec((None, B_PLANES, per_plane, bw), lambda i: (i // tiles, 0, i % tiles, 0))
    nat_shape = jax.ShapeDtypeStruct((n, bw), BF16)
    plane_shape = jax.ShapeDtypeStruct((bsz, B_PLANES, seq // B_PLANES, bw), BF16)
    stage_rows = ROW_SUB // B_PLANES * STAGE_PITCH
    vmem = (2 * (tm * d * 4 + d * f * 2 + tm * aw * 2 + 6 * tm * bw * 2)
            + 2 * (ROW_SUB * d * 2 + ROW_SUB * f * 4 + 3 * stage_rows * bw * 4)
            + 2 * ROW_SUB * (2 * aw + 3 * bw) * 4)
    return pl.pallas_call(
        _even_in_kernel,
        grid=(n // tm,),
        in_specs=[pl.BlockSpec((tm, d), lambda i: (i, 0)),
                  pl.BlockSpec((1, d), lambda i: (0, 0)),
                  _layer_spec(w_all, li),
                  pl.BlockSpec((1, aw), lambda i: (0, 0)),
                  pl.BlockSpec((1, aw), lambda i: (0, 0)),
                  pl.BlockSpec((pairs, A_CHUNK, 2 * A_CHUNK), lambda i: (0, 0, 0)),
                  pl.BlockSpec((A_CHUNK, aw), lambda i: (0, 0)),
                  pl.BlockSpec((1, V7X_LANES), lambda i: (0, 0)),
                  pl.BlockSpec((1, V7X_LANES), lambda i: (0, 0))],
        out_specs=(pl.BlockSpec((tm, aw), lambda i: (i, 0)), nat_spec, nat_spec, nat_spec,
                   plane_spec, plane_spec, plane_spec),
        out_shape=(jax.ShapeDtypeStruct((n, aw), BF16), nat_shape, nat_shape, nat_shape,
                   plane_shape, plane_shape, plane_shape),
        scratch_shapes=[pltpu.VMEM((2, ROW_SUB, d), BF16), pltpu.VMEM((2, ROW_SUB, f), F32),
                        pltpu.VMEM((2, 3 * (bw // V7X_LANES), stage_rows, V7X_LANES), F32)],
        compiler_params=_params(("parallel",), vmem),
        name="even_in",
    )(x, g.reshape(1, d), w_all, ln_g.reshape(1, aw), ln_b.reshape(1, aw), ws_pairs, bias, qg2, kg2)


def _dilated_kernel(qb_ref, kn_ref, vn_ref, qp_ref, kp_ref, vp_ref, o_ref, ob_ref, mb_ref, db_ref, mask_ref):
    m_rows = qp_ref.shape[1]
    blk2 = 2 * B_BLOCK
    head0 = lax.broadcasted_iota(jnp.int32, (1, V7X_LANES), 1) < B_HEAD_DIM

    rowi = lax.broadcasted_iota(jnp.int32, (blk2, blk2), 0) % B_BLOCK
    coli = lax.broadcasted_iota(jnp.int32, (blk2, blk2), 1)
    is_cur = coli >= B_BLOCK
    colj = coli % B_BLOCK
    for bi, d in enumerate(B_DILATIONS):
        planes = B_PLANES // d
        mb = B_BLOCK // planes
        i_pos = (rowi % mb) * planes + rowi // mb
        j_pos = colj if d == 1 else (colj % mb) * planes + colj // mb
        band = jnp.where(jnp.where(is_cur, i_pos - j_pos, j_pos - i_pos) >= 0, 0.0, NEG)
        mask_ref[2 * bi] = jnp.where(is_cur, band, NEG)
        mask_ref[2 * bi + 1] = band

    ones = jnp.ones((blk2, V7X_LANES), BF16)

    def attend(qb, kcat, vcat, bias):
        zero = jnp.zeros_like(qb)
        q2 = jnp.concatenate([jnp.where(head0, qb, zero), jnp.where(head0, zero, qb)], axis=0)
        s = lax.dot_general(q2, kcat, NT_DIMS, preferred_element_type=F32) + bias
        m = jnp.max(s, axis=-1, keepdims=True)
        p = jnp.exp2(s - m).astype(BF16)
        pv = jnp.dot(p, jnp.concatenate([vcat, ones], axis=1), preferred_element_type=F32)
        top, bot = pv[:B_BLOCK], pv[B_BLOCK:]
        return (jnp.where(head0, top[:, :V7X_LANES], bot[:, :V7X_LANES]),
                jnp.where(head0, m[:B_BLOCK], m[B_BLOCK:]),
                jnp.where(head0, top[:, V7X_LANES:], bot[:, V7X_LANES:]))

    def token_block(n, carry):
        per_block = B_BLOCK // B_PLANES
        off = pl.multiple_of(n * B_BLOCK, B_BLOCK)
        off_prev = pl.multiple_of(jnp.maximum(n - 1, 0) * B_BLOCK, B_BLOCK)
        off_plane = pl.multiple_of(n * per_block, per_block)
        kcat = jnp.concatenate([kn_ref[pl.ds(off_prev, B_BLOCK), :], kn_ref[pl.ds(off, B_BLOCK), :]], axis=0)
        vcat = jnp.concatenate([vn_ref[pl.ds(off_prev, B_BLOCK), :], vn_ref[pl.ds(off, B_BLOCK), :]], axis=0)
        o, m, den = attend(qb_ref[pl.ds(off, B_BLOCK), :], kcat, vcat, mask_ref[jnp.minimum(n, 1)])
        for p in range(B_PLANES):
            for ref, val in ((ob_ref, o), (mb_ref, m), (db_ref, den)):
                ref[0, p, pl.ds(off_plane, per_block), :] = val[p * per_block:(p + 1) * per_block]
        return carry

    lax.fori_loop(0, kn_ref.shape[0] // B_BLOCK, token_block, 0, unroll=32)

    for bi, d in enumerate(B_DILATIONS):
        if d == 1:
            continue
        planes = B_PLANES // d
        mb = B_BLOCK // planes
        nb = m_rows // mb

        def plane_block(blk, carry, bi=bi, d=d, planes=planes, mb=mb, nb=nb):
            r = blk // nb
            n = blk % nb
            off = pl.multiple_of(n * mb, mb)
            off_prev = pl.multiple_of(jnp.maximum(n - 1, 0) * mb, mb)

            def gather(ref, offs):
                return jnp.concatenate([ref[r + d * a, pl.ds(o, mb), :] for o in offs for a in range(planes)],
                                       axis=0)

            o, m, den = attend(gather(qp_ref, (off,)), gather(kp_ref, (off_prev, off)),
                               gather(vp_ref, (off_prev, off)), mask_ref[2 * bi + jnp.minimum(n, 1)])
            for a in range(planes):
                for ref, val in ((ob_ref, o), (mb_ref, m), (db_ref, den)):
                    ref[bi, r + d * a, pl.ds(off, mb), :] = val[a * mb:(a + 1) * mb]
            return carry

        lax.fori_loop(0, d * nb, plane_block, 0, unroll=32)

    branches = range(len(B_DILATIONS))
    for r in range(B_PLANES):
        mx = functools.reduce(jnp.maximum, [mb_ref[bi, r] for bi in branches])
        es = [jnp.exp2(mb_ref[bi, r] - mx) for bi in branches]
        num = functools.reduce(lambda a, b: a + b, [es[bi] * ob_ref[bi, r] for bi in branches])
        den = functools.reduce(lambda a, b: a + b, [es[bi] * db_ref[bi, r] for bi in branches])
        o_ref[pl.ds(r, m_rows, stride=B_PLANES), :] = num / den


def _dilated(qb, kn, vn, qp, kp, vp):
    bsz, s, bw = kn.shape
    pairs = bw // V7X_LANES
    m_rows = s // B_PLANES
    tile_f32 = s * V7X_LANES * 4
    vmem = 2 * (6 * tile_f32 // 2 + tile_f32) + 9 * tile_f32 + 6 * 4 * B_BLOCK * B_BLOCK * 4 + 8 * tile_f32 // 16
    nat_spec = pl.BlockSpec((None, s, V7X_LANES), lambda b, p: (b, 0, p))
    plane_spec = pl.BlockSpec((None, B_PLANES, m_rows, V7X_LANES), lambda b, p: (b, 0, 0, p))
    return pl.pallas_call(
        _dilated_kernel,
        grid=(bsz, pairs),
        in_specs=[nat_spec, nat_spec, nat_spec, plane_spec, plane_spec, plane_spec],
        out_specs=pl.BlockSpec((None, s, V7X_LANES), lambda b, p: (b, 0, p)),
        out_shape=jax.ShapeDtypeStruct((bsz, s, bw), F32),
        scratch_shapes=[pltpu.VMEM((len(B_DILATIONS), B_PLANES, m_rows, V7X_LANES), F32)] * 3
                       + [pltpu.VMEM((2 * len(B_DILATIONS), 2 * B_BLOCK, 2 * B_BLOCK), F32)],
        compiler_params=_params(("parallel", "parallel"), vmem),
        name="dilated_attention",
    )(qb, kn, vn, qp, kp, vp)


def _gla_kernel(q_ref, k_ref, v_ref, r_ref, ga_ref, wa_ref, ba_ref, hg_ref, o_ref, st_ref):
    tc = q_ref.shape[0]
    chunks = tc // C_CHUNK

    @pl.when(pl.program_id(1) == 0)
    def _():
        st_ref[...] = jnp.zeros_like(st_ref)

    gate = jnp.dot(ga_ref[...].astype(BF16), wa_ref[...], preferred_element_type=F32) + ba_ref[...]
    log_a = jax.nn.log_sigmoid(gate) / C_TAU
    ci = lax.broadcasted_iota(jnp.int32, (C_CHUNK, C_CHUNK), 0)
    cj = lax.broadcasted_iota(jnp.int32, (C_CHUNK, C_CHUNK), 1)
    causal = ci >= cj
    tri = jnp.where(causal, 1.0, 0.0).astype(BF16)
    rows = [slice(c * C_CHUNK, (c + 1) * C_CHUNK) for c in range(chunks)]
    for hd in range(C_HEADS):
        kl = slice(hd * C_DK, (hd + 1) * C_DK)
        vl = slice(hd * C_DV, (hd + 1) * C_DV)
        la = log_a[:, kl]
        hi = la.astype(BF16)
        rest = la - hi.astype(F32)
        mid = rest.astype(BF16)
        lo = (rest - mid.astype(F32)).astype(BF16)
        pieces = jnp.concatenate([hi, mid, lo], axis=1)
        sums = [jnp.dot(tri, pieces[sl], preferred_element_type=F32) for sl in rows]
        b = jnp.concatenate([s3[:, :C_DK] + s3[:, C_DK:2 * C_DK] + s3[:, 2 * C_DK:] for s3 in sums], axis=0)
        b3 = b.reshape(chunks, C_CHUNK, C_DK)
        b_last = b3[:, C_CHUNK - 1:C_CHUNK, :]
        k = k_ref[:, kl]
        q_t = ((q_ref[:, kl] * (C_DK ** -0.5)) * jnp.exp(b)).astype(BF16)
        k_t = (k * jnp.exp(-b)).astype(BF16)
        k_s = (k.reshape(chunks, C_CHUNK, C_DK) * jnp.exp(b_last - b3)).reshape(tc, C_DK).astype(BF16)
        decay = jnp.exp(b_last)
        o_intra, kv_t = [], []
        for sl in rows:
            v_c = v_ref[sl, vl].astype(BF16)
            attn = lax.dot_general(q_t[sl], k_t[sl], NT_DIMS, preferred_element_type=F32)
            attn = jnp.where(causal, attn, 0.0).astype(BF16)
            o_intra.append(jnp.dot(attn, v_c, preferred_element_type=F32))
            kv_t.append(lax.dot_general(v_c, k_s[sl], TN_DIMS, preferred_element_type=F32))
        st = st_ref[hd]
        entering = []
        for c in range(chunks):
            entering.append(st.astype(BF16))
            st = st * decay[c] + kv_t[c]
        st_ref[hd] = st
        for c, sl in enumerate(rows):
            o = o_intra[c] + lax.dot_general(q_t[sl], entering[c], NT_DIMS, preferred_element_type=F32)
            o_ref[sl, vl] = (_rms_rows(o, hg_ref[...]) * _silu(r_ref[sl, vl])).astype(BF16)


def _gla(z3, ga3, w_a2, b_a, head_g, *, tc):
    bsz, s, _ = z3.shape
    hk, hv = C_HEADS * C_DK, C_HEADS * C_DV
    wa = jnp.pad(w_a2, ((0, V7X_LANES - C_GATE_RANK), (0, 0))).astype(BF16)
    vmem = (2 * (2 * tc * hk * 4 + 3 * tc * hv * 4 + tc * V7X_LANES * 4 + V7X_LANES * hk * 2)
            + hv * C_DK * 4 + 12 * tc * C_DK * 4 + 2 * tc * hk * 4)
    return pl.pallas_call(
        _gla_kernel,
        grid=(bsz, s // tc),
        in_specs=[pl.BlockSpec((None, tc, hk), lambda b, t: (b, t, 0)),
                  pl.BlockSpec((None, tc, hk), lambda b, t: (b, t, 1)),
                  pl.BlockSpec((None, tc, hv), lambda b, t: (b, t, 1)),
                  pl.BlockSpec((None, tc, hv), lambda b, t: (b, t, 2)),
                  pl.BlockSpec((None, tc, V7X_LANES), lambda b, t: (b, t, 0)),
                  pl.BlockSpec((V7X_LANES, hk), lambda b, t: (0, 0)),
                  pl.BlockSpec((1, hk), lambda b, t: (0, 0)),
                  pl.BlockSpec((1, C_DV), lambda b, t: (0, 0))],
        out_specs=pl.BlockSpec((None, tc, hv), lambda b, t: (b, t, 0)),
        out_shape=jax.ShapeDtypeStruct((bsz, s, hv), BF16),
        scratch_shapes=[pltpu.VMEM((C_HEADS, C_DV, C_DK), F32)],
        compiler_params=_params(("parallel", "arbitrary"), vmem),
        name="gla",
    )(z3, z3, z3, z3, ga3, wa, b_a.reshape(1, -1), head_g.reshape(1, C_DV))


def _ffn_kernel(x_ref, a_ref, b_ref, wa_ref, wb_ref, g_ref, wg_ref, wu_ref, cw_ref, cb_ref, wd_ref, o_ref,
                h_ref, act_ref, gbuf_ref, halo_ref, *, tiles_per_seq, tf):
    tm = x_ref.shape[0]
    halo_rows = V7X_SUBLANES
    for r in range(tm // ROW_SUB):
        rows = slice(r * ROW_SUB, (r + 1) * ROW_SUB)
        x1 = x_ref[rows, :] + jnp.dot(a_ref[rows, :].astype(BF16), wa_ref[...], preferred_element_type=F32)
        x1 = x1 + jnp.dot(b_ref[rows, :].astype(BF16), wb_ref[...], preferred_element_type=F32)
        o_ref[rows, :] = x1
        h_ref[rows, :] = _rms_rows(x1, g_ref[...]).astype(BF16)

    @pl.when(pl.program_id(0) % tiles_per_seq == 0)
    def _():
        halo_ref[...] = jnp.zeros_like(halo_ref)

    for j in range(wg_ref.shape[1] // tf):
        cols = slice(j * tf, (j + 1) * tf)
        h = h_ref[...]
        gate = jnp.dot(h, wg_ref[:, cols], preferred_element_type=F32)
        up = jnp.dot(h, wu_ref[:, cols], preferred_element_type=F32)
        gbuf = gbuf_ref.at[j % 2]
        gbuf[0:halo_rows, :] = halo_ref[:, cols]
        gbuf[halo_rows:, :] = gate
        halo_ref[:, cols] = gate[tm - halo_rows:, :]
        conv = cb_ref[:, cols]
        for tap in range(CONV_W - 1):
            conv = conv + gbuf[pl.ds(halo_rows - (CONV_W - 1) + tap, tm), :] * cw_ref[tap:tap + 1, cols]
        conv = conv + gate * cw_ref[CONV_W - 1:CONV_W, cols]
        act_ref[:, cols] = (_silu(conv) * up).astype(BF16)
    o_ref[...] += jnp.dot(act_ref[...], wd_ref[...], preferred_element_type=F32)


def _ffn(x, a, a_blk, b, b_blk, w_out_all, lo, g, w_gate_all, w_up_all, conv_w, conv_b, w_down_all, lf,
         *, seq, tm, tf):
    n, d = x.shape
    dff = w_gate_all.shape[2]
    kh = w_out_all.shape[1] // 2
    vmem = (2 * (2 * tm * d * 4 + 2 * tm * kh * 4 + 2 * kh * d * 2 + 3 * d * dff * 2 + 4 * dff * 4)
            + tm * d * 2 + tm * dff * 2 + 2 * (tm + 8) * tf * 4 + 8 * dff * 4 + 8 * tm * tf * 4)
    return pl.pallas_call(
        functools.partial(_ffn_kernel, tiles_per_seq=seq // tm, tf=tf),
        grid=(n // tm,),
        in_specs=[pl.BlockSpec((tm, d), lambda i: (i, 0)),
                  pl.BlockSpec((tm, kh), lambda i: (i, a_blk)),
                  pl.BlockSpec((tm, kh), lambda i: (i, b_blk)),
                  pl.BlockSpec((None, kh, d), lambda i: (lo, 0, 0)),
                  pl.BlockSpec((None, kh, d), lambda i: (lo, 1, 0)),
                  pl.BlockSpec((1, d), lambda i: (0, 0)),
                  _layer_spec(w_gate_all, lf),
                  _layer_spec(w_up_all, lf),
                  pl.BlockSpec((CONV_W, dff), lambda i: (0, 0)),
                  pl.BlockSpec((1, dff), lambda i: (0, 0)),
                  _layer_spec(w_down_all, lf)],
        out_specs=pl.BlockSpec((tm, d), lambda i: (i, 0)),
        out_shape=jax.ShapeDtypeStruct((n, d), F32),
        scratch_shapes=[pltpu.VMEM((tm, d), BF16), pltpu.VMEM((tm, dff), BF16),
                        pltpu.VMEM((2, V7X_SUBLANES + tm, tf), F32),
                        pltpu.VMEM((V7X_SUBLANES, dff), F32)],
        compiler_params=_params(("arbitrary",), vmem),
        name="conv_ffn",
    )(x, a, b, w_out_all, w_out_all, g.reshape(1, d), w_gate_all, w_up_all, conv_w, conv_b.reshape(1, dff),
      w_down_all)


def kernel(x, norm_mix_g, norm_ffn_g, ev_w_in, ev_a_ln_g, ev_a_ln_b, ev_a_ws, ev_a_bs, ev_q_g, ev_k_g,
           ev_w_out, od_w_in, od_w_a2, od_b_a, od_head_g, od_w_out, ffn_w_gate, ffn_w_up, ffn_conv_w,
           ffn_conv_b, ffn_w_down):
    bsz, seq, d = x.shape
    n = bsz * seq
    depth = norm_mix_g.shape[0]
    main = 2 * C_HEADS * (C_DK + C_DV)
    ev_w_in, ev_w_out, od_w_out = ev_w_in.astype(BF16), ev_w_out.astype(BF16), od_w_out.astype(BF16)
    od_w_all = od_w_in[:, :, :main].astype(BF16)
    od_w_side = jnp.pad(od_w_in[:, :, main:], ((0, 0), (0, 0), (0, V7X_LANES - C_GATE_RANK))).astype(BF16)
    ffn_w_gate, ffn_w_up, ffn_w_down = ffn_w_gate.astype(BF16), ffn_w_up.astype(BF16), ffn_w_down.astype(BF16)
    xf = x.reshape(n, d)
    for layer in range(depth):
        if layer % 2 == 0:
            e = layer // 2
            a_out, qb, kn, vn, qp, kp, vp = _even_in(
                xf, norm_mix_g[layer], ev_w_in, e, ev_a_ln_g[e], ev_a_ln_b[e], ev_a_ws[e], ev_a_bs[e],
                ev_q_g[e], ev_k_g[e], bsz=bsz, tm=1024)
            in_seq = lambda t: t.reshape(bsz, seq, -1)
            b_out = _dilated(in_seq(qb), in_seq(kn), in_seq(vn), qp, kp, vp).reshape(n, -1)
            mix = (a_out, 0, b_out, 0, ev_w_out, e)
        else:
            o = layer // 2
            z, ga = _norm_matmul(xf, norm_mix_g[layer], od_w_all, o, main, od_w_side[o], tm=512)
            mixed = _gla(z.reshape(bsz, seq, -1), ga.reshape(bsz, seq, -1), od_w_a2[o], od_b_a[o],
                         od_head_g[o], tc=512).reshape(n, -1)
            mix = (mixed, 0, mixed, 1, od_w_out, o)
        xf = _ffn(xf, *mix, norm_ffn_g[layer], ffn_w_gate, ffn_w_up, ffn_conv_w[layer], ffn_conv_b[layer],
                  ffn_w_down, layer, seq=seq, tm=512, tf=256)
    return xf.reshape(bsz, seq, d)
```

```python
import functools

import jax
import jax.numpy as jnp
from jax import lax
from jax.experimental import pallas as pl
from jax.experimental.pallas import tpu as pltpu

A_GROUPS = 8
A_GROUP_DIM = 64
A_CHUNK = 128
B_HEAD_DIM = 64
B_DILATIONS = (1, 4, 16)
B_BLOCK = 128
C_HEADS = 4
C_DK = 128
C_DV = 256
C_GATE_RANK = 16
C_TAU = 16.0
C_CHUNK = 64
CONV_W = 3
EPS = 1e-6
NEG = -1e30

V7X_LANES = 128
V7X_SUBLANES = 8
V7X_VMEM_BUDGET = 56 * 1024 * 1024

F32 = jnp.float32
BF16 = jnp.bfloat16
NT_DIMS = (((1,), (1,)), ((), ()))
TN_DIMS = (((0,), (0,)), ((), ()))

ROW_SUB = 2 * A_CHUNK
B_PLANES = max(B_DILATIONS)
STAGE_PITCH = 24
B_SCORE_SCALE = B_HEAD_DIM ** -0.5 * 1.4426950408889634


def _params(semantics, vmem_bytes):
    return pltpu.CompilerParams(
        dimension_semantics=semantics,
        vmem_limit_bytes=min(int(vmem_bytes * 1.25) + (4 << 20), V7X_VMEM_BUDGET))


def _layer_spec(w_all, li):
    return pl.BlockSpec((None,) + w_all.shape[1:], lambda i: (li, 0, 0))


def _rms_rows(x, g):
    return x * lax.rsqrt(jnp.mean(x * x, axis=-1, keepdims=True) + EPS) * g


def _gelu(x):
    return 0.5 * x * (1.0 + lax.erf(x * (0.5 ** 0.5)))


def _silu(x):
    return x * jax.nn.sigmoid(x)


def _head_pair_rms(x, g, head0):
    x2 = x * x
    s0 = jnp.sum(jnp.where(head0, x2, 0.0), axis=-1, keepdims=True)
    s1 = jnp.sum(jnp.where(head0, 0.0, x2), axis=-1, keepdims=True)
    ms = jnp.where(head0, s0, s1) * (1.0 / B_HEAD_DIM)
    return x * lax.rsqrt(ms + EPS) * g


def _norm_matmul_kernel(x_ref, g_ref, w_ref, ws_ref, o_ref, os_ref):
    for r in range(x_ref.shape[0] // ROW_SUB):
        rows = slice(r * ROW_SUB, (r + 1) * ROW_SUB)
        h = _rms_rows(x_ref[rows, :], g_ref[...]).astype(BF16)
        o_ref[rows, :] = jnp.dot(h, w_ref[:, :o_ref.shape[1]], preferred_element_type=F32)
        os_ref[rows, :] = jnp.dot(h, ws_ref[...], preferred_element_type=F32)


def _norm_matmul(x, g, w_all, li, f, w_side, *, tm):
    n, d = x.shape
    fs = w_side.shape[1]
    vmem = 2 * (tm * d * 4 + d * (w_all.shape[2] + fs) * 2 + tm * (f + fs) * 4) + 2 * ROW_SUB * f * 4
    return pl.pallas_call(
        _norm_matmul_kernel,
        grid=(n // tm,),
        in_specs=[pl.BlockSpec((tm, d), lambda i: (i, 0)),
                  pl.BlockSpec((1, d), lambda i: (0, 0)),
                  _layer_spec(w_all, li),
                  pl.BlockSpec((d, fs), lambda i: (0, 0))],
        out_specs=(pl.BlockSpec((tm, f), lambda i: (i, 0)), pl.BlockSpec((tm, fs), lambda i: (i, 0))),
        out_shape=(jax.ShapeDtypeStruct((n, f), F32), jax.ShapeDtypeStruct((n, fs), F32)),
        compiler_params=_params(("parallel",), vmem),
        name="norm_matmul",
    )(x, g.reshape(1, d), w_all, w_side)


def _even_in_kernel(x_ref, g_ref, w_ref, lng_ref, lnb_ref, ws_ref, bs_ref, qg_ref, kg_ref,
                    a_ref, qb_ref, kn_ref, vn_ref, qp_ref, kp_ref, vp_ref, h_ref, z_ref, stage_ref):
    tm = x_ref.shape[0]
    aw = a_ref.shape[1]
    bw = kn_ref.shape[1]
    tiles = bw // V7X_LANES
    groups = ROW_SUB // B_PLANES
    per_block = B_BLOCK // B_PLANES
    row = lax.broadcasted_iota(jnp.int32, (A_CHUNK, 2 * A_CHUNK), 0)
    col = lax.broadcasted_iota(jnp.int32, (A_CHUNK, 2 * A_CHUNK), 1)
    causal = (col % A_CHUNK) <= row
    lane2 = lax.broadcasted_iota(jnp.int32, (1, 2 * V7X_LANES), 1)
    first_group = (lane2 % V7X_LANES) < A_GROUP_DIM
    head0 = lax.broadcasted_iota(jnp.int32, (1, V7X_LANES), 1) < B_HEAD_DIM
    w_pairs = [jnp.where(causal, ws_ref[t], 0.0).astype(BF16) for t in range(ws_ref.shape[0])]
    for r in range(tm // ROW_SUB):
        base = r * ROW_SUB
        rows = slice(base, base + ROW_SUB)
        h, z, stage = h_ref.at[r % 2], z_ref.at[r % 2], stage_ref.at[r % 2]
        h[...] = _rms_rows(x_ref[rows, :], g_ref[...]).astype(BF16)
        z[...] = jnp.dot(h[...], w_ref[...], preferred_element_type=F32)

        v = _gelu(z[:, aw:2 * aw])
        vc = v - jnp.mean(v, axis=-1, keepdims=True)
        vn = vc * lax.rsqrt(jnp.mean(vc * vc, axis=-1, keepdims=True) + EPS) * lng_ref[...] + lnb_ref[...]
        for t in range(aw // V7X_LANES):
            lanes = slice(t * V7X_LANES, (t + 1) * V7X_LANES)
            cc = jnp.concatenate([vn[:A_CHUNK, lanes], vn[A_CHUNK:, lanes]], axis=1)
            rhs = jnp.concatenate([jnp.where(first_group, cc, 0.0),
                                   jnp.where(first_group, 0.0, cc)], axis=0).astype(BF16)
            mixed = jnp.dot(w_pairs[t], rhs, preferred_element_type=F32)
            bias = bs_ref[:, lanes]
            for c in range(2):
                chunk = slice(c * A_CHUNK, (c + 1) * A_CHUNK)
                u = _gelu(z[chunk, lanes])
                a_ref[base + c * A_CHUNK:base + (c + 1) * A_CHUNK, lanes] = (
                    u * (mixed[:, c * V7X_LANES:(c + 1) * V7X_LANES] + bias)).astype(BF16)

        for t in range(tiles):
            lanes = slice(t * V7X_LANES, (t + 1) * V7X_LANES)
            q = _head_pair_rms(z[:, 2 * aw + t * V7X_LANES:2 * aw + (t + 1) * V7X_LANES], qg_ref[...], head0)
            q = q * B_SCORE_SCALE
            k = _head_pair_rms(z[:, 2 * aw + bw + t * V7X_LANES:2 * aw + bw + (t + 1) * V7X_LANES],
                               kg_ref[...], head0)
            vb = z[:, 2 * aw + 2 * bw + t * V7X_LANES:2 * aw + 2 * bw + (t + 1) * V7X_LANES]
            kn_ref[rows, lanes] = k.astype(BF16)
            vn_ref[rows, lanes] = vb.astype(BF16)
            for i, val in enumerate((q, k, vb)):
                for grp in range(groups):
                    stage[i * tiles + t, grp * STAGE_PITCH:grp * STAGE_PITCH + B_PLANES, :] = (
                        val[grp * B_PLANES:(grp + 1) * B_PLANES])
            for p in range(B_PLANES):
                sel = pl.ds(p, groups, stride=STAGE_PITCH)
                for i, planes in enumerate((qp_ref, kp_ref, vp_ref)):
                    planes[p, r * groups:(r + 1) * groups, lanes] = stage[i * tiles + t, sel, :].astype(BF16)
            for blk in range(ROW_SUB // B_BLOCK):
                first = blk * per_block * STAGE_PITCH
                slabs = [stage[t, pl.ds(first + p, per_block, stride=STAGE_PITCH), :] for p in range(B_PLANES)]
                qb_ref[base + blk * B_BLOCK:base + (blk + 1) * B_BLOCK, lanes] = (
                    jnp.concatenate(slabs, axis=0).astype(BF16))


def _even_in(x, g, w_all, li, ln_g, ln_b, w_s, b_s, q_g, k_g, *, bsz, tm):
    n, d = x.shape
    seq = n // bsz
    aw = A_GROUPS * A_GROUP_DIM
    f = w_all.shape[2]
    bw = (f - 2 * aw) // 3
    pairs = A_GROUPS // 2
    tiles = seq // tm
    per_plane = tm // B_PLANES
    ws_pairs = w_s.reshape(pairs, 2, A_CHUNK, A_CHUNK).transpose(0, 2, 1, 3).reshape(pairs, A_CHUNK, 2 * A_CHUNK)
    bias = jnp.repeat(b_s.T, A_GROUP_DIM, axis=1)
    qg2 = jnp.tile(q_g, 2).reshape(1, V7X_LANES)
    kg2 = jnp.tile(k_g, 2).reshape(1, V7X_LANES)
    nat_spec = pl.BlockSpec((tm, bw), lambda i: (i, 0))
    plane_spec = pl.BlockSpec((None, B_PLANES, per_plane, bw), lambda i: (i // tiles, 0, i % tiles, 0))
    nat_shape = jax.ShapeDtypeStruct((n, bw), BF16)
    plane_shape = jax.ShapeDtypeStruct((bsz, B_PLANES, seq // B_PLANES, bw), BF16)
    stage_rows = ROW_SUB // B_PLANES * STAGE_PITCH
    vmem = (2 * (tm * d * 4 + d * f * 2 + tm * aw * 2 + 6 * tm * bw * 2)
            + 2 * (ROW_SUB * d * 2 + ROW_SUB * f * 4 + 3 * stage_rows * bw * 4)
            + 2 * ROW_SUB * (2 * aw + 3 * bw) * 4)
    return pl.pallas_call(
        _even_in_kernel,
        grid=(n // tm,),
        in_specs=[pl.BlockSpec((tm, d), lambda i: (i, 0)),
                  pl.BlockSpec((1, d), lambda i: (0, 0)),
                  _layer_spec(w_all, li),
                  pl.BlockSpec((1, aw), lambda i: (0, 0)),
                  pl.BlockSpec((1, aw), lambda i: (0, 0)),
                  pl.BlockSpec((pairs, A_CHUNK, 2 * A_CHUNK), lambda i: (0, 0, 0)),
                  pl.BlockSpec((A_CHUNK, aw), lambda i: (0, 0)),
                  pl.BlockSpec((1, V7X_LANES), lambda i: (0, 0)),
                  pl.BlockSpec((1, V7X_LANES), lambda i: (0, 0))],
        out_specs=(pl.BlockSpec((tm, aw), lambda i: (i, 0)), nat_spec, nat_spec, nat_spec,
                   plane_spec, plane_spec, plane_spec),
        out_shape=(jax.ShapeDtypeStruct((n, aw), BF16), nat_shape, nat_shape, nat_shape,
                   plane_shape, plane_shape, plane_shape),
        scratch_shapes=[pltpu.VMEM((2, ROW_SUB, d), BF16), pltpu.VMEM((2, ROW_SUB, f), F32),
                        pltpu.VMEM((2, 3 * (bw // V7X_LANES), stage_rows, V7X_LANES), F32)],
        compiler_params=_params(("parallel",), vmem),
        name="even_in",
    )(x, g.reshape(1, d), w_all, ln_g.reshape(1, aw), ln_b.reshape(1, aw), ws_pairs, bias, qg2, kg2)


def _dilated_kernel(qb_ref, kn_ref, vn_ref, qp_ref, kp_ref, vp_ref, o_ref, ob_ref, mb_ref, db_ref, mask_ref):
    m_rows = qp_ref.shape[1]
    blk2 = 2 * B_BLOCK
    head0 = lax.broadcasted_iota(jnp.int32, (1, V7X_LANES), 1) < B_HEAD_DIM

    rowi = lax.broadcasted_iota(jnp.int32, (blk2, blk2), 0) % B_BLOCK
    coli = lax.broadcasted_iota(jnp.int32, (blk2, blk2), 1)
    is_cur = coli >= B_BLOCK
    colj = coli % B_BLOCK
    for bi, d in enumerate(B_DILATIONS):
        planes = B_PLANES // d
        mb = B_BLOCK // planes
        i_pos = (rowi % mb) * planes + rowi // mb
        j_pos = colj if d == 1 else (colj % mb) * planes + colj // mb
        band = jnp.where(jnp.where(is_cur, i_pos - j_pos, j_pos - i_pos) >= 0, 0.0, NEG)
        mask_ref[2 * bi] = jnp.where(is_cur, band, NEG)
        mask_ref[2 * bi + 1] = band

    ones = jnp.ones((blk2, V7X_LANES), BF16)

    def attend(qb, kcat, vcat, bias):
        zero = jnp.zeros_like(qb)
        q2 = jnp.concatenate([jnp.where(head0, qb, zero), jnp.where(head0, zero, qb)], axis=0)
        s = lax.dot_general(q2, kcat, NT_DIMS, preferred_element_type=F32) + bias
        m = jnp.max(s, axis=-1, keepdims=True)
        p = jnp.exp2(s - m).astype(BF16)
        pv = jnp.dot(p, jnp.concatenate([vcat, ones], axis=1), preferred_element_type=F32)
        top, bot = pv[:B_BLOCK], pv[B_BLOCK:]
        return (jnp.where(head0, top[:, :V7X_LANES], bot[:, :V7X_LANES]),
                jnp.where(head0, m[:B_BLOCK], m[B_BLOCK:]),
                jnp.where(head0, top[:, V7X_LANES:], bot[:, V7X_LANES:]))

    def token_block(n, carry):
        per_block = B_BLOCK // B_PLANES
        off = pl.multiple_of(n * B_BLOCK, B_BLOCK)
        off_prev = pl.multiple_of(jnp.maximum(n - 1, 0) * B_BLOCK, B_BLOCK)
        off_plane = pl.multiple_of(n * per_block, per_block)
        kcat = jnp.concatenate([kn_ref[pl.ds(off_prev, B_BLOCK), :], kn_ref[pl.ds(off, B_BLOCK), :]], axis=0)
        vcat = jnp.concatenate([vn_ref[pl.ds(off_prev, B_BLOCK), :], vn_ref[pl.ds(off, B_BLOCK), :]], axis=0)
        o, m, den = attend(qb_ref[pl.ds(off, B_BLOCK), :], kcat, vcat, mask_ref[jnp.minimum(n, 1)])
        for p in range(B_PLANES):
            for ref, val in ((ob_ref, o), (mb_ref, m), (db_ref, den)):
                ref[0, p, pl.ds(off_plane, per_block), :] = val[p * per_block:(p + 1) * per_block]
        return carry

    lax.fori_loop(0, kn_ref.shape[0] // B_BLOCK, token_block, 0, unroll=32)

    for bi, d in enumerate(B_DILATIONS):
        if d == 1:
            continue
        planes = B_PLANES // d
        mb = B_BLOCK // planes
        nb = m_rows // mb

        def plane_block(blk, carry, bi=bi, d=d, planes=planes, mb=mb, nb=nb):
            r = blk // nb
            n = blk % nb
            off = pl.multiple_of(n * mb, mb)
            off_prev = pl.multiple_of(jnp.maximum(n - 1, 0) * mb, mb)

            def gather(ref, offs):
                return jnp.concatenate([ref[r + d * a, pl.ds(o, mb), :] for o in offs for a in range(planes)],
                                       axis=0)

            o, m, den = attend(gather(qp_ref, (off,)), gather(kp_ref, (off_prev, off)),
                               gather(vp_ref, (off_prev, off)), mask_ref[2 * bi + jnp.minimum(n, 1)])
            for a in range(planes):
                for ref, val in ((ob_ref, o), (mb_ref, m), (db_ref, den)):
                    ref[bi, r + d * a, pl.ds(off, mb), :] = val[a * mb:(a + 1) * mb]
            return carry

        lax.fori_loop(0, d * nb, plane_block, 0, unroll=32)

    branches = range(len(B_DILATIONS))
    for r in range(B_PLANES):
        mx = functools.reduce(jnp.maximum, [mb_ref[bi, r] for bi in branches])
        es = [jnp.exp2(mb_ref[bi, r] - mx) for bi in branches]
        num = functools.reduce(lambda a, b: a + b, [es[bi] * ob_ref[bi, r] for bi in branches])
        den = functools.reduce(lambda a, b: a + b, [es[bi] * db_ref[bi, r] for bi in branches])
        o_ref[pl.ds(r, m_rows, stride=B_PLANES), :] = num / den


def _dilated(qb, kn, vn, qp, kp, vp):
    bsz, s, bw = kn.shape
    pairs = bw // V7X_LANES
    m_rows = s // B_PLANES
    tile_f32 = s * V7X_LANES * 4
    vmem = 2 * (6 * tile_f32 // 2 + tile_f32) + 9 * tile_f32 + 6 * 4 * B_BLOCK * B_BLOCK * 4 + 8 * tile_f32 // 16
    nat_spec = pl.BlockSpec((None, s, V7X_LANES), lambda b, p: (b, 0, p))
    plane_spec = pl.BlockSpec((None, B_PLANES, m_rows, V7X_LANES), lambda b, p: (b, 0, 0, p))
    return pl.pallas_call(
        _dilated_kernel,
        grid=(bsz, pairs),
        in_specs=[nat_spec, nat_spec, nat_spec, plane_spec, plane_spec, plane_spec],
        out_specs=pl.BlockSpec((None, s, V7X_LANES), lambda b, p: (b, 0, p)),
        out_shape=jax.ShapeDtypeStruct((bsz, s, bw), F32),
        scratch_shapes=[pltpu.VMEM((len(B_DILATIONS), B_PLANES, m_rows, V7X_LANES), F32)] * 3
                       + [pltpu.VMEM((2 * len(B_DILATIONS), 2 * B_BLOCK, 2 * B_BLOCK), F32)],
        compiler_params=_params(("parallel", "parallel"), vmem),
        name="dilated_attention",
    )(qb, kn, vn, qp, kp, vp)


def _gla_kernel(q_ref, k_ref, v_ref, r_ref, ga_ref, wa_ref, ba_ref, hg_ref, o_ref, st_ref):
    tc = q_ref.shape[0]
    chunks = tc // C_CHUNK

    @pl.when(pl.program_id(1) == 0)
    def _():
        st_ref[...] = jnp.zeros_like(st_ref)

    gate = jnp.dot(ga_ref[...].astype(BF16), wa_ref[...], preferred_element_type=F32) + ba_ref[...]
    log_a = jax.nn.log_sigmoid(gate) / C_TAU
    ci = lax.broadcasted_iota(jnp.int32, (C_CHUNK, C_CHUNK), 0)
    cj = lax.broadcasted_iota(jnp.int32, (C_CHUNK, C_CHUNK), 1)
    causal = ci >= cj
    tri = jnp.where(causal, 1.0, 0.0).astype(BF16)
    rows = [slice(c * C_CHUNK, (c + 1) * C_CHUNK) for c in range(chunks)]
    for hd in range(C_HEADS):
        kl = slice(hd * C_DK, (hd + 1) * C_DK)
        vl = slice(hd * C_DV, (hd + 1) * C_DV)
        la = log_a[:, kl]
        hi = la.astype(BF16)
        rest = la - hi.astype(F32)
        mid = rest.astype(BF16)
        lo = (rest - mid.astype(F32)).astype(BF16)
        pieces = jnp.concatenate([hi, mid, lo], axis=1)
        sums = [jnp.dot(tri, pieces[sl], preferred_element_type=F32) for sl in rows]
        b = jnp.concatenate([s3[:, :C_DK] + s3[:, C_DK:2 * C_DK] + s3[:, 2 * C_DK:] for s3 in sums], axis=0)
        b3 = b.reshape(chunks, C_CHUNK, C_DK)
        b_last = b3[:, C_CHUNK - 1:C_CHUNK, :]
        k = k_ref[:, kl]
        q_t = ((q_ref[:, kl] * (C_DK ** -0.5)) * jnp.exp(b)).astype(BF16)
        k_t = (k * jnp.exp(-b)).astype(BF16)
        k_s = (k.reshape(chunks, C_CHUNK, C_DK) * jnp.exp(b_last - b3)).reshape(tc, C_DK).astype(BF16)
        decay = jnp.exp(b_last)
        o_intra, kv_t = [], []
        for sl in rows:
            v_c = v_ref[sl, vl].astype(BF16)
            attn = lax.dot_general(q_t[sl], k_t[sl], NT_DIMS, preferred_element_type=F32)
            attn = jnp.where(causal, attn, 0.0).astype(BF16)
            o_intra.append(jnp.dot(attn, v_c, preferred_element_type=F32))
            kv_t.append(lax.dot_general(v_c, k_s[sl], TN_DIMS, preferred_element_type=F32))
        st = st_ref[hd]
        entering = []
        for c in range(chunks):
            entering.append(st.astype(BF16))
            st = st * decay[c] + kv_t[c]
        st_ref[hd] = st
        for c, sl in enumerate(rows):
            o = o_intra[c] + lax.dot_general(q_t[sl], entering[c], NT_DIMS, preferred_element_type=F32)
            o_ref[sl, vl] = (_rms_rows(o, hg_ref[...]) * _silu(r_ref[sl, vl])).astype(BF16)


def _gla(z3, ga3, w_a2, b_a, head_g, *, tc):
    bsz, s, _ = z3.shape
    hk, hv = C_HEADS * C_DK, C_HEADS * C_DV
    wa = jnp.pad(w_a2, ((0, V7X_LANES - C_GATE_RANK), (0, 0))).astype(BF16)
    vmem = (2 * (2 * tc * hk * 4 + 3 * tc * hv * 4 + tc * V7X_LANES * 4 + V7X_LANES * hk * 2)
            + hv * C_DK * 4 + 12 * tc * C_DK * 4 + 2 * tc * hk * 4)
    return pl.pallas_call(
        _gla_kernel,
        grid=(bsz, s // tc),
        in_specs=[pl.BlockSpec((None, tc, hk), lambda b, t: (b, t, 0)),
                  pl.BlockSpec((None, tc, hk), lambda b, t: (b, t, 1)),
                  pl.BlockSpec((None, tc, hv), lambda b, t: (b, t, 1)),
                  pl.BlockSpec((None, tc, hv), lambda b, t: (b, t, 2)),
                  pl.BlockSpec((None, tc, V7X_LANES), lambda b, t: (b, t, 0)),
                  pl.BlockSpec((V7X_LANES, hk), lambda b, t: (0, 0)),
                  pl.BlockSpec((1, hk), lambda b, t: (0, 0)),
                  pl.BlockSpec((1, C_DV), lambda b, t: (0, 0))],
        out_specs=pl.BlockSpec((None, tc, hv), lambda b, t: (b, t, 0)),
        out_shape=jax.ShapeDtypeStruct((bsz, s, hv), BF16),
        scratch_shapes=[pltpu.VMEM((C_HEADS, C_DV, C_DK), F32)],
        compiler_params=_params(("parallel", "arbitrary"), vmem),
        name="gla",
    )(z3, z3, z3, z3, ga3, wa, b_a.reshape(1, -1), head_g.reshape(1, C_DV))


def _ffn_kernel(x_ref, a_ref, b_ref, wa_ref, wb_ref, g_ref, wg_ref, wu_ref, cw_ref, cb_ref, wd_ref, o_ref,
                h_ref, act_ref, gbuf_ref, halo_ref, *, tiles_per_seq, tf):
    tm = x_ref.shape[0]
    halo_rows = V7X_SUBLANES
    for r in range(tm // ROW_SUB):
        rows = slice(r * ROW_SUB, (r + 1) * ROW_SUB)
        x1 = x_ref[rows, :] + jnp.dot(a_ref[rows, :].astype(BF16), wa_ref[...], preferred_element_type=F32)
        x1 = x1 + jnp.dot(b_ref[rows, :].astype(BF16), wb_ref[...], preferred_element_type=F32)
        o_ref[rows, :] = x1
        h_ref[rows, :] = _rms_rows(x1, g_ref[...]).astype(BF16)

    @pl.when(pl.program_id(0) % tiles_per_seq == 0)
    def _():
        halo_ref[...] = jnp.zeros_like(halo_ref)

    for j in range(wg_ref.shape[1] // tf):
        cols = slice(j * tf, (j + 1) * tf)
        h = h_ref[...]
        gate = jnp.dot(h, wg_ref[:, cols], preferred_element_type=F32)
        up = jnp.dot(h, wu_ref[:, cols], preferred_element_type=F32)
        gbuf = gbuf_ref.at[j % 2]
        gbuf[0:halo_rows, :] = halo_ref[:, cols]
        gbuf[halo_rows:, :] = gate
        halo_ref[:, cols] = gate[tm - halo_rows:, :]
        conv = cb_ref[:, cols]
        for tap in range(CONV_W - 1):
            conv = conv + gbuf[pl.ds(halo_rows - (CONV_W - 1) + tap, tm), :] * cw_ref[tap:tap + 1, cols]
        conv = conv + gate * cw_ref[CONV_W - 1:CONV_W, cols]
        act_ref[:, cols] = (_silu(conv) * up).astype(BF16)
    o_ref[...] += jnp.dot(act_ref[...], wd_ref[...], preferred_element_type=F32)


def _ffn(x, a, a_blk, b, b_blk, w_out_all, lo, g, w_gate_all, w_up_all, conv_w, conv_b, w_down_all, lf,
         *, seq, tm, tf):
    n, d = x.shape
    dff = w_gate_all.shape[2]
    kh = w_out_all.shape[1] // 2
    vmem = (2 * (2 * tm * d * 4 + 2 * tm * kh * 4 + 2 * kh * d * 2 + 3 * d * dff * 2 + 4 * dff * 4)
            + tm * d * 2 + tm * dff * 2 + 2 * (tm + 8) * tf * 4 + 8 * dff * 4 + 8 * tm * tf * 4)
    return pl.pallas_call(
        functools.partial(_ffn_kernel, tiles_per_seq=seq // tm, tf=tf),
        grid=(n // tm,),
        in_specs=[pl.BlockSpec((tm, d), lambda i: (i, 0)),
                  pl.BlockSpec((tm, kh), lambda i: (i, a_blk)),
                  pl.BlockSpec((tm, kh), lambda i: (i, b_blk)),
                  pl.BlockSpec((None, kh, d), lambda i: (lo, 0, 0)),
                  pl.BlockSpec((None, kh, d), lambda i: (lo, 1, 0)),
                  pl.BlockSpec((1, d), lambda i: (0, 0)),
                  _layer_spec(w_gate_all, lf),
                  _layer_spec(w_up_all, lf),
                  pl.BlockSpec((CONV_W, dff), lambda i: (0, 0)),
                  pl.BlockSpec((1, dff), lambda i: (0, 0)),
                  _layer_spec(w_down_all, lf)],
        out_specs=pl.BlockSpec((tm, d), lambda i: (i, 0)),
        out_shape=jax.ShapeDtypeStruct((n, d), F32),
        scratch_shapes=[pltpu.VMEM((tm, d), BF16), pltpu.VMEM((tm, dff), BF16),
                        pltpu.VMEM((2, V7X_SUBLANES + tm, tf), F32),
                        pltpu.VMEM((V7X_SUBLANES, dff), F32)],
        compiler_params=_params(("arbitrary",), vmem),
        name="conv_ffn",
    )(x, a, b, w_out_all, w_out_all, g.reshape(1, d), w_gate_all, w_up_all, conv_w, conv_b.reshape(1, dff),
      w_down_all)


def kernel(x, norm_mix_g, norm_ffn_g, ev_w_in, ev_a_ln_g, ev_a_ln_b, ev_a_ws, ev_a_bs, ev_q_g, ev_k_g,
           ev_w_out, od_w_in, od_w_a2, od_b_a, od_head_g, od_w_out, ffn_w_gate, ffn_w_up, ffn_conv_w,
           ffn_conv_b, ffn_w_down):
    bsz, seq, d = x.shape
    n = bsz * seq
    depth = norm_mix_g.shape[0]
    main = 2 * C_HEADS * (C_DK + C_DV)
    ev_w_in, ev_w_out, od_w_out = ev_w_in.astype(BF16), ev_w_out.astype(BF16), od_w_out.astype(BF16)
    od_w_all = od_w_in.astype(BF16)
    od_w_side = jnp.pad(od_w_in[:, :, main:], ((0, 0), (0, 0), (0, V7X_LANES - C_GATE_RANK))).astype(BF16)
    ffn_w_gate, ffn_w_up, ffn_w_down = ffn_w_gate.astype(BF16), ffn_w_up.astype(BF16), ffn_w_down.astype(BF16)
    xf = x.reshape(n, d)
    for layer in range(depth):
        if layer % 2 == 0:
            e = layer // 2
            a_out, qb, kn, vn, qp, kp, vp = _even_in(
                xf, norm_mix_g[layer], ev_w_in, e, ev_a_ln_g[e], ev_a_ln_b[e], ev_a_ws[e], ev_a_bs[e],
                ev_q_g[e], ev_k_g[e], bsz=bsz, tm=1024)
            in_seq = lambda t: t.reshape(bsz, seq, -1)
            b_out = _dilated(in_seq(qb), in_seq(kn), in_seq(vn), qp, kp, vp).reshape(n, -1)
            mix = (a_out, 0, b_out, 0, ev_w_out, e)
        else:
            o = layer // 2
            z, ga = _norm_matmul(xf, norm_mix_g[layer], od_w_all, o, main, od_w_side[o], tm=512)
            mixed = _gla(z.reshape(bsz, seq, -1), ga.reshape(bsz, seq, -1), od_w_a2[o], od_b_a[o],
                         od_head_g[o], tc=512).reshape(n, -1)
            mix = (mixed, 0, mixed, 1, od_w_out, o)
        xf = _ffn(xf, *mix, norm_ffn_g[layer], ffn_w_gate, ffn_w_up, ffn_conv_w[layer], ffn_conv_b[layer],
                  ffn_w_down, layer, seq=seq, tm=512, tf=256)
    return xf.reshape(bsz, seq, d)
```

```python
import functools

import jax
import jax.numpy as jnp
from jax import lax
from jax.experimental import pallas as pl
from jax.experimental.pallas import tpu as pltpu

A_GROUPS = 8
A_GROUP_DIM = 64
A_CHUNK = 128
B_HEAD_DIM = 64
B_DILATIONS = (1, 4, 16)
B_BLOCK = 128
C_HEADS = 4
C_DK = 128
C_DV = 256
C_GATE_RANK = 16
C_TAU = 16.0
C_CHUNK = 64
CONV_W = 3
EPS = 1e-6
NEG = -1e30

V7X_LANES = 128
V7X_SUBLANES = 8
V7X_VMEM_BUDGET = 56 * 1024 * 1024

F32 = jnp.float32
BF16 = jnp.bfloat16
NT_DIMS = (((1,), (1,)), ((), ()))
TN_DIMS = (((0,), (0,)), ((), ()))

ROW_SUB = 2 * A_CHUNK
B_PLANES = max(B_DILATIONS)
STAGE_PITCH = 24
B_SCORE_SCALE = B_HEAD_DIM ** -0.5 * 1.4426950408889634


def _params(semantics, vmem_bytes):
    return pltpu.CompilerParams(
        dimension_semantics=semantics,
        vmem_limit_bytes=min(int(vmem_bytes * 1.25) + (4 << 20), V7X_VMEM_BUDGET))


def _layer_spec(w_all, li):
    return pl.BlockSpec((None,) + w_all.shape[1:], lambda i: (li, 0, 0))


def _rms_rows(x, g):
    return x * lax.rsqrt(jnp.mean(x * x, axis=-1, keepdims=True) + EPS) * g


def _gelu(x):
    return 0.5 * x * (1.0 + lax.erf(x * (0.5 ** 0.5)))


def _silu(x):
    return x * jax.nn.sigmoid(x)


def _head_pair_rms(x, g, head0):
    x2 = x * x
    s0 = jnp.sum(jnp.where(head0, x2, 0.0), axis=-1, keepdims=True)
    s1 = jnp.sum(jnp.where(head0, 0.0, x2), axis=-1, keepdims=True)
    ms = jnp.where(head0, s0, s1) * (1.0 / B_HEAD_DIM)
    return x * lax.rsqrt(ms + EPS) * g


def _norm_matmul_kernel(x_ref, g_ref, w_ref, ws_ref, *rest):
    n_cast = (len(rest) - 2) // 2
    o_ref, os_ref = rest[n_cast:n_cast + 2]
    for r in range(x_ref.shape[0] // ROW_SUB):
        rows = slice(r * ROW_SUB, (r + 1) * ROW_SUB)
        h = _rms_rows(x_ref[rows, :], g_ref[...]).astype(BF16)
        o_ref[rows, :] = jnp.dot(h, w_ref[:, :o_ref.shape[1]], preferred_element_type=F32)
        os_ref[rows, :] = jnp.dot(h, ws_ref[...], preferred_element_type=F32)
    for src, dst in zip(rest[:n_cast], rest[n_cast + 2:]):
        dst[...] = src[...].astype(BF16)


def _norm_matmul(x, g, w_all, li, f, w_side, cast=(), *, tm):
    n, d = x.shape
    fs = w_side.shape[1]
    steps = n // tm
    vmem = 2 * (tm * d * 4 + d * (w_all.shape[2] + fs) * 2 + tm * (f + fs) * 4) + 2 * ROW_SUB * f * 4
    cast_in, cast_out, cast_shape = [], [], []
    for arr, layer in cast:
        slab = (None, arr.shape[1] // steps, arr.shape[2])
        cast_in.append(pl.BlockSpec(slab, lambda i, layer=layer: (layer, i, 0)))
        cast_out.append(pl.BlockSpec(slab, lambda i: (0, i, 0)))
        cast_shape.append(jax.ShapeDtypeStruct((1,) + arr.shape[1:], BF16))
        vmem += 2 * slab[1] * slab[2] * 6
    outs = pl.pallas_call(
        _norm_matmul_kernel,
        grid=(steps,),
        in_specs=[pl.BlockSpec((tm, d), lambda i: (i, 0)),
                  pl.BlockSpec((1, d), lambda i: (0, 0)),
                  _layer_spec(w_all, li),
                  pl.BlockSpec((d, fs), lambda i: (0, 0))] + cast_in,
        out_specs=[pl.BlockSpec((tm, f), lambda i: (i, 0)), pl.BlockSpec((tm, fs), lambda i: (i, 0))] + cast_out,
        out_shape=[jax.ShapeDtypeStruct((n, f), F32), jax.ShapeDtypeStruct((n, fs), F32)] + cast_shape,
        compiler_params=_params(("parallel",), vmem),
        name="norm_matmul",
    )(x, g.reshape(1, d), w_all, w_side, *[arr for arr, _ in cast])
    return outs[0], outs[1], outs[2:]


def _even_in_kernel(x_ref, g_ref, w_ref, lng_ref, lnb_ref, ws_ref, bs_ref, qg_ref, kg_ref,
                    a_ref, qb_ref, kn_ref, vn_ref, qp_ref, kp_ref, vp_ref, h_ref, z_ref, stage_ref):
    tm = x_ref.shape[0]
    aw = a_ref.shape[1]
    bw = kn_ref.shape[1]
    tiles = bw // V7X_LANES
    groups = ROW_SUB // B_PLANES
    per_block = B_BLOCK // B_PLANES
    row = lax.broadcasted_iota(jnp.int32, (A_CHUNK, 2 * A_CHUNK), 0)
    col = lax.broadcasted_iota(jnp.int32, (A_CHUNK, 2 * A_CHUNK), 1)
    causal = (col % A_CHUNK) <= row
    lane2 = lax.broadcasted_iota(jnp.int32, (1, 2 * V7X_LANES), 1)
    first_group = (lane2 % V7X_LANES) < A_GROUP_DIM
    head0 = lax.broadcasted_iota(jnp.int32, (1, V7X_LANES), 1) < B_HEAD_DIM
    w_pairs = [jnp.where(causal, ws_ref[t], 0.0).astype(BF16) for t in range(ws_ref.shape[0])]
    for r in range(tm // ROW_SUB):
        base = r * ROW_SUB
        rows = slice(base, base + ROW_SUB)
        h, z, stage = h_ref.at[r % 2], z_ref.at[r % 2], stage_ref.at[r % 2]
        h[...] = _rms_rows(x_ref[rows, :], g_ref[...]).astype(BF16)
        z[...] = jnp.dot(h[...], w_ref[...], preferred_element_type=F32)

        v = _gelu(z[:, aw:2 * aw])
        vc = v - jnp.mean(v, axis=-1, keepdims=True)
        vn = vc * lax.rsqrt(jnp.mean(vc * vc, axis=-1, keepdims=True) + EPS) * lng_ref[...] + lnb_ref[...]
        for t in range(aw // V7X_LANES):
            lanes = slice(t * V7X_LANES, (t + 1) * V7X_LANES)
            cc = jnp.concatenate([vn[:A_CHUNK, lanes], vn[A_CHUNK:, lanes]], axis=1)
            rhs = jnp.concatenate([jnp.where(first_group, cc, 0.0),
                                   jnp.where(first_group, 0.0, cc)], axis=0).astype(BF16)
            mixed = jnp.dot(w_pairs[t], rhs, preferred_element_type=F32)
            bias = bs_ref[:, lanes]
            for c in range(2):
                chunk = slice(c * A_CHUNK, (c + 1) * A_CHUNK)
                u = _gelu(z[chunk, lanes])
                a_ref[base + c * A_CHUNK:base + (c + 1) * A_CHUNK, lanes] = (
                    u * (mixed[:, c * V7X_LANES:(c + 1) * V7X_LANES] + bias)).astype(BF16)

        for t in range(tiles):
            lanes = slice(t * V7X_LANES, (t + 1) * V7X_LANES)
            q = _head_pair_rms(z[:, 2 * aw + t * V7X_LANES:2 * aw + (t + 1) * V7X_LANES], qg_ref[...], head0)
            q = q * B_SCORE_SCALE
            k = _head_pair_rms(z[:, 2 * aw + bw + t * V7X_LANES:2 * aw + bw + (t + 1) * V7X_LANES],
                               kg_ref[...], head0)
            vb = z[:, 2 * aw + 2 * bw + t * V7X_LANES:2 * aw + 2 * bw + (t + 1) * V7X_LANES]
            kn_ref[rows, lanes] = k.astype(BF16)
            vn_ref[rows, lanes] = vb.astype(BF16)
            for i, val in enumerate((q, k, vb)):
                for grp in range(groups):
                    stage[i * tiles + t, grp * STAGE_PITCH:grp * STAGE_PITCH + B_PLANES, :] = (
                        val[grp * B_PLANES:(grp + 1) * B_PLANES])
            for p in range(B_PLANES):
                sel = pl.ds(p, groups, stride=STAGE_PITCH)
                for i, planes in enumerate((qp_ref, kp_ref, vp_ref)):
                    planes[p, r * groups:(r + 1) * groups, lanes] = stage[i * tiles + t, sel, :].astype(BF16)
            for blk in range(ROW_SUB // B_BLOCK):
                first = blk * per_block * STAGE_PITCH
                slabs = [stage[t, pl.ds(first + p, per_block, stride=STAGE_PITCH), :] for p in range(B_PLANES)]
                qb_ref[base + blk * B_BLOCK:base + (blk + 1) * B_BLOCK, lanes] = (
                    jnp.concatenate(slabs, axis=0).astype(BF16))


def _even_in(x, g, w_all, li, ln_g, ln_b, w_s, b_s, q_g, k_g, *, bsz, tm):
    n, d = x.shape
    seq = n // bsz
    aw = A_GROUPS * A_GROUP_DIM
    f = w_all.shape[2]
    bw = (f - 2 * aw) // 3
    pairs = A_GROUPS // 2
    tiles = seq // tm
    per_plane = tm // B_PLANES
    ws_pairs = w_s.reshape(pairs, 2, A_CHUNK, A_CHUNK).transpose(0, 2, 1, 3).reshape(pairs, A_CHUNK, 2 * A_CHUNK)
    bias = jnp.repeat(b_s.T, A_GROUP_DIM, axis=1)
    qg2 = jnp.tile(q_g, 2).reshape(1, V7X_LANES)
    kg2 = jnp.tile(k_g, 2).reshape(1, V7X_LANES)
    nat_spec = pl.BlockSpec((tm, bw), lambda i: (i, 0))
    plane_spec = pl.BlockSpec((None, B_PLANES, per_plane, bw), lambda i: (i // tiles, 0, i % tiles, 0))
    nat_shape = jax.ShapeDtypeStruct((n, bw), BF16)
    plane_shape = jax.ShapeDtypeStruct((bsz, B_PLANES, seq // B_PLANES, bw), BF16)
    stage_rows = ROW_SUB // B_PLANES * STAGE_PITCH
    vmem = (2 * (tm * d * 4 + d * f * 2 + tm * aw * 2 + 6 * tm * bw * 2)
            + 2 * (ROW_SUB * d * 2 + ROW_SUB * f * 4 + 3 * stage_rows * bw * 4)
            + 2 * ROW_SUB * (2 * aw + 3 * bw) * 4)
    return pl.pallas_call(
        _even_in_kernel,
        grid=(n // tm,),
        in_specs=[pl.BlockSpec((tm, d), lambda i: (i, 0)),
                  pl.BlockSpec((1, d), lambda i: (0, 0)),
                  _layer_spec(w_all, li),
                  pl.BlockSpec((1, aw), lambda i: (0, 0)),
                  pl.BlockSpec((1, aw), lambda i: (0, 0)),
                  pl.BlockSpec((pairs, A_CHUNK, 2 * A_CHUNK), lambda i: (0, 0, 0)),
                  pl.BlockSpec((A_CHUNK, aw), lambda i: (0, 0)),
                  pl.BlockSpec((1, V7X_LANES), lambda i: (0, 0)),
                  pl.BlockSpec((1, V7X_LANES), lambda i: (0, 0))],
        out_specs=(pl.BlockSpec((tm, aw), lambda i: (i, 0)), nat_spec, nat_spec, nat_spec,
                   plane_spec, plane_spec, plane_spec),
        out_shape=(jax.ShapeDtypeStruct((n, aw), BF16), nat_shape, nat_shape, nat_shape,
                   plane_shape, plane_shape, plane_shape),
        scratch_shapes=[pltpu.VMEM((2, ROW_SUB, d), BF16), pltpu.VMEM((2, ROW_SUB, f), F32),
                        pltpu.VMEM((2, 3 * (bw // V7X_LANES), stage_rows, V7X_LANES), F32)],
        compiler_params=_params(("parallel",), vmem),
        name="even_in",
    )(x, g.reshape(1, d), w_all, ln_g.reshape(1, aw), ln_b.reshape(1, aw), ws_pairs, bias, qg2, kg2)


def _dilated_kernel(qb_ref, kn_ref, vn_ref, qp_ref, kp_ref, vp_ref, o_ref, ob_ref, mb_ref, db_ref, mask_ref):
    m_rows = qp_ref.shape[1]
    blk2 = 2 * B_BLOCK
    head0 = lax.broadcasted_iota(jnp.int32, (1, V7X_LANES), 1) < B_HEAD_DIM

    rowi = lax.broadcasted_iota(jnp.int32, (blk2, blk2), 0) % B_BLOCK
    coli = lax.broadcasted_iota(jnp.int32, (blk2, blk2), 1)
    is_cur = coli >= B_BLOCK
    colj = coli % B_BLOCK
    for bi, d in enumerate(B_DILATIONS):
        planes = B_PLANES // d
        mb = B_BLOCK // planes
        i_pos = (rowi % mb) * planes + rowi // mb
        j_pos = colj if d == 1 else (colj % mb) * planes + colj // mb
        band = jnp.where(jnp.where(is_cur, i_pos - j_pos, j_pos - i_pos) >= 0, 0.0, NEG)
        mask_ref[2 * bi] = jnp.where(is_cur, band, NEG)
        mask_ref[2 * bi + 1] = band

    ones = jnp.ones((blk2, V7X_LANES), BF16)

    def attend(qb, kcat, vcat, bias):
        zero = jnp.zeros_like(qb)
        q2 = jnp.concatenate([jnp.where(head0, qb, zero), jnp.where(head0, zero, qb)], axis=0)
        s = lax.dot_general(q2, kcat, NT_DIMS, preferred_element_type=F32) + bias
        m = jnp.max(s, axis=-1, keepdims=True)
        p = jnp.exp2(s - m).astype(BF16)
        pv = jnp.dot(p, jnp.concatenate([vcat, ones], axis=1), preferred_element_type=F32)
        top, bot = pv[:B_BLOCK], pv[B_BLOCK:]
        return (jnp.where(head0, top[:, :V7X_LANES], bot[:, :V7X_LANES]),
                jnp.where(head0, m[:B_BLOCK], m[B_BLOCK:]),
                jnp.where(head0, top[:, V7X_LANES:], bot[:, V7X_LANES:]))

    def token_block(n, carry):
        per_block = B_BLOCK // B_PLANES
        off = pl.multiple_of(n * B_BLOCK, B_BLOCK)
        off_prev = pl.multiple_of(jnp.maximum(n - 1, 0) * B_BLOCK, B_BLOCK)
        off_plane = pl.multiple_of(n * per_block, per_block)
        kcat = jnp.concatenate([kn_ref[pl.ds(off_prev, B_BLOCK), :], kn_ref[pl.ds(off, B_BLOCK), :]], axis=0)
        vcat = jnp.concatenate([vn_ref[pl.ds(off_prev, B_BLOCK), :], vn_ref[pl.ds(off, B_BLOCK), :]], axis=0)
        o, m, den = attend(qb_ref[pl.ds(off, B_BLOCK), :], kcat, vcat, mask_ref[jnp.minimum(n, 1)])
        for p in range(B_PLANES):
            for ref, val in ((ob_ref, o), (mb_ref, m), (db_ref, den)):
                ref[0, p, pl.ds(off_plane, per_block), :] = val[p * per_block:(p + 1) * per_block]
        return carry

    lax.fori_loop(0, kn_ref.shape[0] // B_BLOCK, token_block, 0, unroll=32)

    for bi, d in enumerate(B_DILATIONS):
        if d == 1:
            continue
        planes = B_PLANES // d
        mb = B_BLOCK // planes
        nb = m_rows // mb

        def plane_block(blk, carry, bi=bi, d=d, planes=planes, mb=mb, nb=nb):
            r = blk // nb
            n = blk % nb
            off = pl.multiple_of(n * mb, mb)
            off_prev = pl.multiple_of(jnp.maximum(n - 1, 0) * mb, mb)

            def gather(ref, offs):
                return jnp.concatenate([ref[r + d * a, pl.ds(o, mb), :] for o in offs for a in range(planes)],
                                       axis=0)

            o, m, den = attend(gather(qp_ref, (off,)), gather(kp_ref, (off_prev, off)),
                               gather(vp_ref, (off_prev, off)), mask_ref[2 * bi + jnp.minimum(n, 1)])
            for a in range(planes):
                for ref, val in ((ob_ref, o), (mb_ref, m), (db_ref, den)):
                    ref[bi, r + d * a, pl.ds(off, mb), :] = val[a * mb:(a + 1) * mb]
            return carry

        lax.fori_loop(0, d * nb, plane_block, 0, unroll=32)

    branches = range(len(B_DILATIONS))
    for r in range(B_PLANES):
        mx = functools.reduce(jnp.maximum, [mb_ref[bi, r] for bi in branches])
        es = [jnp.exp2(mb_ref[bi, r] - mx) for bi in branches]
        num = functools.reduce(lambda a, b: a + b, [es[bi] * ob_ref[bi, r] for bi in branches])
        den = functools.reduce(lambda a, b: a + b, [es[bi] * db_ref[bi, r] for bi in branches])
        o_ref[pl.ds(r, m_rows, stride=B_PLANES), :] = num / den


def _dilated(qb, kn, vn, qp, kp, vp):
    bsz, s, bw = kn.shape
    pairs = bw // V7X_LANES
    m_rows = s // B_PLANES
    tile_f32 = s * V7X_LANES * 4
    vmem = 2 * (6 * tile_f32 // 2 + tile_f32) + 9 * tile_f32 + 6 * 4 * B_BLOCK * B_BLOCK * 4 + 8 * tile_f32 // 16
    nat_spec = pl.BlockSpec((None, s, V7X_LANES), lambda b, p: (b, 0, p))
    plane_spec = pl.BlockSpec((None, B_PLANES, m_rows, V7X_LANES), lambda b, p: (b, 0, 0, p))
    return pl.pallas_call(
        _dilated_kernel,
        grid=(bsz, pairs),
        in_specs=[nat_spec, nat_spec, nat_spec, plane_spec, plane_spec, plane_spec],
        out_specs=pl.BlockSpec((None, s, V7X_LANES), lambda b, p: (b, 0, p)),
        out_shape=jax.ShapeDtypeStruct((bsz, s, bw), F32),
        scratch_shapes=[pltpu.VMEM((len(B_DILATIONS), B_PLANES, m_rows, V7X_LANES), F32)] * 3
                       + [pltpu.VMEM((2 * len(B_DILATIONS), 2 * B_BLOCK, 2 * B_BLOCK), F32)],
        compiler_params=_params(("parallel", "parallel"), vmem),
        name="dilated_attention",
    )(qb, kn, vn, qp, kp, vp)


def _gla_kernel(q_ref, k_ref, v_ref, r_ref, ga_ref, wa_ref, ba_ref, hg_ref, o_ref, st_ref):
    tc = q_ref.shape[0]
    chunks = tc // C_CHUNK

    @pl.when(pl.program_id(1) == 0)
    def _():
        st_ref[...] = jnp.zeros_like(st_ref)

    gate = jnp.dot(ga_ref[...].astype(BF16), wa_ref[...], preferred_element_type=F32) + ba_ref[...]
    log_a = jax.nn.log_sigmoid(gate) / C_TAU
    ci = lax.broadcasted_iota(jnp.int32, (C_CHUNK, C_CHUNK), 0)
    cj = lax.broadcasted_iota(jnp.int32, (C_CHUNK, C_CHUNK), 1)
    causal = ci >= cj
    tri = jnp.where(causal, 1.0, 0.0).astype(BF16)
    rows = [slice(c * C_CHUNK, (c + 1) * C_CHUNK) for c in range(chunks)]
    for hd in range(C_HEADS):
        kl = slice(hd * C_DK, (hd + 1) * C_DK)
        vl = slice(hd * C_DV, (hd + 1) * C_DV)
        la = log_a[:, kl]
        hi = la.astype(BF16)
        rest = la - hi.astype(F32)
        mid = rest.astype(BF16)
        lo = (rest - mid.astype(F32)).astype(BF16)
        pieces = jnp.concatenate([hi, mid, lo], axis=1)
        sums = [jnp.dot(tri, pieces[sl], preferred_element_type=F32) for sl in rows]
        b = jnp.concatenate([s3[:, :C_DK] + s3[:, C_DK:2 * C_DK] + s3[:, 2 * C_DK:] for s3 in sums], axis=0)
        b3 = b.reshape(chunks, C_CHUNK, C_DK)
        b_last = b3[:, C_CHUNK - 1:C_CHUNK, :]
        k = k_ref[:, kl]
        q_t = ((q_ref[:, kl] * (C_DK ** -0.5)) * jnp.exp(b)).astype(BF16)
        k_t = (k * jnp.exp(-b)).astype(BF16)
        k_s = (k.reshape(chunks, C_CHUNK, C_DK) * jnp.exp(b_last - b3)).reshape(tc, C_DK).astype(BF16)
        decay = jnp.exp(b_last)
        o_intra, kv_t = [], []
        for sl in rows:
            v_c = v_ref[sl, vl].astype(BF16)
            attn = lax.dot_general(q_t[sl], k_t[sl], NT_DIMS, preferred_element_type=F32)
            attn = jnp.where(causal, attn, 0.0).astype(BF16)
            o_intra.append(jnp.dot(attn, v_c, preferred_element_type=F32))
            kv_t.append(lax.dot_general(v_c, k_s[sl], TN_DIMS, preferred_element_type=F32))
        st = st_ref[hd]
        entering = []
        for c in range(chunks):
            entering.append(st.astype(BF16))
            st = st * decay[c] + kv_t[c]
        st_ref[hd] = st
        for c, sl in enumerate(rows):
            o = o_intra[c] + lax.dot_general(q_t[sl], entering[c], NT_DIMS, preferred_element_type=F32)
            o_ref[sl, vl] = (_rms_rows(o, hg_ref[...]) * _silu(r_ref[sl, vl])).astype(BF16)


def _gla(z3, ga3, w_a2, b_a, head_g, *, tc):
    bsz, s, _ = z3.shape
    hk, hv = C_HEADS * C_DK, C_HEADS * C_DV
    wa = jnp.pad(w_a2, ((0, V7X_LANES - C_GATE_RANK), (0, 0))).astype(BF16)
    vmem = (2 * (2 * tc * hk * 4 + 3 * tc * hv * 4 + tc * V7X_LANES * 4 + V7X_LANES * hk * 2)
            + hv * C_DK * 4 + 12 * tc * C_DK * 4 + 2 * tc * hk * 4)
    return pl.pallas_call(
        _gla_kernel,
        grid=(bsz, s // tc),
        in_specs=[pl.BlockSpec((None, tc, hk), lambda b, t: (b, t, 0)),
                  pl.BlockSpec((None, tc, hk), lambda b, t: (b, t, 1)),
                  pl.BlockSpec((None, tc, hv), lambda b, t: (b, t, 1)),
                  pl.BlockSpec((None, tc, hv), lambda b, t: (b, t, 2)),
                  pl.BlockSpec((None, tc, V7X_LANES), lambda b, t: (b, t, 0)),
                  pl.BlockSpec((V7X_LANES, hk), lambda b, t: (0, 0)),
                  pl.BlockSpec((1, hk), lambda b, t: (0, 0)),
                  pl.BlockSpec((1, C_DV), lambda b, t: (0, 0))],
        out_specs=pl.BlockSpec((None, tc, hv), lambda b, t: (b, t, 0)),
        out_shape=jax.ShapeDtypeStruct((bsz, s, hv), BF16),
        scratch_shapes=[pltpu.VMEM((C_HEADS, C_DV, C_DK), F32)],
        compiler_params=_params(("parallel", "arbitrary"), vmem),
        name="gla",
    )(z3, z3, z3, z3, ga3, wa, b_a.reshape(1, -1), head_g.reshape(1, C_DV))


def _ffn_kernel(x_ref, a_ref, b_ref, wa_ref, wb_ref, g_ref, wg_ref, wu_ref, cw_ref, cb_ref, wd_ref, o_ref,
                h_ref, act_ref, gbuf_ref, halo_ref, *, tiles_per_seq, tf):
    tm = x_ref.shape[0]
    halo_rows = V7X_SUBLANES
    for r in range(tm // ROW_SUB):
        rows = slice(r * ROW_SUB, (r + 1) * ROW_SUB)
        x1 = x_ref[rows, :] + jnp.dot(a_ref[rows, :].astype(BF16), wa_ref[...], preferred_element_type=F32)
        x1 = x1 + jnp.dot(b_ref[rows, :].astype(BF16), wb_ref[...], preferred_element_type=F32)
        o_ref[rows, :] = x1
        h_ref[rows, :] = _rms_rows(x1, g_ref[...]).astype(BF16)

    @pl.when(pl.program_id(0) % tiles_per_seq == 0)
    def _():
        halo_ref[...] = jnp.zeros_like(halo_ref)

    for j in range(wg_ref.shape[1] // tf):
        cols = slice(j * tf, (j + 1) * tf)
        h = h_ref[...]
        gate = jnp.dot(h, wg_ref[:, cols], preferred_element_type=F32)
        up = jnp.dot(h, wu_ref[:, cols], preferred_element_type=F32)
        gbuf = gbuf_ref.at[j % 2]
        gbuf[0:halo_rows, :] = halo_ref[:, cols]
        gbuf[halo_rows:, :] = gate
        halo_ref[:, cols] = gate[tm - halo_rows:, :]
        conv = cb_ref[:, cols]
        for tap in range(CONV_W - 1):
            conv = conv + gbuf[pl.ds(halo_rows - (CONV_W - 1) + tap, tm), :] * cw_ref[tap:tap + 1, cols]
        conv = conv + gate * cw_ref[CONV_W - 1:CONV_W, cols]
        act_ref[:, cols] = (_silu(conv) * up).astype(BF16)
    o_ref[...] += jnp.dot(act_ref[...], wd_ref[...], preferred_element_type=F32)


def _ffn(x, a, a_blk, b, b_blk, w_out_all, lo, g, w_gate_all, w_up_all, conv_w, conv_b, w_down_all, lf,
         *, seq, tm, tf):
    n, d = x.shape
    dff = w_gate_all.shape[2]
    kh = w_out_all.shape[1] // 2
    vmem = (2 * (2 * tm * d * 4 + 2 * tm * kh * 4 + 2 * kh * d * 2 + 3 * d * dff * 2 + 4 * dff * 4)
            + tm * d * 2 + tm * dff * 2 + 2 * (tm + 8) * tf * 4 + 8 * dff * 4 + 8 * tm * tf * 4)
    return pl.pallas_call(
        functools.partial(_ffn_kernel, tiles_per_seq=seq // tm, tf=tf),
        grid=(n // tm,),
        in_specs=[pl.BlockSpec((tm, d), lambda i: (i, 0)),
                  pl.BlockSpec((tm, kh), lambda i: (i, a_blk)),
                  pl.BlockSpec((tm, kh), lambda i: (i, b_blk)),
                  pl.BlockSpec((None, kh, d), lambda i: (lo, 0, 0)),
                  pl.BlockSpec((None, kh, d), lambda i: (lo, 1, 0)),
                  pl.BlockSpec((1, d), lambda i: (0, 0)),
                  _layer_spec(w_gate_all, lf),
                  _layer_spec(w_up_all, lf),
                  pl.BlockSpec((CONV_W, dff), lambda i: (0, 0)),
                  pl.BlockSpec((1, dff), lambda i: (0, 0)),
                  _layer_spec(w_down_all, lf)],
        out_specs=pl.BlockSpec((tm, d), lambda i: (i, 0)),
        out_shape=jax.ShapeDtypeStruct((n, d), F32),
        scratch_shapes=[pltpu.VMEM((tm, d), BF16), pltpu.VMEM((tm, dff), BF16),
                        pltpu.VMEM((2, V7X_SUBLANES + tm, tf), F32),
                        pltpu.VMEM((V7X_SUBLANES, dff), F32)],
        compiler_params=_params(("arbitrary",), vmem),
        name="conv_ffn",
    )(x, a, b, w_out_all, w_out_all, g.reshape(1, d), w_gate_all, w_up_all, conv_w, conv_b.reshape(1, dff),
      w_down_all)


def kernel(x, norm_mix_g, norm_ffn_g, ev_w_in, ev_a_ln_g, ev_a_ln_b, ev_a_ws, ev_a_bs, ev_q_g, ev_k_g,
           ev_w_out, od_w_in, od_w_a2, od_b_a, od_head_g, od_w_out, ffn_w_gate, ffn_w_up, ffn_conv_w,
           ffn_conv_b, ffn_w_down):
    bsz, seq, d = x.shape
    n = bsz * seq
    depth = norm_mix_g.shape[0]
    main = 2 * C_HEADS * (C_DK + C_DV)
    ev_w_in, ev_w_out, od_w_out = ev_w_in.astype(BF16), ev_w_out.astype(BF16), od_w_out.astype(BF16)
    od_w_all = od_w_in.astype(BF16)
    od_w_side = jnp.pad(od_w_in[:, :, main:], ((0, 0), (0, 0), (0, V7X_LANES - C_GATE_RANK))).astype(BF16)
    dff = ffn_w_gate.shape[2]
    ffn_f32 = (ffn_w_gate, ffn_w_up, ffn_w_down.reshape(depth, d, dff))
    ffn_w = {0: tuple(w[:1].astype(BF16) for w in ffn_f32)}
    xf = x.reshape(n, d)
    for layer in range(depth):
        if layer % 2 == 0:
            e = layer // 2
            a_out, qb, kn, vn, qp, kp, vp = _even_in(
                xf, norm_mix_g[layer], ev_w_in, e, ev_a_ln_g[e], ev_a_ln_b[e], ev_a_ws[e], ev_a_bs[e],
                ev_q_g[e], ev_k_g[e], bsz=bsz, tm=1024)
            in_seq = lambda t: t.reshape(bsz, seq, -1)
            b_out = _dilated(in_seq(qb), in_seq(kn), in_seq(vn), qp, kp, vp).reshape(n, -1)
            mix = (a_out, 0, b_out, 0, ev_w_out, e)
        else:
            o = layer // 2
            cast_layers = [l for l in (layer, layer + 1) if l < depth]
            z, ga, cast = _norm_matmul(xf, norm_mix_g[layer], od_w_all, o, main, od_w_side[o],
                                       [(w, l) for l in cast_layers for w in ffn_f32], tm=512)
            for k, l in enumerate(cast_layers):
                ffn_w[l] = tuple(cast[3 * k:3 * k + 3])
            mixed = _gla(z.reshape(bsz, seq, -1), ga.reshape(bsz, seq, -1), od_w_a2[o], od_b_a[o],
                         od_head_g[o], tc=512).reshape(n, -1)
            mix = (mixed, 0, mixed, 1, od_w_out, o)
        w_gate, w_up, w_down = ffn_w.pop(layer)
        xf = _ffn(xf, *mix, norm_ffn_g[layer], w_gate, w_up, ffn_conv_w[layer], ffn_conv_b[layer],
                  w_down.reshape(1, dff, d), 0, seq=seq, tm=512, tf=256)
    return xf.reshape(bsz, seq, d)
```

```python
import functools

import jax
import jax.numpy as jnp
from jax import lax
from jax.experimental import pallas as pl
from jax.experimental.pallas import tpu as pltpu

A_GROUPS = 8
A_GROUP_DIM = 64
A_CHUNK = 128
B_HEAD_DIM = 64
B_DILATIONS = (1, 4, 16)
B_BLOCK = 128
C_HEADS = 4
C_DK = 128
C_DV = 256
C_GATE_RANK = 16
C_TAU = 16.0
C_CHUNK = 64
CONV_W = 3
EPS = 1e-6
NEG = -1e30

V7X_LANES = 128
V7X_SUBLANES = 8
V7X_VMEM_BUDGET = 56 * 1024 * 1024
BF16_ROWS = 16
FFN_ROWS = 512

F32 = jnp.float32
BF16 = jnp.bfloat16
NT_DIMS = (((1,), (1,)), ((), ()))
TN_DIMS = (((0,), (0,)), ((), ()))

ROW_SUB = 2 * A_CHUNK
B_PLANES = max(B_DILATIONS)
STAGE_PITCH = 24
B_SCORE_SCALE = B_HEAD_DIM ** -0.5 * 1.4426950408889634


def _params(semantics, vmem_bytes):
    return pltpu.CompilerParams(
        dimension_semantics=semantics,
        vmem_limit_bytes=min(int(vmem_bytes * 1.25) + (4 << 20), V7X_VMEM_BUDGET))


def _layer_spec(w_all, li):
    return pl.BlockSpec((None,) + w_all.shape[1:], lambda i: (li, 0, 0))


def _rms_rows(x, g):
    return x * lax.rsqrt(jnp.mean(x * x, axis=-1, keepdims=True) + EPS) * g


def _gelu(x):
    return 0.5 * x * (1.0 + lax.erf(x * (0.5 ** 0.5)))


def _silu(x):
    return x * jax.nn.sigmoid(x)


def _head_pair_rms(x, g, head0):
    x2 = x * x
    s0 = jnp.sum(jnp.where(head0, x2, 0.0), axis=-1, keepdims=True)
    s1 = jnp.sum(jnp.where(head0, 0.0, x2), axis=-1, keepdims=True)
    ms = jnp.where(head0, s0, s1) * (1.0 / B_HEAD_DIM)
    return x * lax.rsqrt(ms + EPS) * g


def _norm_matmul_kernel(x_ref, g_ref, w_ref, ws_ref, o_ref, os_ref):
    for r in range(x_ref.shape[0] // ROW_SUB):
        rows = slice(r * ROW_SUB, (r + 1) * ROW_SUB)
        h = _rms_rows(x_ref[rows, :], g_ref[...]).astype(BF16)
        o_ref[rows, :] = jnp.dot(h, w_ref[:, :o_ref.shape[1]], preferred_element_type=F32)
        os_ref[rows, :] = jnp.dot(h, ws_ref[...], preferred_element_type=F32)


def _norm_matmul(x, g, w_all, li, f, w_side, *, tm):
    n, d = x.shape
    fs = w_side.shape[1]
    vmem = 2 * (tm * d * 4 + d * (w_all.shape[2] + fs) * 2 + tm * (f + fs) * 4) + 2 * ROW_SUB * f * 4
    return pl.pallas_call(
        _norm_matmul_kernel,
        grid=(n // tm,),
        in_specs=[pl.BlockSpec((tm, d), lambda i: (i, 0)),
                  pl.BlockSpec((1, d), lambda i: (0, 0)),
                  _layer_spec(w_all, li),
                  pl.BlockSpec((d, fs), lambda i: (0, 0))],
        out_specs=(pl.BlockSpec((tm, f), lambda i: (i, 0)), pl.BlockSpec((tm, fs), lambda i: (i, 0))),
        out_shape=(jax.ShapeDtypeStruct((n, f), F32), jax.ShapeDtypeStruct((n, fs), F32)),
        compiler_params=_params(("parallel",), vmem),
        name="norm_matmul",
    )(x, g.reshape(1, d), w_all, w_side)


def _even_in_kernel(x_ref, g_ref, w_ref, lng_ref, lnb_ref, ws_ref, bs_ref, qg_ref, kg_ref,
                    a_ref, qb_ref, kn_ref, vn_ref, qp_ref, kp_ref, vp_ref, h_ref, z_ref, stage_ref):
    tm = x_ref.shape[0]
    aw = a_ref.shape[1]
    bw = kn_ref.shape[1]
    tiles = bw // V7X_LANES
    groups = ROW_SUB // B_PLANES
    per_block = B_BLOCK // B_PLANES
    row = lax.broadcasted_iota(jnp.int32, (A_CHUNK, 2 * A_CHUNK), 0)
    col = lax.broadcasted_iota(jnp.int32, (A_CHUNK, 2 * A_CHUNK), 1)
    causal = (col % A_CHUNK) <= row
    lane2 = lax.broadcasted_iota(jnp.int32, (1, 2 * V7X_LANES), 1)
    first_group = (lane2 % V7X_LANES) < A_GROUP_DIM
    head0 = lax.broadcasted_iota(jnp.int32, (1, V7X_LANES), 1) < B_HEAD_DIM
    w_pairs = [jnp.where(causal, ws_ref[t], 0.0).astype(BF16) for t in range(ws_ref.shape[0])]
    for r in range(tm // ROW_SUB):
        base = r * ROW_SUB
        rows = slice(base, base + ROW_SUB)
        h, z, stage = h_ref.at[r % 2], z_ref.at[r % 2], stage_ref.at[r % 2]
        h[...] = _rms_rows(x_ref[rows, :], g_ref[...]).astype(BF16)
        z[...] = jnp.dot(h[...], w_ref[...], preferred_element_type=F32)

        v = _gelu(z[:, aw:2 * aw])
        vc = v - jnp.mean(v, axis=-1, keepdims=True)
        vn = vc * lax.rsqrt(jnp.mean(vc * vc, axis=-1, keepdims=True) + EPS) * lng_ref[...] + lnb_ref[...]
        for t in range(aw // V7X_LANES):
            lanes = slice(t * V7X_LANES, (t + 1) * V7X_LANES)
            cc = jnp.concatenate([vn[:A_CHUNK, lanes], vn[A_CHUNK:, lanes]], axis=1)
            rhs = jnp.concatenate([jnp.where(first_group, cc, 0.0),
                                   jnp.where(first_group, 0.0, cc)], axis=0).astype(BF16)
            mixed = jnp.dot(w_pairs[t], rhs, preferred_element_type=F32)
            bias = bs_ref[:, lanes]
            for c in range(2):
                chunk = slice(c * A_CHUNK, (c + 1) * A_CHUNK)
                u = _gelu(z[chunk, lanes])
                a_ref[base + c * A_CHUNK:base + (c + 1) * A_CHUNK, lanes] = (
                    u * (mixed[:, c * V7X_LANES:(c + 1) * V7X_LANES] + bias)).astype(BF16)

        for t in range(tiles):
            lanes = slice(t * V7X_LANES, (t + 1) * V7X_LANES)
            q = _head_pair_rms(z[:, 2 * aw + t * V7X_LANES:2 * aw + (t + 1) * V7X_LANES], qg_ref[...], head0)
            q = q * B_SCORE_SCALE
            k = _head_pair_rms(z[:, 2 * aw + bw + t * V7X_LANES:2 * aw + bw + (t + 1) * V7X_LANES],
                               kg_ref[...], head0)
            vb = z[:, 2 * aw + 2 * bw + t * V7X_LANES:2 * aw + 2 * bw + (t + 1) * V7X_LANES]
            kn_ref[rows, lanes] = k.astype(BF16)
            vn_ref[rows, lanes] = vb.astype(BF16)
            for i, val in enumerate((q, k, vb)):
                for grp in range(groups):
                    stage[i * tiles + t, grp * STAGE_PITCH:grp * STAGE_PITCH + B_PLANES, :] = (
                        val[grp * B_PLANES:(grp + 1) * B_PLANES])
            for p in range(B_PLANES):
                sel = pl.ds(p, groups, stride=STAGE_PITCH)
                for i, planes in enumerate((qp_ref, kp_ref, vp_ref)):
                    planes[p, r * groups:(r + 1) * groups, lanes] = stage[i * tiles + t, sel, :].astype(BF16)
            for blk in range(ROW_SUB // B_BLOCK):
                first = blk * per_block * STAGE_PITCH
                slabs = [stage[t, pl.ds(first + p, per_block, stride=STAGE_PITCH), :] for p in range(B_PLANES)]
                qb_ref[base + blk * B_BLOCK:base + (blk + 1) * B_BLOCK, lanes] = (
                    jnp.concatenate(slabs, axis=0).astype(BF16))


def _even_in(x, g, w_all, li, ln_g, ln_b, w_s, b_s, q_g, k_g, *, bsz, tm):
    n, d = x.shape
    seq = n // bsz
    aw = A_GROUPS * A_GROUP_DIM
    f = w_all.shape[2]
    bw = (f - 2 * aw) // 3
    pairs = A_GROUPS // 2
    tiles = seq // tm
    per_plane = tm // B_PLANES
    ws_pairs = w_s.reshape(pairs, 2, A_CHUNK, A_CHUNK).transpose(0, 2, 1, 3).reshape(pairs, A_CHUNK, 2 * A_CHUNK)
    bias = jnp.repeat(b_s.T, A_GROUP_DIM, axis=1)
    qg2 = jnp.tile(q_g, 2).reshape(1, V7X_LANES)
    kg2 = jnp.tile(k_g, 2).reshape(1, V7X_LANES)
    nat_spec = pl.BlockSpec((tm, bw), lambda i: (i, 0))
    plane_spec = pl.BlockSpec((None, B_PLANES, per_plane, bw), lambda i: (i // tiles, 0, i % tiles, 0))
    nat_shape = jax.ShapeDtypeStruct((n, bw), BF16)
    plane_shape = jax.ShapeDtypeStruct((bsz, B_PLANES, seq // B_PLANES, bw), BF16)
    stage_rows = ROW_SUB // B_PLANES * STAGE_PITCH
    vmem = (2 * (tm * d * 4 + d * f * 2 + tm * aw * 2 + 6 * tm * bw * 2)
            + 2 * (ROW_SUB * d * 2 + ROW_SUB * f * 4 + 3 * stage_rows * bw * 4)
            + 2 * ROW_SUB * (2 * aw + 3 * bw) * 4)
    return pl.pallas_call(
        _even_in_kernel,
        grid=(n // tm,),
        in_specs=[pl.BlockSpec((tm, d), lambda i: (i, 0)),
                  pl.BlockSpec((1, d), lambda i: (0, 0)),
                  _layer_spec(w_all, li),
                  pl.BlockSpec((1, aw), lambda i: (0, 0)),
                  pl.BlockSpec((1, aw), lambda i: (0, 0)),
                  pl.BlockSpec((pairs, A_CHUNK, 2 * A_CHUNK), lambda i: (0, 0, 0)),
                  pl.BlockSpec((A_CHUNK, aw), lambda i: (0, 0)),
                  pl.BlockSpec((1, V7X_LANES), lambda i: (0, 0)),
                  pl.BlockSpec((1, V7X_LANES), lambda i: (0, 0))],
        out_specs=(pl.BlockSpec((tm, aw), lambda i: (i, 0)), nat_spec, nat_spec, nat_spec,
                   plane_spec, plane_spec, plane_spec),
        out_shape=(jax.ShapeDtypeStruct((n, aw), BF16), nat_shape, nat_shape, nat_shape,
                   plane_shape, plane_shape, plane_shape),
        scratch_shapes=[pltpu.VMEM((2, ROW_SUB, d), BF16), pltpu.VMEM((2, ROW_SUB, f), F32),
                        pltpu.VMEM((2, 3 * (bw // V7X_LANES), stage_rows, V7X_LANES), F32)],
        compiler_params=_params(("parallel",), vmem),
        name="even_in",
    )(x, g.reshape(1, d), w_all, ln_g.reshape(1, aw), ln_b.reshape(1, aw), ws_pairs, bias, qg2, kg2)


def _dilated_kernel(qb_ref, kn_ref, vn_ref, qp_ref, kp_ref, vp_ref, o_ref, ob_ref, mb_ref, db_ref, mask_ref):
    m_rows = qp_ref.shape[1]
    blk2 = 2 * B_BLOCK
    head0 = lax.broadcasted_iota(jnp.int32, (1, V7X_LANES), 1) < B_HEAD_DIM

    rowi = lax.broadcasted_iota(jnp.int32, (blk2, blk2), 0) % B_BLOCK
    coli = lax.broadcasted_iota(jnp.int32, (blk2, blk2), 1)
    is_cur = coli >= B_BLOCK
    colj = coli % B_BLOCK
    for bi, d in enumerate(B_DILATIONS):
        planes = B_PLANES // d
        mb = B_BLOCK // planes
        i_pos = (rowi % mb) * planes + rowi // mb
        j_pos = colj if d == 1 else (colj % mb) * planes + colj // mb
        band = jnp.where(jnp.where(is_cur, i_pos - j_pos, j_pos - i_pos) >= 0, 0.0, NEG)
        mask_ref[2 * bi] = jnp.where(is_cur, band, NEG)
        mask_ref[2 * bi + 1] = band

    ones = jnp.ones((blk2, V7X_LANES), BF16)

    def attend(qb, kcat, vcat, bias):
        zero = jnp.zeros_like(qb)
        q2 = jnp.concatenate([jnp.where(head0, qb, zero), jnp.where(head0, zero, qb)], axis=0)
        s = lax.dot_general(q2, kcat, NT_DIMS, preferred_element_type=F32) + bias
        m = jnp.max(s, axis=-1, keepdims=True)
        p = jnp.exp2(s - m).astype(BF16)
        pv = jnp.dot(p, jnp.concatenate([vcat, ones], axis=1), preferred_element_type=F32)
        top, bot = pv[:B_BLOCK], pv[B_BLOCK:]
        return (jnp.where(head0, top[:, :V7X_LANES], bot[:, :V7X_LANES]),
                jnp.where(head0, m[:B_BLOCK], m[B_BLOCK:]),
                jnp.where(head0, top[:, V7X_LANES:], bot[:, V7X_LANES:]))

    def token_block(n, carry):
        per_block = B_BLOCK // B_PLANES
        off = pl.multiple_of(n * B_BLOCK, B_BLOCK)
        off_prev = pl.multiple_of(jnp.maximum(n - 1, 0) * B_BLOCK, B_BLOCK)
        off_plane = pl.multiple_of(n * per_block, per_block)
        kcat = jnp.concatenate([kn_ref[pl.ds(off_prev, B_BLOCK), :], kn_ref[pl.ds(off, B_BLOCK), :]], axis=0)
        vcat = jnp.concatenate([vn_ref[pl.ds(off_prev, B_BLOCK), :], vn_ref[pl.ds(off, B_BLOCK), :]], axis=0)
        o, m, den = attend(qb_ref[pl.ds(off, B_BLOCK), :], kcat, vcat, mask_ref[jnp.minimum(n, 1)])
        for p in range(B_PLANES):
            for ref, val in ((ob_ref, o), (mb_ref, m), (db_ref, den)):
                ref[0, p, pl.ds(off_plane, per_block), :] = val[p * per_block:(p + 1) * per_block]
        return carry

    lax.fori_loop(0, kn_ref.shape[0] // B_BLOCK, token_block, 0, unroll=32)

    for bi, d in enumerate(B_DILATIONS):
        if d == 1:
            continue
        planes = B_PLANES // d
        mb = B_BLOCK // planes
        nb = m_rows // mb

        def plane_block(blk, carry, bi=bi, d=d, planes=planes, mb=mb, nb=nb):
            r = blk // nb
            n = blk % nb
            off = pl.multiple_of(n * mb, mb)
            off_prev = pl.multiple_of(jnp.maximum(n - 1, 0) * mb, mb)

            def gather(ref, offs):
                return jnp.concatenate([ref[r + d * a, pl.ds(o, mb), :] for o in offs for a in range(planes)],
                                       axis=0)

            o, m, den = attend(gather(qp_ref, (off,)), gather(kp_ref, (off_prev, off)),
                               gather(vp_ref, (off_prev, off)), mask_ref[2 * bi + jnp.minimum(n, 1)])
            for a in range(planes):
                for ref, val in ((ob_ref, o), (mb_ref, m), (db_ref, den)):
                    ref[bi, r + d * a, pl.ds(off, mb), :] = val[a * mb:(a + 1) * mb]
            return carry

        lax.fori_loop(0, d * nb, plane_block, 0, unroll=32)

    branches = range(len(B_DILATIONS))
    for r in range(B_PLANES):
        mx = functools.reduce(jnp.maximum, [mb_ref[bi, r] for bi in branches])
        es = [jnp.exp2(mb_ref[bi, r] - mx) for bi in branches]
        num = functools.reduce(lambda a, b: a + b, [es[bi] * ob_ref[bi, r] for bi in branches])
        den = functools.reduce(lambda a, b: a + b, [es[bi] * db_ref[bi, r] for bi in branches])
        o_ref[pl.ds(r, m_rows, stride=B_PLANES), :] = num / den


def _dilated(qb, kn, vn, qp, kp, vp):
    bsz, s, bw = kn.shape
    pairs = bw // V7X_LANES
    m_rows = s // B_PLANES
    tile_f32 = s * V7X_LANES * 4
    vmem = 2 * (6 * tile_f32 // 2 + tile_f32) + 9 * tile_f32 + 6 * 4 * B_BLOCK * B_BLOCK * 4 + 8 * tile_f32 // 16
    nat_spec = pl.BlockSpec((None, s, V7X_LANES), lambda b, p: (b, 0, p))
    plane_spec = pl.BlockSpec((None, B_PLANES, m_rows, V7X_LANES), lambda b, p: (b, 0, 0, p))
    return pl.pallas_call(
        _dilated_kernel,
        grid=(bsz, pairs),
        in_specs=[nat_spec, nat_spec, nat_spec, plane_spec, plane_spec, plane_spec],
        out_specs=pl.BlockSpec((None, s, V7X_LANES), lambda b, p: (b, 0, p)),
        out_shape=jax.ShapeDtypeStruct((bsz, s, bw), F32),
        scratch_shapes=[pltpu.VMEM((len(B_DILATIONS), B_PLANES, m_rows, V7X_LANES), F32)] * 3
                       + [pltpu.VMEM((2 * len(B_DILATIONS), 2 * B_BLOCK, 2 * B_BLOCK), F32)],
        compiler_params=_params(("parallel", "parallel"), vmem),
        name="dilated_attention",
    )(qb, kn, vn, qp, kp, vp)


def _gla_kernel(q_ref, k_ref, v_ref, r_ref, ga_ref, wa_ref, ba_ref, hg_ref, o_ref, st_ref):
    tc = q_ref.shape[0]
    chunks = tc // C_CHUNK

    @pl.when(pl.program_id(1) == 0)
    def _():
        st_ref[...] = jnp.zeros_like(st_ref)

    gate = jnp.dot(ga_ref[...].astype(BF16), wa_ref[...], preferred_element_type=F32) + ba_ref[...]
    log_a = jax.nn.log_sigmoid(gate) / C_TAU
    ci = lax.broadcasted_iota(jnp.int32, (C_CHUNK, C_CHUNK), 0)
    cj = lax.broadcasted_iota(jnp.int32, (C_CHUNK, C_CHUNK), 1)
    causal = ci >= cj
    tri = jnp.where(causal, 1.0, 0.0).astype(BF16)
    rows = [slice(c * C_CHUNK, (c + 1) * C_CHUNK) for c in range(chunks)]
    for hd in range(C_HEADS):
        kl = slice(hd * C_DK, (hd + 1) * C_DK)
        vl = slice(hd * C_DV, (hd + 1) * C_DV)
        la = log_a[:, kl]
        hi = la.astype(BF16)
        rest = la - hi.astype(F32)
        mid = rest.astype(BF16)
        lo = (rest - mid.astype(F32)).astype(BF16)
        pieces = jnp.concatenate([hi, mid, lo], axis=1)
        sums = [jnp.dot(tri, pieces[sl], preferred_element_type=F32) for sl in rows]
        b = jnp.concatenate([s3[:, :C_DK] + s3[:, C_DK:2 * C_DK] + s3[:, 2 * C_DK:] for s3 in sums], axis=0)
        b3 = b.reshape(chunks, C_CHUNK, C_DK)
        b_last = b3[:, C_CHUNK - 1:C_CHUNK, :]
        k = k_ref[:, kl]
        q_t = ((q_ref[:, kl] * (C_DK ** -0.5)) * jnp.exp(b)).astype(BF16)
        k_t = (k * jnp.exp(-b)).astype(BF16)
        k_s = (k.reshape(chunks, C_CHUNK, C_DK) * jnp.exp(b_last - b3)).reshape(tc, C_DK).astype(BF16)
        decay = jnp.exp(b_last)
        o_intra, kv_t = [], []
        for sl in rows:
            v_c = v_ref[sl, vl].astype(BF16)
            attn = lax.dot_general(q_t[sl], k_t[sl], NT_DIMS, preferred_element_type=F32)
            attn = jnp.where(causal, attn, 0.0).astype(BF16)
            o_intra.append(jnp.dot(attn, v_c, preferred_element_type=F32))
            kv_t.append(lax.dot_general(v_c, k_s[sl], TN_DIMS, preferred_element_type=F32))
        st = st_ref[hd]
        entering = []
        for c in range(chunks):
            entering.append(st.astype(BF16))
            st = st * decay[c] + kv_t[c]
        st_ref[hd] = st
        for c, sl in enumerate(rows):
            o = o_intra[c] + lax.dot_general(q_t[sl], entering[c], NT_DIMS, preferred_element_type=F32)
            o_ref[sl, vl] = (_rms_rows(o, hg_ref[...]) * _silu(r_ref[sl, vl])).astype(BF16)


def _gla(z3, ga3, w_a2, b_a, head_g, *, tc):
    bsz, s, _ = z3.shape
    hk, hv = C_HEADS * C_DK, C_HEADS * C_DV
    wa = jnp.pad(w_a2, ((0, V7X_LANES - C_GATE_RANK), (0, 0))).astype(BF16)
    vmem = (2 * (2 * tc * hk * 4 + 3 * tc * hv * 4 + tc * V7X_LANES * 4 + V7X_LANES * hk * 2)
            + hv * C_DK * 4 + 12 * tc * C_DK * 4 + 2 * tc * hk * 4)
    return pl.pallas_call(
        _gla_kernel,
        grid=(bsz, s // tc),
        in_specs=[pl.BlockSpec((None, tc, hk), lambda b, t: (b, t, 0)),
                  pl.BlockSpec((None, tc, hk), lambda b, t: (b, t, 1)),
                  pl.BlockSpec((None, tc, hv), lambda b, t: (b, t, 1)),
                  pl.BlockSpec((None, tc, hv), lambda b, t: (b, t, 2)),
                  pl.BlockSpec((None, tc, V7X_LANES), lambda b, t: (b, t, 0)),
                  pl.BlockSpec((V7X_LANES, hk), lambda b, t: (0, 0)),
                  pl.BlockSpec((1, hk), lambda b, t: (0, 0)),
                  pl.BlockSpec((1, C_DV), lambda b, t: (0, 0))],
        out_specs=pl.BlockSpec((None, tc, hv), lambda b, t: (b, t, 0)),
        out_shape=jax.ShapeDtypeStruct((bsz, s, hv), BF16),
        scratch_shapes=[pltpu.VMEM((C_HEADS, C_DV, C_DK), F32)],
        compiler_params=_params(("parallel", "arbitrary"), vmem),
        name="gla",
    )(z3, z3, z3, z3, ga3, wa, b_a.reshape(1, -1), head_g.reshape(1, C_DV))


def _ffn_kernel(x_ref, a_ref, b_ref, wa_ref, wb_ref, g_ref, wg_ref, wu_ref, cw_ref, cb_ref, wd_ref, *rest,
                tiles_per_seq, tf, n_cast):
    cast_src, o_ref, cast_dst = rest[:n_cast], rest[n_cast], rest[n_cast + 1:2 * n_cast + 1]
    h_ref, act_ref, gbuf_ref, halo_ref = rest[2 * n_cast + 1:]
    tm = x_ref.shape[0]
    halo_rows = V7X_SUBLANES
    for r in range(tm // ROW_SUB):
        rows = slice(r * ROW_SUB, (r + 1) * ROW_SUB)
        x1 = x_ref[rows, :] + jnp.dot(a_ref[rows, :].astype(BF16), wa_ref[...], preferred_element_type=F32)
        x1 = x1 + jnp.dot(b_ref[rows, :].astype(BF16), wb_ref[...], preferred_element_type=F32)
        o_ref[rows, :] = x1
        h_ref[rows, :] = _rms_rows(x1, g_ref[...]).astype(BF16)

    @pl.when(pl.program_id(0) % tiles_per_seq == 0)
    def _():
        halo_ref[...] = jnp.zeros_like(halo_ref)

    for j in range(wg_ref.shape[1] // tf):
        cols = slice(j * tf, (j + 1) * tf)
        h = h_ref[...]
        gate = jnp.dot(h, wg_ref[:, cols], preferred_element_type=F32)
        up = jnp.dot(h, wu_ref[:, cols], preferred_element_type=F32)
        gbuf = gbuf_ref.at[j % 2]
        gbuf[0:halo_rows, :] = halo_ref[:, cols]
        gbuf[halo_rows:, :] = gate
        halo_ref[:, cols] = gate[tm - halo_rows:, :]
        conv = cb_ref[:, cols]
        for tap in range(CONV_W - 1):
            conv = conv + gbuf[pl.ds(halo_rows - (CONV_W - 1) + tap, tm), :] * cw_ref[tap:tap + 1, cols]
        conv = conv + gate * cw_ref[CONV_W - 1:CONV_W, cols]
        act_ref[:, cols] = (_silu(conv) * up).astype(BF16)
    o_ref[...] += jnp.dot(act_ref[...], wd_ref[...], preferred_element_type=F32)
    for src, dst in zip(cast_src, cast_dst):
        dst[...] = src[...].astype(BF16)


def _ffn(x, a, a_blk, b, b_blk, w_out_all, lo, g, w_gate_all, w_up_all, conv_w, conv_b, w_down_all, lf,
         cast=(), *, seq, tm, tf):
    n, d = x.shape
    dff = w_gate_all.shape[2]
    kh = w_out_all.shape[1] // 2
    cast_in, cast_out, cast_shape = [], [], []
    for arr, layer, slabs in cast:
        slab = (None, arr.shape[1] // slabs, arr.shape[2])
        cast_in.append(pl.BlockSpec(slab, lambda i, layer=layer, last=slabs - 1: (layer, jnp.minimum(i, last), 0)))
        cast_out.append(pl.BlockSpec(slab, lambda i, last=slabs - 1: (0, jnp.minimum(i, last), 0)))
        cast_shape.append(jax.ShapeDtypeStruct((1,) + arr.shape[1:], BF16))
    vmem = (2 * (2 * tm * d * 4 + 2 * tm * kh * 4 + 2 * kh * d * 2 + 3 * d * dff * 2 + 4 * dff * 4)
            + tm * d * 2 + tm * dff * 2 + 2 * (tm + 8) * tf * 4 + 8 * dff * 4 + 8 * tm * tf * 4
            + sum(2 * (arr.shape[1] // slabs) * arr.shape[2] * 6 for arr, _, slabs in cast))
    outs = pl.pallas_call(
        functools.partial(_ffn_kernel, tiles_per_seq=seq // tm, tf=tf, n_cast=len(cast)),
        grid=(n // tm,),
        in_specs=[pl.BlockSpec((tm, d), lambda i: (i, 0)),
                  pl.BlockSpec((tm, kh), lambda i: (i, a_blk)),
                  pl.BlockSpec((tm, kh), lambda i: (i, b_blk)),
                  pl.BlockSpec((None, kh, d), lambda i: (lo, 0, 0)),
                  pl.BlockSpec((None, kh, d), lambda i: (lo, 1, 0)),
                  pl.BlockSpec((1, d), lambda i: (0, 0)),
                  _layer_spec(w_gate_all, lf),
                  _layer_spec(w_up_all, lf),
                  pl.BlockSpec((CONV_W, dff), lambda i: (0, 0)),
                  pl.BlockSpec((1, dff), lambda i: (0, 0)),
                  _layer_spec(w_down_all, lf)] + cast_in,
        out_specs=[pl.BlockSpec((tm, d), lambda i: (i, 0))] + cast_out,
        out_shape=[jax.ShapeDtypeStruct((n, d), F32)] + cast_shape,
        scratch_shapes=[pltpu.VMEM((tm, d), BF16), pltpu.VMEM((tm, dff), BF16),
                        pltpu.VMEM((2, V7X_SUBLANES + tm, tf), F32),
                        pltpu.VMEM((V7X_SUBLANES, dff), F32)],
        compiler_params=_params(("arbitrary",), vmem),
        name="conv_ffn",
    )(x, a, b, w_out_all, w_out_all, g.reshape(1, d), w_gate_all, w_up_all, conv_w, conv_b.reshape(1, dff),
      w_down_all, *[arr for arr, _, _ in cast])
    return outs[0], outs[1:]


def kernel(x, norm_mix_g, norm_ffn_g, ev_w_in, ev_a_ln_g, ev_a_ln_b, ev_a_ws, ev_a_bs, ev_q_g, ev_k_g,
           ev_w_out, od_w_in, od_w_a2, od_b_a, od_head_g, od_w_out, ffn_w_gate, ffn_w_up, ffn_conv_w,
           ffn_conv_b, ffn_w_down):
    bsz, seq, d = x.shape
    n = bsz * seq
    depth = norm_mix_g.shape[0]
    main = 2 * C_HEADS * (C_DK + C_DV)
    ev_w_in, ev_w_out, od_w_out = ev_w_in.astype(BF16), ev_w_out.astype(BF16), od_w_out.astype(BF16)
    od_w_all = od_w_in.astype(BF16)
    od_w_side = jnp.pad(od_w_in[:, :, main:], ((0, 0), (0, 0), (0, V7X_LANES - C_GATE_RANK))).astype(BF16)
    ffn_f32 = (ffn_w_gate, ffn_w_up, ffn_w_down)
    ffn_w = [w[:1].astype(BF16) for w in ffn_f32]
    steps = n // FFN_ROWS
    cast_slabs = [steps // (1 if w.shape[1] // steps % BF16_ROWS == 0 else 2) for w in ffn_f32]
    xf = x.reshape(n, d)
    for layer in range(depth):
        if layer % 2 == 0:
            e = layer // 2
            a_out, qb, kn, vn, qp, kp, vp = _even_in(
                xf, norm_mix_g[layer], ev_w_in, e, ev_a_ln_g[e], ev_a_ln_b[e], ev_a_ws[e], ev_a_bs[e],
                ev_q_g[e], ev_k_g[e], bsz=bsz, tm=1024)
            in_seq = lambda t: t.reshape(bsz, seq, -1)
            b_out = _dilated(in_seq(qb), in_seq(kn), in_seq(vn), qp, kp, vp).reshape(n, -1)
            mix = (a_out, 0, b_out, 0, ev_w_out, e)
        else:
            o = layer // 2
            z, ga = _norm_matmul(xf, norm_mix_g[layer], od_w_all, o, main, od_w_side[o], tm=512)
            mixed = _gla(z.reshape(bsz, seq, -1), ga.reshape(bsz, seq, -1), od_w_a2[o], od_b_a[o],
                         od_head_g[o], tc=512).reshape(n, -1)
            mix = (mixed, 0, mixed, 1, od_w_out, o)
        cast = [(w, layer + 1, slabs) for w, slabs in zip(ffn_f32, cast_slabs)] if layer + 1 < depth else []
        xf, ffn_w_next = _ffn(xf, *mix, norm_ffn_g[layer], ffn_w[0], ffn_w[1], ffn_conv_w[layer],
                              ffn_conv_b[layer], ffn_w[2], 0, cast, seq=seq, tm=FFN_ROWS, tf=256)
        ffn_w = ffn_w_next
    return xf.reshape(bsz, seq, d)
```

```python
import functools

import jax
import jax.numpy as jnp
from jax import lax
from jax.experimental import pallas as pl
from jax.experimental.pallas import tpu as pltpu

A_GROUPS = 8
A_GROUP_DIM = 64
A_CHUNK = 128
B_HEAD_DIM = 64
B_DILATIONS = (1, 4, 16)
B_BLOCK = 128
C_HEADS = 4
C_DK = 128
C_DV = 256
C_GATE_RANK = 16
C_TAU = 16.0
C_CHUNK = 64
CONV_W = 3
EPS = 1e-6
NEG = -1e30

V7X_LANES = 128
V7X_SUBLANES = 8
V7X_VMEM_BUDGET = 56 * 1024 * 1024
BF16_ROWS = 16
FFN_ROWS = 512

F32 = jnp.float32
BF16 = jnp.bfloat16
NT_DIMS = (((1,), (1,)), ((), ()))
TN_DIMS = (((0,), (0,)), ((), ()))

ROW_SUB = 2 * A_CHUNK
B_PLANES = max(B_DILATIONS)
STAGE_PITCH = 24
B_SCORE_SCALE = B_HEAD_DIM ** -0.5 * 1.4426950408889634


def _params(semantics, vmem_bytes):
    return pltpu.CompilerParams(
        dimension_semantics=semantics,
        vmem_limit_bytes=min(int(vmem_bytes * 1.25) + (4 << 20), V7X_VMEM_BUDGET))


def _layer_spec(w_all, li):
    return pl.BlockSpec((None,) + w_all.shape[1:], lambda i: (li, 0, 0))


def _cast_slabs(arr, steps):
    return steps if arr.shape[1] // steps % BF16_ROWS == 0 else steps // 2


def _cast_plumbing(cast, linear_step):
    ins, outs, shapes, vmem = [], [], [], 0
    for arr, layer, slabs in cast:
        slab = (None, arr.shape[1] // slabs, arr.shape[2])
        ins.append(pl.BlockSpec(
            slab, lambda *g, layer=layer, last=slabs - 1: (layer, jnp.minimum(linear_step(*g), last), 0)))
        outs.append(pl.BlockSpec(slab, lambda *g, last=slabs - 1: (0, jnp.minimum(linear_step(*g), last), 0)))
        shapes.append(jax.ShapeDtypeStruct((1,) + arr.shape[1:], BF16))
        vmem += 2 * slab[1] * slab[2] * 6
    return ins, outs, shapes, vmem


def _rms_rows(x, g):
    return x * lax.rsqrt(jnp.mean(x * x, axis=-1, keepdims=True) + EPS) * g


def _gelu(x):
    return 0.5 * x * (1.0 + lax.erf(x * (0.5 ** 0.5)))


def _silu(x):
    return x * jax.nn.sigmoid(x)


def _head_pair_rms(x, g, head0):
    x2 = x * x
    s0 = jnp.sum(jnp.where(head0, x2, 0.0), axis=-1, keepdims=True)
    s1 = jnp.sum(jnp.where(head0, 0.0, x2), axis=-1, keepdims=True)
    ms = jnp.where(head0, s0, s1) * (1.0 / B_HEAD_DIM)
    return x * lax.rsqrt(ms + EPS) * g


def _norm_matmul_kernel(x_ref, g_ref, w_ref, ws_ref, o_ref, os_ref):
    for r in range(x_ref.shape[0] // ROW_SUB):
        rows = slice(r * ROW_SUB, (r + 1) * ROW_SUB)
        h = _rms_rows(x_ref[rows, :], g_ref[...]).astype(BF16)
        o_ref[rows, :] = jnp.dot(h, w_ref[:, :o_ref.shape[1]], preferred_element_type=F32)
        os_ref[rows, :] = jnp.dot(h, ws_ref[...], preferred_element_type=F32)


def _norm_matmul(x, g, w_all, li, f, w_side, *, tm):
    n, d = x.shape
    fs = w_side.shape[1]
    vmem = 2 * (tm * d * 4 + d * (w_all.shape[2] + fs) * 2 + tm * (f + fs) * 4) + 2 * ROW_SUB * f * 4
    return pl.pallas_call(
        _norm_matmul_kernel,
        grid=(n // tm,),
        in_specs=[pl.BlockSpec((tm, d), lambda i: (i, 0)),
                  pl.BlockSpec((1, d), lambda i: (0, 0)),
                  _layer_spec(w_all, li),
                  pl.BlockSpec((d, fs), lambda i: (0, 0))],
        out_specs=(pl.BlockSpec((tm, f), lambda i: (i, 0)), pl.BlockSpec((tm, fs), lambda i: (i, 0))),
        out_shape=(jax.ShapeDtypeStruct((n, f), F32), jax.ShapeDtypeStruct((n, fs), F32)),
        compiler_params=_params(("parallel",), vmem),
        name="norm_matmul",
    )(x, g.reshape(1, d), w_all, w_side)


def _even_in_kernel(x_ref, g_ref, w_ref, lng_ref, lnb_ref, ws_ref, bs_ref, qg_ref, kg_ref,
                    a_ref, qb_ref, kn_ref, vn_ref, qp_ref, kp_ref, vp_ref, h_ref, z_ref, stage_ref):
    tm = x_ref.shape[0]
    aw = a_ref.shape[1]
    bw = kn_ref.shape[1]
    tiles = bw // V7X_LANES
    groups = ROW_SUB // B_PLANES
    per_block = B_BLOCK // B_PLANES
    row = lax.broadcasted_iota(jnp.int32, (A_CHUNK, 2 * A_CHUNK), 0)
    col = lax.broadcasted_iota(jnp.int32, (A_CHUNK, 2 * A_CHUNK), 1)
    causal = (col % A_CHUNK) <= row
    lane2 = lax.broadcasted_iota(jnp.int32, (1, 2 * V7X_LANES), 1)
    first_group = (lane2 % V7X_LANES) < A_GROUP_DIM
    head0 = lax.broadcasted_iota(jnp.int32, (1, V7X_LANES), 1) < B_HEAD_DIM
    w_pairs = [jnp.where(causal, ws_ref[t], 0.0).astype(BF16) for t in range(ws_ref.shape[0])]
    for r in range(tm // ROW_SUB):
        base = r * ROW_SUB
        rows = slice(base, base + ROW_SUB)
        h, z, stage = h_ref.at[r % 2], z_ref.at[r % 2], stage_ref.at[r % 2]
        h[...] = _rms_rows(x_ref[rows, :], g_ref[...]).astype(BF16)
        z[...] = jnp.dot(h[...], w_ref[...], preferred_element_type=F32)

        v = _gelu(z[:, aw:2 * aw])
        vc = v - jnp.mean(v, axis=-1, keepdims=True)
        vn = vc * lax.rsqrt(jnp.mean(vc * vc, axis=-1, keepdims=True) + EPS) * lng_ref[...] + lnb_ref[...]
        for t in range(aw // V7X_LANES):
            lanes = slice(t * V7X_LANES, (t + 1) * V7X_LANES)
            cc = jnp.concatenate([vn[:A_CHUNK, lanes], vn[A_CHUNK:, lanes]], axis=1)
            rhs = jnp.concatenate([jnp.where(first_group, cc, 0.0),
                                   jnp.where(first_group, 0.0, cc)], axis=0).astype(BF16)
            mixed = jnp.dot(w_pairs[t], rhs, preferred_element_type=F32)
            bias = bs_ref[:, lanes]
            for c in range(2):
                chunk = slice(c * A_CHUNK, (c + 1) * A_CHUNK)
                u = _gelu(z[chunk, lanes])
                a_ref[base + c * A_CHUNK:base + (c + 1) * A_CHUNK, lanes] = (
                    u * (mixed[:, c * V7X_LANES:(c + 1) * V7X_LANES] + bias)).astype(BF16)

        for t in range(tiles):
            lanes = slice(t * V7X_LANES, (t + 1) * V7X_LANES)
            q = _head_pair_rms(z[:, 2 * aw + t * V7X_LANES:2 * aw + (t + 1) * V7X_LANES], qg_ref[...], head0)
            q = q * B_SCORE_SCALE
            k = _head_pair_rms(z[:, 2 * aw + bw + t * V7X_LANES:2 * aw + bw + (t + 1) * V7X_LANES],
                               kg_ref[...], head0)
            vb = z[:, 2 * aw + 2 * bw + t * V7X_LANES:2 * aw + 2 * bw + (t + 1) * V7X_LANES]
            kn_ref[rows, lanes] = k.astype(BF16)
            vn_ref[rows, lanes] = vb.astype(BF16)
            for i, val in enumerate((q, k, vb)):
                for grp in range(groups):
                    stage[i * tiles + t, grp * STAGE_PITCH:grp * STAGE_PITCH + B_PLANES, :] = (
                        val[grp * B_PLANES:(grp + 1) * B_PLANES])
            for p in range(B_PLANES):
                sel = pl.ds(p, groups, stride=STAGE_PITCH)
                for i, planes in enumerate((qp_ref, kp_ref, vp_ref)):
                    planes[p, r * groups:(r + 1) * groups, lanes] = stage[i * tiles + t, sel, :].astype(BF16)
            for blk in range(ROW_SUB // B_BLOCK):
                first = blk * per_block * STAGE_PITCH
                slabs = [stage[t, pl.ds(first + p, per_block, stride=STAGE_PITCH), :] for p in range(B_PLANES)]
                qb_ref[base + blk * B_BLOCK:base + (blk + 1) * B_BLOCK, lanes] = (
                    jnp.concatenate(slabs, axis=0).astype(BF16))


def _even_in(x, g, w_all, li, ln_g, ln_b, w_s, b_s, q_g, k_g, *, bsz, tm):
    n, d = x.shape
    seq = n // bsz
    aw = A_GROUPS * A_GROUP_DIM
    f = w_all.shape[2]
    bw = (f - 2 * aw) // 3
    pairs = A_GROUPS // 2
    tiles = seq // tm
    per_plane = tm // B_PLANES
    ws_pairs = w_s.reshape(pairs, 2, A_CHUNK, A_CHUNK).transpose(0, 2, 1, 3).reshape(pairs, A_CHUNK, 2 * A_CHUNK)
    bias = jnp.repeat(b_s.T, A_GROUP_DIM, axis=1)
    qg2 = jnp.tile(q_g, 2).reshape(1, V7X_LANES)
    kg2 = jnp.tile(k_g, 2).reshape(1, V7X_LANES)
    nat_spec = pl.BlockSpec((tm, bw), lambda i: (i, 0))
    plane_spec = pl.BlockSpec((None, B_PLANES, per_plane, bw), lambda i: (i // tiles, 0, i % tiles, 0))
    nat_shape = jax.ShapeDtypeStruct((n, bw), BF16)
    plane_shape = jax.ShapeDtypeStruct((bsz, B_PLANES, seq // B_PLANES, bw), BF16)
    stage_rows = ROW_SUB // B_PLANES * STAGE_PITCH
    vmem = (2 * (tm * d * 4 + d * f * 2 + tm * aw * 2 + 6 * tm * bw * 2)
            + 2 * (ROW_SUB * d * 2 + ROW_SUB * f * 4 + 3 * stage_rows * bw * 4)
            + 2 * ROW_SUB * (2 * aw + 3 * bw) * 4)
    return pl.pallas_call(
        _even_in_kernel,
        grid=(n // tm,),
        in_specs=[pl.BlockSpec((tm, d), lambda i: (i, 0)),
                  pl.BlockSpec((1, d), lambda i: (0, 0)),
                  _layer_spec(w_all, li),
                  pl.BlockSpec((1, aw), lambda i: (0, 0)),
                  pl.BlockSpec((1, aw), lambda i: (0, 0)),
                  pl.BlockSpec((pairs, A_CHUNK, 2 * A_CHUNK), lambda i: (0, 0, 0)),
                  pl.BlockSpec((A_CHUNK, aw), lambda i: (0, 0)),
                  pl.BlockSpec((1, V7X_LANES), lambda i: (0, 0)),
                  pl.BlockSpec((1, V7X_LANES), lambda i: (0, 0))],
        out_specs=(pl.BlockSpec((tm, aw), lambda i: (i, 0)), nat_spec, nat_spec, nat_spec,
                   plane_spec, plane_spec, plane_spec),
        out_shape=(jax.ShapeDtypeStruct((n, aw), BF16), nat_shape, nat_shape, nat_shape,
                   plane_shape, plane_shape, plane_shape),
        scratch_shapes=[pltpu.VMEM((2, ROW_SUB, d), BF16), pltpu.VMEM((2, ROW_SUB, f), F32),
                        pltpu.VMEM((2, 3 * (bw // V7X_LANES), stage_rows, V7X_LANES), F32)],
        compiler_params=_params(("parallel",), vmem),
        name="even_in",
    )(x, g.reshape(1, d), w_all, ln_g.reshape(1, aw), ln_b.reshape(1, aw), ws_pairs, bias, qg2, kg2)


def _dilated_kernel(qb_ref, kn_ref, vn_ref, qp_ref, kp_ref, vp_ref, *rest, n_cast):
    cast_src, o_ref, cast_dst = rest[:n_cast], rest[n_cast], rest[n_cast + 1:2 * n_cast + 1]
    ob_ref, mb_ref, db_ref, mask_ref = rest[2 * n_cast + 1:]
    for src, dst in zip(cast_src, cast_dst):
        dst[...] = src[...].astype(BF16)
    m_rows = qp_ref.shape[1]
    blk2 = 2 * B_BLOCK
    head0 = lax.broadcasted_iota(jnp.int32, (1, V7X_LANES), 1) < B_HEAD_DIM

    rowi = lax.broadcasted_iota(jnp.int32, (blk2, blk2), 0) % B_BLOCK
    coli = lax.broadcasted_iota(jnp.int32, (blk2, blk2), 1)
    is_cur = coli >= B_BLOCK
    colj = coli % B_BLOCK
    for bi, d in enumerate(B_DILATIONS):
        planes = B_PLANES // d
        mb = B_BLOCK // planes
        i_pos = (rowi % mb) * planes + rowi // mb
        j_pos = colj if d == 1 else (colj % mb) * planes + colj // mb
        band = jnp.where(jnp.where(is_cur, i_pos - j_pos, j_pos - i_pos) >= 0, 0.0, NEG)
        mask_ref[2 * bi] = jnp.where(is_cur, band, NEG)
        mask_ref[2 * bi + 1] = band

    ones = jnp.ones((blk2, V7X_LANES), BF16)

    def attend(qb, kcat, vcat, bias):
        zero = jnp.zeros_like(qb)
        q2 = jnp.concatenate([jnp.where(head0, qb, zero), jnp.where(head0, zero, qb)], axis=0)
        s = lax.dot_general(q2, kcat, NT_DIMS, preferred_element_type=F32) + bias
        m = jnp.max(s, axis=-1, keepdims=True)
        p = jnp.exp2(s - m).astype(BF16)
        pv = jnp.dot(p, jnp.concatenate([vcat, ones], axis=1), preferred_element_type=F32)
        top, bot = pv[:B_BLOCK], pv[B_BLOCK:]
        return (jnp.where(head0, top[:, :V7X_LANES], bot[:, :V7X_LANES]),
                jnp.where(head0, m[:B_BLOCK], m[B_BLOCK:]),
                jnp.where(head0, top[:, V7X_LANES:], bot[:, V7X_LANES:]))

    def token_block(n, carry):
        per_block = B_BLOCK // B_PLANES
        off = pl.multiple_of(n * B_BLOCK, B_BLOCK)
        off_prev = pl.multiple_of(jnp.maximum(n - 1, 0) * B_BLOCK, B_BLOCK)
        off_plane = pl.multiple_of(n * per_block, per_block)
        kcat = jnp.concatenate([kn_ref[pl.ds(off_prev, B_BLOCK), :], kn_ref[pl.ds(off, B_BLOCK), :]], axis=0)
        vcat = jnp.concatenate([vn_ref[pl.ds(off_prev, B_BLOCK), :], vn_ref[pl.ds(off, B_BLOCK), :]], axis=0)
        o, m, den = attend(qb_ref[pl.ds(off, B_BLOCK), :], kcat, vcat, mask_ref[jnp.minimum(n, 1)])
        for p in range(B_PLANES):
            for ref, val in ((ob_ref, o), (mb_ref, m), (db_ref, den)):
                ref[0, p, pl.ds(off_plane, per_block), :] = val[p * per_block:(p + 1) * per_block]
        return carry

    lax.fori_loop(0, kn_ref.shape[0] // B_BLOCK, token_block, 0, unroll=32)

    for bi, d in enumerate(B_DILATIONS):
        if d == 1:
            continue
        planes = B_PLANES // d
        mb = B_BLOCK // planes
        nb = m_rows // mb

        def plane_block(blk, carry, bi=bi, d=d, planes=planes, mb=mb, nb=nb):
            r = blk // nb
            n = blk % nb
            off = pl.multiple_of(n * mb, mb)
            off_prev = pl.multiple_of(jnp.maximum(n - 1, 0) * mb, mb)

            def gather(ref, offs):
                return jnp.concatenate([ref[r + d * a, pl.ds(o, mb), :] for o in offs for a in range(planes)],
                                       axis=0)

            o, m, den = attend(gather(qp_ref, (off,)), gather(kp_ref, (off_prev, off)),
                               gather(vp_ref, (off_prev, off)), mask_ref[2 * bi + jnp.minimum(n, 1)])
            for a in range(planes):
                for ref, val in ((ob_ref, o), (mb_ref, m), (db_ref, den)):
                    ref[bi, r + d * a, pl.ds(off, mb), :] = val[a * mb:(a + 1) * mb]
            return carry

        lax.fori_loop(0, d * nb, plane_block, 0, unroll=32)

    branches = range(len(B_DILATIONS))
    for r in range(B_PLANES):
        mx = functools.reduce(jnp.maximum, [mb_ref[bi, r] for bi in branches])
        es = [jnp.exp2(mb_ref[bi, r] - mx) for bi in branches]
        num = functools.reduce(lambda a, b: a + b, [es[bi] * ob_ref[bi, r] for bi in branches])
        den = functools.reduce(lambda a, b: a + b, [es[bi] * db_ref[bi, r] for bi in branches])
        o_ref[pl.ds(r, m_rows, stride=B_PLANES), :] = num / den


def _dilated(qb, kn, vn, qp, kp, vp, cast=()):
    bsz, s, bw = kn.shape
    pairs = bw // V7X_LANES
    cast_in, cast_out, cast_shape, cast_vmem = _cast_plumbing(cast, lambda b, p: b * pairs + p)
    m_rows = s // B_PLANES
    tile_f32 = s * V7X_LANES * 4
    vmem = (2 * (6 * tile_f32 // 2 + tile_f32) + 9 * tile_f32 + 6 * 4 * B_BLOCK * B_BLOCK * 4 + 8 * tile_f32 // 16
            + cast_vmem)
    nat_spec = pl.BlockSpec((None, s, V7X_LANES), lambda b, p: (b, 0, p))
    plane_spec = pl.BlockSpec((None, B_PLANES, m_rows, V7X_LANES), lambda b, p: (b, 0, 0, p))
    outs = pl.pallas_call(
        functools.partial(_dilated_kernel, n_cast=len(cast)),
        grid=(bsz, pairs),
        in_specs=[nat_spec, nat_spec, nat_spec, plane_spec, plane_spec, plane_spec] + cast_in,
        out_specs=[pl.BlockSpec((None, s, V7X_LANES), lambda b, p: (b, 0, p))] + cast_out,
        out_shape=[jax.ShapeDtypeStruct((bsz, s, bw), F32)] + cast_shape,
        scratch_shapes=[pltpu.VMEM((len(B_DILATIONS), B_PLANES, m_rows, V7X_LANES), F32)] * 3
                       + [pltpu.VMEM((2 * len(B_DILATIONS), 2 * B_BLOCK, 2 * B_BLOCK), F32)],
        compiler_params=_params(("parallel", "parallel"), vmem),
        name="dilated_attention",
    )(qb, kn, vn, qp, kp, vp, *[arr for arr, _, _ in cast])
    return outs[0], outs[1:]


def _gla_kernel(q_ref, k_ref, v_ref, r_ref, ga_ref, wa_ref, ba_ref, hg_ref, o_ref, st_ref):
    tc = q_ref.shape[0]
    chunks = tc // C_CHUNK

    @pl.when(pl.program_id(1) == 0)
    def _():
        st_ref[...] = jnp.zeros_like(st_ref)

    gate = jnp.dot(ga_ref[...].astype(BF16), wa_ref[...], preferred_element_type=F32) + ba_ref[...]
    log_a = jax.nn.log_sigmoid(gate) / C_TAU
    ci = lax.broadcasted_iota(jnp.int32, (C_CHUNK, C_CHUNK), 0)
    cj = lax.broadcasted_iota(jnp.int32, (C_CHUNK, C_CHUNK), 1)
    causal = ci >= cj
    tri = jnp.where(causal, 1.0, 0.0).astype(BF16)
    rows = [slice(c * C_CHUNK, (c + 1) * C_CHUNK) for c in range(chunks)]
    for hd in range(C_HEADS):
        kl = slice(hd * C_DK, (hd + 1) * C_DK)
        vl = slice(hd * C_DV, (hd + 1) * C_DV)
        la = log_a[:, kl]
        hi = la.astype(BF16)
        rest = la - hi.astype(F32)
        mid = rest.astype(BF16)
        lo = (rest - mid.astype(F32)).astype(BF16)
        pieces = jnp.concatenate([hi, mid, lo], axis=1)
        sums = [jnp.dot(tri, pieces[sl], preferred_element_type=F32) for sl in rows]
        b = jnp.concatenate([s3[:, :C_DK] + s3[:, C_DK:2 * C_DK] + s3[:, 2 * C_DK:] for s3 in sums], axis=0)
        b3 = b.reshape(chunks, C_CHUNK, C_DK)
        b_last = b3[:, C_CHUNK - 1:C_CHUNK, :]
        k = k_ref[:, kl]
        q_t = ((q_ref[:, kl] * (C_DK ** -0.5)) * jnp.exp(b)).astype(BF16)
        k_t = (k * jnp.exp(-b)).astype(BF16)
        k_s = (k.reshape(chunks, C_CHUNK, C_DK) * jnp.exp(b_last - b3)).reshape(tc, C_DK).astype(BF16)
        decay = jnp.exp(b_last)
        o_intra, kv_t = [], []
        for sl in rows:
            v_c = v_ref[sl, vl].astype(BF16)
            attn = lax.dot_general(q_t[sl], k_t[sl], NT_DIMS, preferred_element_type=F32)
            attn = jnp.where(causal, attn, 0.0).astype(BF16)
            o_intra.append(jnp.dot(attn, v_c, preferred_element_type=F32))
            kv_t.append(lax.dot_general(v_c, k_s[sl], TN_DIMS, preferred_element_type=F32))
        st = st_ref[hd]
        entering = []
        for c in range(chunks):
            entering.append(st.astype(BF16))
            st = st * decay[c] + kv_t[c]
        st_ref[hd] = st
        for c, sl in enumerate(rows):
            o = o_intra[c] + lax.dot_general(q_t[sl], entering[c], NT_DIMS, preferred_element_type=F32)
            o_ref[sl, vl] = (_rms_rows(o, hg_ref[...]) * _silu(r_ref[sl, vl])).astype(BF16)


def _gla(z3, ga3, w_a2, b_a, head_g, *, tc):
    bsz, s, _ = z3.shape
    hk, hv = C_HEADS * C_DK, C_HEADS * C_DV
    wa = jnp.pad(w_a2, ((0, V7X_LANES - C_GATE_RANK), (0, 0))).astype(BF16)
    vmem = (2 * (2 * tc * hk * 4 + 3 * tc * hv * 4 + tc * V7X_LANES * 4 + V7X_LANES * hk * 2)
            + hv * C_DK * 4 + 12 * tc * C_DK * 4 + 2 * tc * hk * 4)
    return pl.pallas_call(
        _gla_kernel,
        grid=(bsz, s // tc),
        in_specs=[pl.BlockSpec((None, tc, hk), lambda b, t: (b, t, 0)),
                  pl.BlockSpec((None, tc, hk), lambda b, t: (b, t, 1)),
                  pl.BlockSpec((None, tc, hv), lambda b, t: (b, t, 1)),
                  pl.BlockSpec((None, tc, hv), lambda b, t: (b, t, 2)),
                  pl.BlockSpec((None, tc, V7X_LANES), lambda b, t: (b, t, 0)),
                  pl.BlockSpec((V7X_LANES, hk), lambda b, t: (0, 0)),
                  pl.BlockSpec((1, hk), lambda b, t: (0, 0)),
                  pl.BlockSpec((1, C_DV), lambda b, t: (0, 0))],
        out_specs=pl.BlockSpec((None, tc, hv), lambda b, t: (b, t, 0)),
        out_shape=jax.ShapeDtypeStruct((bsz, s, hv), BF16),
        scratch_shapes=[pltpu.VMEM((C_HEADS, C_DV, C_DK), F32)],
        compiler_params=_params(("parallel", "arbitrary"), vmem),
        name="gla",
    )(z3, z3, z3, z3, ga3, wa, b_a.reshape(1, -1), head_g.reshape(1, C_DV))


def _ffn_kernel(x_ref, a_ref, b_ref, wa_ref, wb_ref, g_ref, wg_ref, wu_ref, cw_ref, cb_ref, wd_ref, *rest,
                tiles_per_seq, tf, n_cast):
    cast_src, o_ref, cast_dst = rest[:n_cast], rest[n_cast], rest[n_cast + 1:2 * n_cast + 1]
    h_ref, act_ref, gbuf_ref, halo_ref = rest[2 * n_cast + 1:]
    tm = x_ref.shape[0]
    halo_rows = V7X_SUBLANES
    for r in range(tm // ROW_SUB):
        rows = slice(r * ROW_SUB, (r + 1) * ROW_SUB)
        x1 = x_ref[rows, :] + jnp.dot(a_ref[rows, :].astype(BF16), wa_ref[...], preferred_element_type=F32)
        x1 = x1 + jnp.dot(b_ref[rows, :].astype(BF16), wb_ref[...], preferred_element_type=F32)
        o_ref[rows, :] = x1
        h_ref[rows, :] = _rms_rows(x1, g_ref[...]).astype(BF16)

    @pl.when(pl.program_id(0) % tiles_per_seq == 0)
    def _():
        halo_ref[...] = jnp.zeros_like(halo_ref)

    for j in range(wg_ref.shape[1] // tf):
        cols = slice(j * tf, (j + 1) * tf)
        h = h_ref[...]
        gate = jnp.dot(h, wg_ref[:, cols], preferred_element_type=F32)
        up = jnp.dot(h, wu_ref[:, cols], preferred_element_type=F32)
        gbuf = gbuf_ref.at[j % 2]
        gbuf[0:halo_rows, :] = halo_ref[:, cols]
        gbuf[halo_rows:, :] = gate
        halo_ref[:, cols] = gate[tm - halo_rows:, :]
        conv = cb_ref[:, cols]
        for tap in range(CONV_W - 1):
            conv = conv + gbuf[pl.ds(halo_rows - (CONV_W - 1) + tap, tm), :] * cw_ref[tap:tap + 1, cols]
        conv = conv + gate * cw_ref[CONV_W - 1:CONV_W, cols]
        act_ref[:, cols] = (_silu(conv) * up).astype(BF16)
    o_ref[...] += jnp.dot(act_ref[...], wd_ref[...], preferred_element_type=F32)
    for src, dst in zip(cast_src, cast_dst):
        dst[...] = src[...].astype(BF16)


def _ffn(x, a, a_blk, b, b_blk, w_out_all, lo, g, w_gate_all, w_up_all, conv_w, conv_b, w_down_all, lf,
         cast=(), *, seq, tm, tf):
    n, d = x.shape
    dff = w_gate_all.shape[2]
    kh = w_out_all.shape[1] // 2
    cast_in, cast_out, cast_shape, cast_vmem = _cast_plumbing(cast, lambda i: i)
    vmem = (2 * (2 * tm * d * 4 + 2 * tm * kh * 4 + 2 * kh * d * 2 + 3 * d * dff * 2 + 4 * dff * 4)
            + tm * d * 2 + tm * dff * 2 + 2 * (tm + 8) * tf * 4 + 8 * dff * 4 + 8 * tm * tf * 4 + cast_vmem)
    outs = pl.pallas_call(
        functools.partial(_ffn_kernel, tiles_per_seq=seq // tm, tf=tf, n_cast=len(cast)),
        grid=(n // tm,),
        in_specs=[pl.BlockSpec((tm, d), lambda i: (i, 0)),
                  pl.BlockSpec((tm, kh), lambda i: (i, a_blk)),
                  pl.BlockSpec((tm, kh), lambda i: (i, b_blk)),
                  pl.BlockSpec((None, kh, d), lambda i: (lo, 0, 0)),
                  pl.BlockSpec((None, kh, d), lambda i: (lo, 1, 0)),
                  pl.BlockSpec((1, d), lambda i: (0, 0)),
                  _layer_spec(w_gate_all, lf),
                  _layer_spec(w_up_all, lf),
                  pl.BlockSpec((CONV_W, dff), lambda i: (0, 0)),
                  pl.BlockSpec((1, dff), lambda i: (0, 0)),
                  _layer_spec(w_down_all, lf)] + cast_in,
        out_specs=[pl.BlockSpec((tm, d), lambda i: (i, 0))] + cast_out,
        out_shape=[jax.ShapeDtypeStruct((n, d), F32)] + cast_shape,
        scratch_shapes=[pltpu.VMEM((tm, d), BF16), pltpu.VMEM((tm, dff), BF16),
                        pltpu.VMEM((2, V7X_SUBLANES + tm, tf), F32),
                        pltpu.VMEM((V7X_SUBLANES, dff), F32)],
        compiler_params=_params(("arbitrary",), vmem),
        name="conv_ffn",
    )(x, a, b, w_out_all, w_out_all, g.reshape(1, d), w_gate_all, w_up_all, conv_w, conv_b.reshape(1, dff),
      w_down_all, *[arr for arr, _, _ in cast])
    return outs[0], outs[1:]


def kernel(x, norm_mix_g, norm_ffn_g, ev_w_in, ev_a_ln_g, ev_a_ln_b, ev_a_ws, ev_a_bs, ev_q_g, ev_k_g,
           ev_w_out, od_w_in, od_w_a2, od_b_a, od_head_g, od_w_out, ffn_w_gate, ffn_w_up, ffn_conv_w,
           ffn_conv_b, ffn_w_down):
    bsz, seq, d = x.shape
    n = bsz * seq
    depth = norm_mix_g.shape[0]
    main = 2 * C_HEADS * (C_DK + C_DV)
    ev_w_out, od_w_out = ev_w_out.astype(BF16), od_w_out.astype(BF16)
    od_w_side = jnp.pad(od_w_in[:, :, main:], ((0, 0), (0, 0), (0, V7X_LANES - C_GATE_RANK))).astype(BF16)
    ffn_f32 = (ffn_w_gate, ffn_w_up, ffn_w_down)

    def cast_for(layer, steps):
        if layer >= depth:
            return []
        items = [(w, layer) for w in ffn_f32]
        if layer > 0:
            items.append((ev_w_in, layer // 2) if layer % 2 == 0 else (od_w_in, layer // 2))
        return [(arr, idx, _cast_slabs(arr, steps)) for arr, idx in items]

    w_in = ev_w_in[:1].astype(BF16)
    ffn_w = None
    xf = x.reshape(n, d)
    for layer in range(depth):
        if layer % 2 == 0:
            a_out, qb, kn, vn, qp, kp, vp = _even_in(
                xf, norm_mix_g[layer], w_in, 0, ev_a_ln_g[layer // 2], ev_a_ln_b[layer // 2], ev_a_ws[layer // 2],
                ev_a_bs[layer // 2], ev_q_g[layer // 2], ev_k_g[layer // 2], bsz=bsz, tm=1024)
            in_seq = lambda t: t.reshape(bsz, seq, -1)
            first_cast = cast_for(layer, bsz * (qb.shape[1] // V7X_LANES)) if layer == 0 else []
            b_out, cast_out = _dilated(in_seq(qb), in_seq(kn), in_seq(vn), qp, kp, vp, first_cast)
            if layer == 0:
                ffn_w = cast_out
            mix = (a_out, 0, b_out.reshape(n, -1), 0, ev_w_out, layer // 2)
        else:
            z, ga = _norm_matmul(xf, norm_mix_g[layer], w_in, 0, main, od_w_side[layer // 2], tm=512)
            mixed = _gla(z.reshape(bsz, seq, -1), ga.reshape(bsz, seq, -1), od_w_a2[layer // 2], od_b_a[layer // 2],
                         od_head_g[layer // 2], tc=512).reshape(n, -1)
            mix = (mixed, 0, mixed, 1, od_w_out, layer // 2)
        xf, cast_out = _ffn(xf, *mix, norm_ffn_g[layer], ffn_w[0], ffn_w[1], ffn_conv_w[layer], ffn_conv_b[layer],
                            ffn_w[2], 0, cast_for(layer + 1, n // FFN_ROWS), seq=seq, tm=FFN_ROWS, tf=256)
        if layer + 1 < depth:
            ffn_w, w_in = cast_out[:3], cast_out[3]
    return xf.reshape(bsz, seq, d)
```

```python
import functools

import jax
import jax.numpy as jnp
from jax import lax
from jax.experimental import pallas as pl
from jax.experimental.pallas import tpu as pltpu

A_GROUPS = 8
A_GROUP_DIM = 64
A_CHUNK = 128
B_HEAD_DIM = 64
B_DILATIONS = (1, 4, 16)
B_BLOCK = 128
C_HEADS = 4
C_DK = 128
C_DV = 256
C_GATE_RANK = 16
C_TAU = 16.0
C_CHUNK = 64
CONV_W = 3
EPS = 1e-6
NEG = -1e30

V7X_LANES = 128
V7X_SUBLANES = 8
V7X_VMEM_BUDGET = 56 * 1024 * 1024
BF16_ROWS = 16

EVEN_IN_ROWS = 1024
ODD_IN_ROWS = 512
GLA_ROWS = 512
FFN_ROWS = 512
FFN_COLS = 256

F32 = jnp.float32
BF16 = jnp.bfloat16
NT_DIMS = (((1,), (1,)), ((), ()))
TN_DIMS = (((0,), (0,)), ((), ()))

ROW_SUB = 2 * A_CHUNK
B_PLANES = max(B_DILATIONS)
STAGE_PITCH = 24
B_SCORE_SCALE = B_HEAD_DIM ** -0.5 * 1.4426950408889634


def _params(semantics, vmem_bytes):
    return pltpu.CompilerParams(
        dimension_semantics=semantics,
        vmem_limit_bytes=min(int(vmem_bytes * 1.25) + (4 << 20), V7X_VMEM_BUDGET))


def _layer_spec(w_all, li):
    return pl.BlockSpec((None,) + w_all.shape[1:], lambda i: (li, 0, 0))


def _cast_slabs(arr, steps):
    return steps if arr.shape[1] // steps % BF16_ROWS == 0 else steps // 2


def _cast_plumbing(cast, linear_step):
    ins, outs, shapes, vmem = [], [], [], 0
    for arr, layer, slabs in cast:
        slab = (None, arr.shape[1] // slabs, arr.shape[2])
        ins.append(pl.BlockSpec(
            slab, lambda *g, layer=layer, last=slabs - 1: (layer, jnp.minimum(linear_step(*g), last), 0)))
        outs.append(pl.BlockSpec(slab, lambda *g, last=slabs - 1: (0, jnp.minimum(linear_step(*g), last), 0)))
        shapes.append(jax.ShapeDtypeStruct((1,) + arr.shape[1:], BF16))
        vmem += 2 * slab[1] * slab[2] * 6
    return ins, outs, shapes, vmem


def _rms_rows(x, g):
    return x * lax.rsqrt(jnp.mean(x * x, axis=-1, keepdims=True) + EPS) * g


def _gelu(x):
    return 0.5 * x * (1.0 + lax.erf(x * (0.5 ** 0.5)))


def _silu(x):
    return x * jax.nn.sigmoid(x)


def _head_pair_rms(x, g, head0):
    x2 = x * x
    s0 = jnp.sum(jnp.where(head0, x2, 0.0), axis=-1, keepdims=True)
    s1 = jnp.sum(jnp.where(head0, 0.0, x2), axis=-1, keepdims=True)
    ms = jnp.where(head0, s0, s1) * (1.0 / B_HEAD_DIM)
    return x * lax.rsqrt(ms + EPS) * g


def _norm_matmul_kernel(x_ref, g_ref, w_ref, ws_ref, o_ref, os_ref):
    for r in range(x_ref.shape[0] // ROW_SUB):
        rows = slice(r * ROW_SUB, (r + 1) * ROW_SUB)
        h = _rms_rows(x_ref[rows, :], g_ref[...]).astype(BF16)
        o_ref[rows, :] = jnp.dot(h, w_ref[:, :o_ref.shape[1]], preferred_element_type=F32)
        os_ref[rows, :] = jnp.dot(h, ws_ref[...], preferred_element_type=F32)


def _norm_matmul(x, g, w_all, li, f, w_side, *, tm):
    n, d = x.shape
    fs = w_side.shape[1]
    vmem = 2 * (tm * d * 4 + d * (w_all.shape[2] + fs) * 2 + tm * (f + fs) * 4) + 2 * ROW_SUB * f * 4
    return pl.pallas_call(
        _norm_matmul_kernel,
        grid=(n // tm,),
        in_specs=[pl.BlockSpec((tm, d), lambda i: (i, 0)),
                  pl.BlockSpec((1, d), lambda i: (0, 0)),
                  _layer_spec(w_all, li),
                  pl.BlockSpec((d, fs), lambda i: (0, 0))],
        out_specs=(pl.BlockSpec((tm, f), lambda i: (i, 0)), pl.BlockSpec((tm, fs), lambda i: (i, 0))),
        out_shape=(jax.ShapeDtypeStruct((n, f), F32), jax.ShapeDtypeStruct((n, fs), F32)),
        compiler_params=_params(("parallel",), vmem),
        name="norm_matmul",
    )(x, g.reshape(1, d), w_all, w_side)


def _even_in_kernel(x_ref, g_ref, w_ref, lng_ref, lnb_ref, ws_ref, bs_ref, qg_ref, kg_ref,
                    a_ref, qb_ref, kn_ref, vn_ref, qp_ref, kp_ref, vp_ref, h_ref, z_ref, stage_ref):
    tm = x_ref.shape[0]
    aw = a_ref.shape[1]
    bw = kn_ref.shape[1]
    tiles = bw // V7X_LANES
    groups = ROW_SUB // B_PLANES
    per_block = B_BLOCK // B_PLANES
    row = lax.broadcasted_iota(jnp.int32, (A_CHUNK, 2 * A_CHUNK), 0)
    col = lax.broadcasted_iota(jnp.int32, (A_CHUNK, 2 * A_CHUNK), 1)
    causal = (col % A_CHUNK) <= row
    lane2 = lax.broadcasted_iota(jnp.int32, (1, 2 * V7X_LANES), 1)
    first_group = (lane2 % V7X_LANES) < A_GROUP_DIM
    head0 = lax.broadcasted_iota(jnp.int32, (1, V7X_LANES), 1) < B_HEAD_DIM
    w_pairs = [jnp.where(causal, ws_ref[t], 0.0).astype(BF16) for t in range(ws_ref.shape[0])]
    for r in range(tm // ROW_SUB):
        base = r * ROW_SUB
        rows = slice(base, base + ROW_SUB)
        h, z, stage = h_ref.at[r % 2], z_ref.at[r % 2], stage_ref.at[r % 2]
        h[...] = _rms_rows(x_ref[rows, :], g_ref[...]).astype(BF16)
        z[...] = jnp.dot(h[...], w_ref[...], preferred_element_type=F32)

        v = _gelu(z[:, aw:2 * aw])
        vc = v - jnp.mean(v, axis=-1, keepdims=True)
        vn = vc * lax.rsqrt(jnp.mean(vc * vc, axis=-1, keepdims=True) + EPS) * lng_ref[...] + lnb_ref[...]
        for t in range(aw // V7X_LANES):
            lanes = slice(t * V7X_LANES, (t + 1) * V7X_LANES)
            cc = jnp.concatenate([vn[:A_CHUNK, lanes], vn[A_CHUNK:, lanes]], axis=1)
            rhs = jnp.concatenate([jnp.where(first_group, cc, 0.0),
                                   jnp.where(first_group, 0.0, cc)], axis=0).astype(BF16)
            mixed = jnp.dot(w_pairs[t], rhs, preferred_element_type=F32)
            bias = bs_ref[:, lanes]
            for c in range(2):
                chunk = slice(c * A_CHUNK, (c + 1) * A_CHUNK)
                u = _gelu(z[chunk, lanes])
                a_ref[base + c * A_CHUNK:base + (c + 1) * A_CHUNK, lanes] = (
                    u * (mixed[:, c * V7X_LANES:(c + 1) * V7X_LANES] + bias)).astype(BF16)

        for t in range(tiles):
            lanes = slice(t * V7X_LANES, (t + 1) * V7X_LANES)
            q = _head_pair_rms(z[:, 2 * aw + t * V7X_LANES:2 * aw + (t + 1) * V7X_LANES], qg_ref[...], head0)
            q = q * B_SCORE_SCALE
            k = _head_pair_rms(z[:, 2 * aw + bw + t * V7X_LANES:2 * aw + bw + (t + 1) * V7X_LANES],
                               kg_ref[...], head0)
            vb = z[:, 2 * aw + 2 * bw + t * V7X_LANES:2 * aw + 2 * bw + (t + 1) * V7X_LANES]
            kn_ref[rows, lanes] = k.astype(BF16)
            vn_ref[rows, lanes] = vb.astype(BF16)
            for i, val in enumerate((q, k, vb)):
                for grp in range(groups):
                    stage[i * tiles + t, grp * STAGE_PITCH:grp * STAGE_PITCH + B_PLANES, :] = (
                        val[grp * B_PLANES:(grp + 1) * B_PLANES])
            for p in range(B_PLANES):
                sel = pl.ds(p, groups, stride=STAGE_PITCH)
                for i, planes in enumerate((qp_ref, kp_ref, vp_ref)):
                    planes[p, r * groups:(r + 1) * groups, lanes] = stage[i * tiles + t, sel, :].astype(BF16)
            for blk in range(ROW_SUB // B_BLOCK):
                first = blk * per_block * STAGE_PITCH
                slabs = [stage[t, pl.ds(first + p, per_block, stride=STAGE_PITCH), :] for p in range(B_PLANES)]
                qb_ref[base + blk * B_BLOCK:base + (blk + 1) * B_BLOCK, lanes] = (
                    jnp.concatenate(slabs, axis=0).astype(BF16))


def _even_in(x, g, w_all, li, ln_g, ln_b, w_s, b_s, q_g, k_g, *, bsz, tm):
    n, d = x.shape
    seq = n // bsz
    aw = A_GROUPS * A_GROUP_DIM
    f = w_all.shape[2]
    bw = (f - 2 * aw) // 3
    pairs = A_GROUPS // 2
    tiles = seq // tm
    per_plane = tm // B_PLANES
    ws_pairs = w_s.reshape(pairs, 2, A_CHUNK, A_CHUNK).transpose(0, 2, 1, 3).reshape(pairs, A_CHUNK, 2 * A_CHUNK)
    bias = jnp.repeat(b_s.T, A_GROUP_DIM, axis=1)
    qg2 = jnp.tile(q_g, 2).reshape(1, V7X_LANES)
    kg2 = jnp.tile(k_g, 2).reshape(1, V7X_LANES)
    nat_spec = pl.BlockSpec((tm, bw), lambda i: (i, 0))
    plane_spec = pl.BlockSpec((None, B_PLANES, per_plane, bw), lambda i: (i // tiles, 0, i % tiles, 0))
    nat_shape = jax.ShapeDtypeStruct((n, bw), BF16)
    plane_shape = jax.ShapeDtypeStruct((bsz, B_PLANES, seq // B_PLANES, bw), BF16)
    stage_rows = ROW_SUB // B_PLANES * STAGE_PITCH
    vmem = (2 * (tm * d * 4 + d * f * 2 + tm * aw * 2 + 6 * tm * bw * 2)
            + 2 * (ROW_SUB * d * 2 + ROW_SUB * f * 4 + 3 * stage_rows * bw * 4)
            + 2 * ROW_SUB * (2 * aw + 3 * bw) * 4)
    return pl.pallas_call(
        _even_in_kernel,
        grid=(n // tm,),
        in_specs=[pl.BlockSpec((tm, d), lambda i: (i, 0)),
                  pl.BlockSpec((1, d), lambda i: (0, 0)),
                  _layer_spec(w_all, li),
                  pl.BlockSpec((1, aw), lambda i: (0, 0)),
                  pl.BlockSpec((1, aw), lambda i: (0, 0)),
                  pl.BlockSpec((pairs, A_CHUNK, 2 * A_CHUNK), lambda i: (0, 0, 0)),
                  pl.BlockSpec((A_CHUNK, aw), lambda i: (0, 0)),
                  pl.BlockSpec((1, V7X_LANES), lambda i: (0, 0)),
                  pl.BlockSpec((1, V7X_LANES), lambda i: (0, 0))],
        out_specs=(pl.BlockSpec((tm, aw), lambda i: (i, 0)), nat_spec, nat_spec, nat_spec,
                   plane_spec, plane_spec, plane_spec),
        out_shape=(jax.ShapeDtypeStruct((n, aw), BF16), nat_shape, nat_shape, nat_shape,
                   plane_shape, plane_shape, plane_shape),
        scratch_shapes=[pltpu.VMEM((2, ROW_SUB, d), BF16), pltpu.VMEM((2, ROW_SUB, f), F32),
                        pltpu.VMEM((2, 3 * (bw // V7X_LANES), stage_rows, V7X_LANES), F32)],
        compiler_params=_params(("parallel",), vmem),
        name="even_in",
    )(x, g.reshape(1, d), w_all, ln_g.reshape(1, aw), ln_b.reshape(1, aw), ws_pairs, bias, qg2, kg2)


def _dilated_kernel(qb_ref, kn_ref, vn_ref, qp_ref, kp_ref, vp_ref, *rest, n_cast):
    cast_src, o_ref, cast_dst = rest[:n_cast], rest[n_cast], rest[n_cast + 1:2 * n_cast + 1]
    ob_ref, mb_ref, db_ref, mask_ref = rest[2 * n_cast + 1:]
    for src, dst in zip(cast_src, cast_dst):
        dst[...] = src[...].astype(BF16)
    m_rows = qp_ref.shape[1]
    blk2 = 2 * B_BLOCK
    head0 = lax.broadcasted_iota(jnp.int32, (1, V7X_LANES), 1) < B_HEAD_DIM

    rowi = lax.broadcasted_iota(jnp.int32, (blk2, blk2), 0) % B_BLOCK
    coli = lax.broadcasted_iota(jnp.int32, (blk2, blk2), 1)
    is_cur = coli >= B_BLOCK
    colj = coli % B_BLOCK
    for bi, d in enumerate(B_DILATIONS):
        planes = B_PLANES // d
        mb = B_BLOCK // planes
        i_pos = (rowi % mb) * planes + rowi // mb
        j_pos = colj if d == 1 else (colj % mb) * planes + colj // mb
        band = jnp.where(jnp.where(is_cur, i_pos - j_pos, j_pos - i_pos) >= 0, 0.0, NEG)
        mask_ref[2 * bi] = jnp.where(is_cur, band, NEG)
        mask_ref[2 * bi + 1] = band

    ones = jnp.ones((blk2, V7X_LANES), BF16)

    def attend(qb, kcat, vcat, bias):
        zero = jnp.zeros_like(qb)
        q2 = jnp.concatenate([jnp.where(head0, qb, zero), jnp.where(head0, zero, qb)], axis=0)
        s = lax.dot_general(q2, kcat, NT_DIMS, preferred_element_type=F32) + bias
        m = jnp.max(s, axis=-1, keepdims=True)
        p = jnp.exp2(s - m).astype(BF16)
        pv = jnp.dot(p, jnp.concatenate([vcat, ones], axis=1), preferred_element_type=F32)
        top, bot = pv[:B_BLOCK], pv[B_BLOCK:]
        return (jnp.where(head0, top[:, :V7X_LANES], bot[:, :V7X_LANES]),
                jnp.where(head0, m[:B_BLOCK], m[B_BLOCK:]),
                jnp.where(head0, top[:, V7X_LANES:], bot[:, V7X_LANES:]))

    def token_block(n, carry):
        per_block = B_BLOCK // B_PLANES
        off = pl.multiple_of(n * B_BLOCK, B_BLOCK)
        off_prev = pl.multiple_of(jnp.maximum(n - 1, 0) * B_BLOCK, B_BLOCK)
        off_plane = pl.multiple_of(n * per_block, per_block)
        kcat = jnp.concatenate([kn_ref[pl.ds(off_prev, B_BLOCK), :], kn_ref[pl.ds(off, B_BLOCK), :]], axis=0)
        vcat = jnp.concatenate([vn_ref[pl.ds(off_prev, B_BLOCK), :], vn_ref[pl.ds(off, B_BLOCK), :]], axis=0)
        o, m, den = attend(qb_ref[pl.ds(off, B_BLOCK), :], kcat, vcat, mask_ref[jnp.minimum(n, 1)])
        for p in range(B_PLANES):
            for ref, val in ((ob_ref, o), (mb_ref, m), (db_ref, den)):
                ref[0, p, pl.ds(off_plane, per_block), :] = val[p * per_block:(p + 1) * per_block]
        return carry

    lax.fori_loop(0, kn_ref.shape[0] // B_BLOCK, token_block, 0, unroll=True)

    for bi, d in enumerate(B_DILATIONS):
        if d == 1:
            continue
        planes = B_PLANES // d
        mb = B_BLOCK // planes
        nb = m_rows // mb

        def plane_block(blk, carry, bi=bi, d=d, planes=planes, mb=mb, nb=nb):
            r = blk // nb
            n = blk % nb
            off = pl.multiple_of(n * mb, mb)
            off_prev = pl.multiple_of(jnp.maximum(n - 1, 0) * mb, mb)

            def gather(ref, offs):
                return jnp.concatenate([ref[r + d * a, pl.ds(o, mb), :] for o in offs for a in range(planes)],
                                       axis=0)

            o, m, den = attend(gather(qp_ref, (off,)), gather(kp_ref, (off_prev, off)),
                               gather(vp_ref, (off_prev, off)), mask_ref[2 * bi + jnp.minimum(n, 1)])
            for a in range(planes):
                for ref, val in ((ob_ref, o), (mb_ref, m), (db_ref, den)):
                    ref[bi, r + d * a, pl.ds(off, mb), :] = val[a * mb:(a + 1) * mb]
            return carry

        lax.fori_loop(0, d * nb, plane_block, 0, unroll=True)

    branches = range(len(B_DILATIONS))
    for r in range(B_PLANES):
        mx = functools.reduce(jnp.maximum, [mb_ref[bi, r] for bi in branches])
        es = [jnp.exp2(mb_ref[bi, r] - mx) for bi in branches]
        num = functools.reduce(lambda a, b: a + b, [es[bi] * ob_ref[bi, r] for bi in branches])
        den = functools.reduce(lambda a, b: a + b, [es[bi] * db_ref[bi, r] for bi in branches])
        o_ref[pl.ds(r, m_rows, stride=B_PLANES), :] = num / den


def _dilated(qb, kn, vn, qp, kp, vp, cast=()):
    bsz, s, bw = kn.shape
    pairs = bw // V7X_LANES
    cast_in, cast_out, cast_shape, cast_vmem = _cast_plumbing(cast, lambda b, p: b * pairs + p)
    m_rows = s // B_PLANES
    tile_f32 = s * V7X_LANES * 4
    vmem = (2 * (6 * tile_f32 // 2 + tile_f32) + 9 * tile_f32 + 6 * 4 * B_BLOCK * B_BLOCK * 4 + 8 * tile_f32 // 16
            + cast_vmem)
    nat_spec = pl.BlockSpec((None, s, V7X_LANES), lambda b, p: (b, 0, p))
    plane_spec = pl.BlockSpec((None, B_PLANES, m_rows, V7X_LANES), lambda b, p: (b, 0, 0, p))
    outs = pl.pallas_call(
        functools.partial(_dilated_kernel, n_cast=len(cast)),
        grid=(bsz, pairs),
        in_specs=[nat_spec, nat_spec, nat_spec, plane_spec, plane_spec, plane_spec] + cast_in,
        out_specs=[pl.BlockSpec((None, s, V7X_LANES), lambda b, p: (b, 0, p))] + cast_out,
        out_shape=[jax.ShapeDtypeStruct((bsz, s, bw), F32)] + cast_shape,
        scratch_shapes=[pltpu.VMEM((len(B_DILATIONS), B_PLANES, m_rows, V7X_LANES), F32)] * 3
                       + [pltpu.VMEM((2 * len(B_DILATIONS), 2 * B_BLOCK, 2 * B_BLOCK), F32)],
        compiler_params=_params(("parallel", "parallel"), vmem),
        name="dilated_attention",
    )(qb, kn, vn, qp, kp, vp, *[arr for arr, _, _ in cast])
    return outs[0], outs[1:]


def _gla_kernel(q_ref, k_ref, v_ref, r_ref, ga_ref, wa_ref, ba_ref, hg_ref, o_ref, st_ref):
    tc = q_ref.shape[0]
    chunks = tc // C_CHUNK

    @pl.when(pl.program_id(1) == 0)
    def _():
        st_ref[...] = jnp.zeros_like(st_ref)

    gate = jnp.dot(ga_ref[...].astype(BF16), wa_ref[...], preferred_element_type=F32) + ba_ref[...]
    log_a = jax.nn.log_sigmoid(gate) / C_TAU
    ci = lax.broadcasted_iota(jnp.int32, (C_CHUNK, C_CHUNK), 0)
    cj = lax.broadcasted_iota(jnp.int32, (C_CHUNK, C_CHUNK), 1)
    causal = ci >= cj
    tri = jnp.where(causal, 1.0, 0.0).astype(BF16)
    rows = [slice(c * C_CHUNK, (c + 1) * C_CHUNK) for c in range(chunks)]
    for hd in range(C_HEADS):
        kl = slice(hd * C_DK, (hd + 1) * C_DK)
        vl = slice(hd * C_DV, (hd + 1) * C_DV)
        la = log_a[:, kl]
        hi = la.astype(BF16)
        rest = la - hi.astype(F32)
        mid = rest.astype(BF16)
        lo = (rest - mid.astype(F32)).astype(BF16)
        pieces = jnp.concatenate([hi, mid, lo], axis=1)
        sums = [jnp.dot(tri, pieces[sl], preferred_element_type=F32) for sl in rows]
        b = jnp.concatenate([s3[:, :C_DK] + s3[:, C_DK:2 * C_DK] + s3[:, 2 * C_DK:] for s3 in sums], axis=0)
        b3 = b.reshape(chunks, C_CHUNK, C_DK)
        b_last = b3[:, C_CHUNK - 1:C_CHUNK, :]
        k = k_ref[:, kl]
        q_t = ((q_ref[:, kl] * (C_DK ** -0.5)) * jnp.exp(b)).astype(BF16)
        k_t = (k * jnp.exp(-b)).astype(BF16)
        k_s = (k.reshape(chunks, C_CHUNK, C_DK) * jnp.exp(b_last - b3)).reshape(tc, C_DK).astype(BF16)
        decay = jnp.exp(b_last)
        o_intra, kv_t = [], []
        for sl in rows:
            v_c = v_ref[sl, vl].astype(BF16)
            attn = lax.dot_general(q_t[sl], k_t[sl], NT_DIMS, preferred_element_type=F32)
            attn = jnp.where(causal, attn, 0.0).astype(BF16)
            o_intra.append(jnp.dot(attn, v_c, preferred_element_type=F32))
            kv_t.append(lax.dot_general(v_c, k_s[sl], TN_DIMS, preferred_element_type=F32))
        st = st_ref[hd]
        entering = []
        for c in range(chunks):
            entering.append(st.astype(BF16))
            st = st * decay[c] + kv_t[c]
        st_ref[hd] = st
        for c, sl in enumerate(rows):
            o = o_intra[c] + lax.dot_general(q_t[sl], entering[c], NT_DIMS, preferred_element_type=F32)
            o_ref[sl, vl] = (_rms_rows(o, hg_ref[...]) * _silu(r_ref[sl, vl])).astype(BF16)


def _gla(z3, ga3, w_a2, b_a, head_g, *, tc):
    bsz, s, _ = z3.shape
    hk, hv = C_HEADS * C_DK, C_HEADS * C_DV
    wa = jnp.pad(w_a2, ((0, V7X_LANES - C_GATE_RANK), (0, 0))).astype(BF16)
    vmem = (2 * (2 * tc * hk * 4 + 3 * tc * hv * 4 + tc * V7X_LANES * 4 + V7X_LANES * hk * 2)
            + hv * C_DK * 4 + 12 * tc * C_DK * 4 + 2 * tc * hk * 4)
    return pl.pallas_call(
        _gla_kernel,
        grid=(bsz, s // tc),
        in_specs=[pl.BlockSpec((None, tc, hk), lambda b, t: (b, t, 0)),
                  pl.BlockSpec((None, tc, hk), lambda b, t: (b, t, 1)),
                  pl.BlockSpec((None, tc, hv), lambda b, t: (b, t, 1)),
                  pl.BlockSpec((None, tc, hv), lambda b, t: (b, t, 2)),
                  pl.BlockSpec((None, tc, V7X_LANES), lambda b, t: (b, t, 0)),
                  pl.BlockSpec((V7X_LANES, hk), lambda b, t: (0, 0)),
                  pl.BlockSpec((1, hk), lambda b, t: (0, 0)),
                  pl.BlockSpec((1, C_DV), lambda b, t: (0, 0))],
        out_specs=pl.BlockSpec((None, tc, hv), lambda b, t: (b, t, 0)),
        out_shape=jax.ShapeDtypeStruct((bsz, s, hv), BF16),
        scratch_shapes=[pltpu.VMEM((C_HEADS, C_DV, C_DK), F32)],
        compiler_params=_params(("parallel", "arbitrary"), vmem),
        name="gla",
    )(z3, z3, z3, z3, ga3, wa, b_a.reshape(1, -1), head_g.reshape(1, C_DV))


def _ffn_kernel(x_ref, a_ref, b_ref, wa_ref, wb_ref, g_ref, wg_ref, wu_ref, cw_ref, cb_ref, wd_ref, *rest,
                tiles_per_seq, tf, n_cast):
    cast_src, o_ref, cast_dst = rest[:n_cast], rest[n_cast], rest[n_cast + 1:2 * n_cast + 1]
    h_ref, act_ref, gbuf_ref, halo_ref = rest[2 * n_cast + 1:]
    tm = x_ref.shape[0]
    halo_rows = V7X_SUBLANES
    for r in range(tm // ROW_SUB):
        rows = slice(r * ROW_SUB, (r + 1) * ROW_SUB)
        x1 = x_ref[rows, :] + jnp.dot(a_ref[rows, :].astype(BF16), wa_ref[...], preferred_element_type=F32)
        x1 = x1 + jnp.dot(b_ref[rows, :].astype(BF16), wb_ref[...], preferred_element_type=F32)
        o_ref[rows, :] = x1
        h_ref[rows, :] = _rms_rows(x1, g_ref[...]).astype(BF16)

    @pl.when(pl.program_id(0) % tiles_per_seq == 0)
    def _():
        halo_ref[...] = jnp.zeros_like(halo_ref)

    for j in range(wg_ref.shape[1] // tf):
        cols = slice(j * tf, (j + 1) * tf)
        h = h_ref[...]
        gate = jnp.dot(h, wg_ref[:, cols], preferred_element_type=F32)
        up = jnp.dot(h, wu_ref[:, cols], preferred_element_type=F32)
        gbuf = gbuf_ref.at[j % 2]
        gbuf[0:halo_rows, :] = halo_ref[:, cols]
        gbuf[halo_rows:, :] = gate
        halo_ref[:, cols] = gate[tm - halo_rows:, :]
        conv = cb_ref[:, cols]
        for tap in range(CONV_W - 1):
            conv = conv + gbuf[pl.ds(halo_rows - (CONV_W - 1) + tap, tm), :] * cw_ref[tap:tap + 1, cols]
        conv = conv + gate * cw_ref[CONV_W - 1:CONV_W, cols]
        act_ref[:, cols] = (_silu(conv) * up).astype(BF16)
    o_ref[...] += jnp.dot(act_ref[...], wd_ref[...], preferred_element_type=F32)
    for src, dst in zip(cast_src, cast_dst):
        dst[...] = src[...].astype(BF16)


def _ffn(x, a, a_blk, b, b_blk, w_out_all, lo, g, w_gate_all, w_up_all, conv_w, conv_b, w_down_all, lf,
         cast=(), *, seq, tm, tf):
    n, d = x.shape
    dff = w_gate_all.shape[2]
    kh = w_out_all.shape[1] // 2
    cast_in, cast_out, cast_shape, cast_vmem = _cast_plumbing(cast, lambda i: i)
    vmem = (2 * (2 * tm * d * 4 + 2 * tm * kh * 4 + 2 * kh * d * 2 + 3 * d * dff * 2 + 4 * dff * 4)
            + tm * d * 2 + tm * dff * 2 + 2 * (tm + 8) * tf * 4 + 8 * dff * 4 + 8 * tm * tf * 4 + cast_vmem)
    outs = pl.pallas_call(
        functools.partial(_ffn_kernel, tiles_per_seq=seq // tm, tf=tf, n_cast=len(cast)),
        grid=(n // tm,),
        in_specs=[pl.BlockSpec((tm, d), lambda i: (i, 0)),
                  pl.BlockSpec((tm, kh), lambda i: (i, a_blk)),
                  pl.BlockSpec((tm, kh), lambda i: (i, b_blk)),
                  pl.BlockSpec((None, kh, d), lambda i: (lo, 0, 0)),
                  pl.BlockSpec((None, kh, d), lambda i: (lo, 1, 0)),
                  pl.BlockSpec((1, d), lambda i: (0, 0)),
                  _layer_spec(w_gate_all, lf),
                  _layer_spec(w_up_all, lf),
                  pl.BlockSpec((CONV_W, dff), lambda i: (0, 0)),
                  pl.BlockSpec((1, dff), lambda i: (0, 0)),
                  _layer_spec(w_down_all, lf)] + cast_in,
        out_specs=[pl.BlockSpec((tm, d), lambda i: (i, 0))] + cast_out,
        out_shape=[jax.ShapeDtypeStruct((n, d), F32)] + cast_shape,
        scratch_shapes=[pltpu.VMEM((tm, d), BF16), pltpu.VMEM((tm, dff), BF16),
                        pltpu.VMEM((2, V7X_SUBLANES + tm, tf), F32),
                        pltpu.VMEM((V7X_SUBLANES, dff), F32)],
        compiler_params=_params(("arbitrary",), vmem),
        name="conv_ffn",
    )(x, a, b, w_out_all, w_out_all, g.reshape(1, d), w_gate_all, w_up_all, conv_w, conv_b.reshape(1, dff),
      w_down_all, *[arr for arr, _, _ in cast])
    return outs[0], outs[1:]


def kernel(x, norm_mix_g, norm_ffn_g, ev_w_in, ev_a_ln_g, ev_a_ln_b, ev_a_ws, ev_a_bs, ev_q_g, ev_k_g,
           ev_w_out, od_w_in, od_w_a2, od_b_a, od_head_g, od_w_out, ffn_w_gate, ffn_w_up, ffn_conv_w,
           ffn_conv_b, ffn_w_down):
    bsz, seq, d = x.shape
    n = bsz * seq
    depth = norm_mix_g.shape[0]
    main = 2 * C_HEADS * (C_DK + C_DV)
    ev_w_out, od_w_out = ev_w_out.astype(BF16), od_w_out.astype(BF16)
    od_w_side = jnp.pad(od_w_in[:, :, main:], ((0, 0), (0, 0), (0, V7X_LANES - C_GATE_RANK))).astype(BF16)
    ffn_f32 = (ffn_w_gate, ffn_w_up, ffn_w_down)

    def cast_for(layer, steps):
        if layer >= depth:
            return []
        items = [(w, layer) for w in ffn_f32]
        if layer > 0:
            items.append((ev_w_in, layer // 2) if layer % 2 == 0 else (od_w_in, layer // 2))
        return [(arr, idx, _cast_slabs(arr, steps)) for arr, idx in items]

    w_in = ev_w_in[:1].astype(BF16)
    ffn_w = None
    xf = x.reshape(n, d)
    for layer in range(depth):
        if layer % 2 == 0:
            a_out, qb, kn, vn, qp, kp, vp = _even_in(
                xf, norm_mix_g[layer], w_in, 0, ev_a_ln_g[layer // 2], ev_a_ln_b[layer // 2], ev_a_ws[layer // 2],
                ev_a_bs[layer // 2], ev_q_g[layer // 2], ev_k_g[layer // 2], bsz=bsz, tm=EVEN_IN_ROWS)
            in_seq = lambda t: t.reshape(bsz, seq, -1)
            first_cast = cast_for(layer, bsz * (qb.shape[1] // V7X_LANES)) if layer == 0 else []
            b_out, cast_out = _dilated(in_seq(qb), in_seq(kn), in_seq(vn), qp, kp, vp, first_cast)
            if layer == 0:
                ffn_w = cast_out
            mix = (a_out, 0, b_out.reshape(n, -1), 0, ev_w_out, layer // 2)
        else:
            z, ga = _norm_matmul(xf, norm_mix_g[layer], w_in, 0, main, od_w_side[layer // 2],
                                 tm=ODD_IN_ROWS)
            mixed = _gla(z.reshape(bsz, seq, -1), ga.reshape(bsz, seq, -1), od_w_a2[layer // 2], od_b_a[layer // 2],
                         od_head_g[layer // 2], tc=GLA_ROWS).reshape(n, -1)
            mix = (mixed, 0, mixed, 1, od_w_out, layer // 2)
        xf, cast_out = _ffn(xf, *mix, norm_ffn_g[layer], ffn_w[0], ffn_w[1], ffn_conv_w[layer], ffn_conv_b[layer],
                            ffn_w[2], 0, cast_for(layer + 1, n // FFN_ROWS), seq=seq, tm=FFN_ROWS,
                            tf=FFN_COLS)
        if layer + 1 < depth:
            ffn_w, w_in = cast_out[:3], cast_out[3]
    return xf.reshape(bsz, seq, d)
```

```python
import functools

import jax
import jax.numpy as jnp
from jax import lax
from jax.experimental import pallas as pl
from jax.experimental.pallas import tpu as pltpu

A_GROUPS = 8
A_GROUP_DIM = 64
A_CHUNK = 128
B_HEAD_DIM = 64
B_DILATIONS = (1, 4, 16)
B_BLOCK = 128
C_HEADS = 4
C_DK = 128
C_DV = 256
C_GATE_RANK = 16
C_TAU = 16.0
C_CHUNK = 64
CONV_W = 3
EPS = 1e-6
NEG = -1e30

V7X_LANES = 128
V7X_SUBLANES = 8
V7X_VMEM_BUDGET = 56 * 1024 * 1024
BF16_ROWS = 16

EVEN_IN_ROWS = 1024
ODD_IN_ROWS = 512
GLA_ROWS = 512
FFN_ROWS = 512
FFN_COLS = 256

F32 = jnp.float32
BF16 = jnp.bfloat16
NT_DIMS = (((1,), (1,)), ((), ()))
TN_DIMS = (((0,), (0,)), ((), ()))

ROW_SUB = 2 * A_CHUNK
B_PLANES = max(B_DILATIONS)
STAGE_PITCH = 24
B_SCORE_SCALE = B_HEAD_DIM ** -0.5 * 1.4426950408889634


def _params(semantics, vmem_bytes):
    return pltpu.CompilerParams(
        dimension_semantics=semantics,
        vmem_limit_bytes=min(int(vmem_bytes * 1.25) + (4 << 20), V7X_VMEM_BUDGET))


def _layer_spec(w_all, li):
    return pl.BlockSpec((None,) + w_all.shape[1:], lambda i: (li, 0, 0))


def _cast_slabs(arr, steps):
    return steps if arr.shape[1] // steps % BF16_ROWS == 0 else steps // 2


def _cast_plumbing(cast, linear_step):
    ins, outs, shapes, vmem = [], [], [], 0
    for arr, layer, slabs in cast:
        slab = (None, arr.shape[1] // slabs, arr.shape[2])
        ins.append(pl.BlockSpec(
            slab, lambda *g, layer=layer, last=slabs - 1: (layer, jnp.minimum(linear_step(*g), last), 0)))
        outs.append(pl.BlockSpec(slab, lambda *g, last=slabs - 1: (0, jnp.minimum(linear_step(*g), last), 0)))
        shapes.append(jax.ShapeDtypeStruct((1,) + arr.shape[1:], BF16))
        vmem += 2 * slab[1] * slab[2] * 6
    return ins, outs, shapes, vmem


def _rms_rows(x, g):
    return x * lax.rsqrt(jnp.mean(x * x, axis=-1, keepdims=True) + EPS) * g


def _gelu(x):
    return 0.5 * x * (1.0 + lax.erf(x * (0.5 ** 0.5)))


def _silu(x):
    return x * jax.nn.sigmoid(x)


def _head_pair_rms(x, g, head0):
    x2 = x * x
    s0 = jnp.sum(jnp.where(head0, x2, 0.0), axis=-1, keepdims=True)
    s1 = jnp.sum(jnp.where(head0, 0.0, x2), axis=-1, keepdims=True)
    ms = jnp.where(head0, s0, s1) * (1.0 / B_HEAD_DIM)
    return x * lax.rsqrt(ms + EPS) * g


def _norm_matmul_kernel(x_ref, g_ref, w_ref, ws_ref, o_ref, os_ref):
    for r in range(x_ref.shape[0] // ROW_SUB):
        rows = slice(r * ROW_SUB, (r + 1) * ROW_SUB)
        h = _rms_rows(x_ref[rows, :], g_ref[...]).astype(BF16)
        o_ref[rows, :] = jnp.dot(h, w_ref[:, :o_ref.shape[1]], preferred_element_type=F32)
        os_ref[rows, :] = jnp.dot(h, ws_ref[...], preferred_element_type=F32)


def _norm_matmul(x, g, w_all, li, f, w_side, *, tm):
    n, d = x.shape
    fs = w_side.shape[1]
    vmem = 2 * (tm * d * 4 + d * (w_all.shape[2] + fs) * 2 + tm * (f + fs) * 4) + 2 * ROW_SUB * f * 4
    return pl.pallas_call(
        _norm_matmul_kernel,
        grid=(n // tm,),
        in_specs=[pl.BlockSpec((tm, d), lambda i: (i, 0)),
                  pl.BlockSpec((1, d), lambda i: (0, 0)),
                  _layer_spec(w_all, li),
                  pl.BlockSpec((d, fs), lambda i: (0, 0))],
        out_specs=(pl.BlockSpec((tm, f), lambda i: (i, 0)), pl.BlockSpec((tm, fs), lambda i: (i, 0))),
        out_shape=(jax.ShapeDtypeStruct((n, f), F32), jax.ShapeDtypeStruct((n, fs), F32)),
        compiler_params=_params(("parallel",), vmem),
        name="norm_matmul",
    )(x, g.reshape(1, d), w_all, w_side)


def _even_in_kernel(x_ref, g_ref, w_ref, lng_ref, lnb_ref, ws_ref, bs_ref, qg_ref, kg_ref,
                    a_ref, qb_ref, kn_ref, vn_ref, qp_ref, kp_ref, vp_ref, h_ref, z_ref, stage_ref):
    tm = x_ref.shape[0]
    aw = a_ref.shape[1]
    bw = kn_ref.shape[1]
    tiles = bw // V7X_LANES
    groups = ROW_SUB // B_PLANES
    per_block = B_BLOCK // B_PLANES
    row = lax.broadcasted_iota(jnp.int32, (A_CHUNK, 2 * A_CHUNK), 0)
    col = lax.broadcasted_iota(jnp.int32, (A_CHUNK, 2 * A_CHUNK), 1)
    causal = (col % A_CHUNK) <= row
    lane2 = lax.broadcasted_iota(jnp.int32, (1, 2 * V7X_LANES), 1)
    first_group = (lane2 % V7X_LANES) < A_GROUP_DIM
    head0 = lax.broadcasted_iota(jnp.int32, (1, V7X_LANES), 1) < B_HEAD_DIM
    w_pairs = [jnp.where(causal, ws_ref[t], 0.0).astype(BF16) for t in range(ws_ref.shape[0])]
    for r in range(tm // ROW_SUB):
        rows = slice(r * ROW_SUB, (r + 1) * ROW_SUB)
        h_ref[rows, :] = _rms_rows(x_ref[rows, :], g_ref[...]).astype(BF16)
    z_ref[...] = jnp.dot(h_ref[...], w_ref[...], preferred_element_type=F32)
    for r in range(tm // ROW_SUB):
        base = r * ROW_SUB
        rows = slice(base, base + ROW_SUB)
        stage = stage_ref.at[r % 2]
        z = z_ref.at[rows]

        v = _gelu(z[:, aw:2 * aw])
        vc = v - jnp.mean(v, axis=-1, keepdims=True)
        vn = vc * lax.rsqrt(jnp.mean(vc * vc, axis=-1, keepdims=True) + EPS) * lng_ref[...] + lnb_ref[...]
        for t in range(aw // V7X_LANES):
            lanes = slice(t * V7X_LANES, (t + 1) * V7X_LANES)
            cc = jnp.concatenate([vn[:A_CHUNK, lanes], vn[A_CHUNK:, lanes]], axis=1)
            rhs = jnp.concatenate([jnp.where(first_group, cc, 0.0),
                                   jnp.where(first_group, 0.0, cc)], axis=0).astype(BF16)
            mixed = jnp.dot(w_pairs[t], rhs, preferred_element_type=F32)
            bias = bs_ref[:, lanes]
            for c in range(2):
                chunk = slice(c * A_CHUNK, (c + 1) * A_CHUNK)
                u = _gelu(z[chunk, lanes])
                a_ref[base + c * A_CHUNK:base + (c + 1) * A_CHUNK, lanes] = (
                    u * (mixed[:, c * V7X_LANES:(c + 1) * V7X_LANES] + bias)).astype(BF16)

        for t in range(tiles):
            lanes = slice(t * V7X_LANES, (t + 1) * V7X_LANES)
            q = _head_pair_rms(z[:, 2 * aw + t * V7X_LANES:2 * aw + (t + 1) * V7X_LANES], qg_ref[...], head0)
            q = q * B_SCORE_SCALE
            k = _head_pair_rms(z[:, 2 * aw + bw + t * V7X_LANES:2 * aw + bw + (t + 1) * V7X_LANES],
                               kg_ref[...], head0)
            vb = z[:, 2 * aw + 2 * bw + t * V7X_LANES:2 * aw + 2 * bw + (t + 1) * V7X_LANES]
            kn_ref[rows, lanes] = k.astype(BF16)
            vn_ref[rows, lanes] = vb.astype(BF16)
            for i, val in enumerate((q, k, vb)):
                for grp in range(groups):
                    stage[i * tiles + t, grp * STAGE_PITCH:grp * STAGE_PITCH + B_PLANES, :] = (
                        val[grp * B_PLANES:(grp + 1) * B_PLANES])
            for p in range(B_PLANES):
                sel = pl.ds(p, groups, stride=STAGE_PITCH)
                for i, planes in enumerate((qp_ref, kp_ref, vp_ref)):
                    planes[p, r * groups:(r + 1) * groups, lanes] = stage[i * tiles + t, sel, :].astype(BF16)
            for blk in range(ROW_SUB // B_BLOCK):
                first = blk * per_block * STAGE_PITCH
                slabs = [stage[t, pl.ds(first + p, per_block, stride=STAGE_PITCH), :] for p in range(B_PLANES)]
                qb_ref[base + blk * B_BLOCK:base + (blk + 1) * B_BLOCK, lanes] = (
                    jnp.concatenate(slabs, axis=0).astype(BF16))


def _even_in(x, g, w_all, li, ln_g, ln_b, w_s, b_s, q_g, k_g, *, bsz, tm):
    n, d = x.shape
    seq = n // bsz
    aw = A_GROUPS * A_GROUP_DIM
    f = w_all.shape[2]
    bw = (f - 2 * aw) // 3
    pairs = A_GROUPS // 2
    tiles = seq // tm
    per_plane = tm // B_PLANES
    ws_pairs = w_s.reshape(pairs, 2, A_CHUNK, A_CHUNK).transpose(0, 2, 1, 3).reshape(pairs, A_CHUNK, 2 * A_CHUNK)
    bias = jnp.repeat(b_s.T, A_GROUP_DIM, axis=1)
    qg2 = jnp.tile(q_g, 2).reshape(1, V7X_LANES)
    kg2 = jnp.tile(k_g, 2).reshape(1, V7X_LANES)
    nat_spec = pl.BlockSpec((tm, bw), lambda i: (i, 0))
    plane_spec = pl.BlockSpec((None, B_PLANES, per_plane, bw), lambda i: (i // tiles, 0, i % tiles, 0))
    nat_shape = jax.ShapeDtypeStruct((n, bw), BF16)
    plane_shape = jax.ShapeDtypeStruct((bsz, B_PLANES, seq // B_PLANES, bw), BF16)
    stage_rows = ROW_SUB // B_PLANES * STAGE_PITCH
    vmem = (2 * (tm * d * 4 + d * f * 2 + tm * aw * 2 + 6 * tm * bw * 2)
            + tm * d * 2 + tm * f * 4 + 2 * 3 * stage_rows * bw * 4
            + 2 * ROW_SUB * (2 * aw + 3 * bw) * 4)
    return pl.pallas_call(
        _even_in_kernel,
        grid=(n // tm,),
        in_specs=[pl.BlockSpec((tm, d), lambda i: (i, 0)),
                  pl.BlockSpec((1, d), lambda i: (0, 0)),
                  _layer_spec(w_all, li),
                  pl.BlockSpec((1, aw), lambda i: (0, 0)),
                  pl.BlockSpec((1, aw), lambda i: (0, 0)),
                  pl.BlockSpec((pairs, A_CHUNK, 2 * A_CHUNK), lambda i: (0, 0, 0)),
                  pl.BlockSpec((A_CHUNK, aw), lambda i: (0, 0)),
                  pl.BlockSpec((1, V7X_LANES), lambda i: (0, 0)),
                  pl.BlockSpec((1, V7X_LANES), lambda i: (0, 0))],
        out_specs=(pl.BlockSpec((tm, aw), lambda i: (i, 0)), nat_spec, nat_spec, nat_spec,
                   plane_spec, plane_spec, plane_spec),
        out_shape=(jax.ShapeDtypeStruct((n, aw), BF16), nat_shape, nat_shape, nat_shape,
                   plane_shape, plane_shape, plane_shape),
        scratch_shapes=[pltpu.VMEM((tm, d), BF16), pltpu.VMEM((tm, f), F32),
                        pltpu.VMEM((2, 3 * (bw // V7X_LANES), stage_rows, V7X_LANES), F32)],
        compiler_params=_params(("parallel",), vmem),
        name="even_in",
    )(x, g.reshape(1, d), w_all, ln_g.reshape(1, aw), ln_b.reshape(1, aw), ws_pairs, bias, qg2, kg2)


def _dilated_kernel(qb_ref, kn_ref, vn_ref, qp_ref, kp_ref, vp_ref, *rest, n_cast):
    cast_src, o_ref, cast_dst = rest[:n_cast], rest[n_cast], rest[n_cast + 1:2 * n_cast + 1]
    ob_ref, mb_ref, db_ref, mask_ref = rest[2 * n_cast + 1:]
    for src, dst in zip(cast_src, cast_dst):
        dst[...] = src[...].astype(BF16)
    m_rows = qp_ref.shape[1]
    blk2 = 2 * B_BLOCK
    head0 = lax.broadcasted_iota(jnp.int32, (1, V7X_LANES), 1) < B_HEAD_DIM

    rowi = lax.broadcasted_iota(jnp.int32, (blk2, blk2), 0) % B_BLOCK
    coli = lax.broadcasted_iota(jnp.int32, (blk2, blk2), 1)
    is_cur = coli >= B_BLOCK
    colj = coli % B_BLOCK
    for bi, d in enumerate(B_DILATIONS):
        planes = B_PLANES // d
        mb = B_BLOCK // planes
        i_pos = (rowi % mb) * planes + rowi // mb
        j_pos = colj if d == 1 else (colj % mb) * planes + colj // mb
        band = jnp.where(jnp.where(is_cur, i_pos - j_pos, j_pos - i_pos) >= 0, 0.0, NEG)
        mask_ref[2 * bi] = jnp.where(is_cur, band, NEG)
        mask_ref[2 * bi + 1] = band

    ones = jnp.ones((blk2, V7X_LANES), BF16)

    def attend(qb, kcat, vcat, bias):
        zero = jnp.zeros_like(qb)
        q2 = jnp.concatenate([jnp.where(head0, qb, zero), jnp.where(head0, zero, qb)], axis=0)
        s = lax.dot_general(q2, kcat, NT_DIMS, preferred_element_type=F32) + bias
        m = jnp.max(s, axis=-1, keepdims=True)
        p = jnp.exp2(s - m).astype(BF16)
        pv = jnp.dot(p, jnp.concatenate([vcat, ones], axis=1), preferred_element_type=F32)
        top, bot = pv[:B_BLOCK], pv[B_BLOCK:]
        return (jnp.where(head0, top[:, :V7X_LANES], bot[:, :V7X_LANES]),
                jnp.where(head0, m[:B_BLOCK], m[B_BLOCK:]),
                jnp.where(head0, top[:, V7X_LANES:], bot[:, V7X_LANES:]))

    def token_block(n, carry):
        per_block = B_BLOCK // B_PLANES
        off = pl.multiple_of(n * B_BLOCK, B_BLOCK)
        off_prev = pl.multiple_of(jnp.maximum(n - 1, 0) * B_BLOCK, B_BLOCK)
        off_plane = pl.multiple_of(n * per_block, per_block)
        kcat = jnp.concatenate([kn_ref[pl.ds(off_prev, B_BLOCK), :], kn_ref[pl.ds(off, B_BLOCK), :]], axis=0)
        vcat = jnp.concatenate([vn_ref[pl.ds(off_prev, B_BLOCK), :], vn_ref[pl.ds(off, B_BLOCK), :]], axis=0)
        o, m, den = attend(qb_ref[pl.ds(off, B_BLOCK), :], kcat, vcat, mask_ref[jnp.minimum(n, 1)])
        for p in range(B_PLANES):
            for ref, val in ((ob_ref, o), (mb_ref, m), (db_ref, den)):
                ref[0, p, pl.ds(off_plane, per_block), :] = val[p * per_block:(p + 1) * per_block]
        return carry

    lax.fori_loop(0, kn_ref.shape[0] // B_BLOCK, token_block, 0, unroll=True)

    for bi, d in enumerate(B_DILATIONS):
        if d == 1:
            continue
        planes = B_PLANES // d
        mb = B_BLOCK // planes
        nb = m_rows // mb

        def plane_block(blk, carry, bi=bi, d=d, planes=planes, mb=mb, nb=nb):
            r = blk // nb
            n = blk % nb
            off = pl.multiple_of(n * mb, mb)
            off_prev = pl.multiple_of(jnp.maximum(n - 1, 0) * mb, mb)

            def gather(ref, offs):
                return jnp.concatenate([ref[r + d * a, pl.ds(o, mb), :] for o in offs for a in range(planes)],
                                       axis=0)

            o, m, den = attend(gather(qp_ref, (off,)), gather(kp_ref, (off_prev, off)),
                               gather(vp_ref, (off_prev, off)), mask_ref[2 * bi + jnp.minimum(n, 1)])
            for a in range(planes):
                for ref, val in ((ob_ref, o), (mb_ref, m), (db_ref, den)):
                    ref[bi, r + d * a, pl.ds(off, mb), :] = val[a * mb:(a + 1) * mb]
            return carry

        lax.fori_loop(0, d * nb, plane_block, 0, unroll=True)

    branches = range(len(B_DILATIONS))
    for r in range(B_PLANES):
        mx = functools.reduce(jnp.maximum, [mb_ref[bi, r] for bi in branches])
        es = [jnp.exp2(mb_ref[bi, r] - mx) for bi in branches]
        num = functools.reduce(lambda a, b: a + b, [es[bi] * ob_ref[bi, r] for bi in branches])
        den = functools.reduce(lambda a, b: a + b, [es[bi] * db_ref[bi, r] for bi in branches])
        o_ref[pl.ds(r, m_rows, stride=B_PLANES), :] = num / den


def _dilated(qb, kn, vn, qp, kp, vp, cast=()):
    bsz, s, bw = kn.shape
    pairs = bw // V7X_LANES
    cast_in, cast_out, cast_shape, cast_vmem = _cast_plumbing(cast, lambda b, p: b * pairs + p)
    m_rows = s // B_PLANES
    tile_f32 = s * V7X_LANES * 4
    vmem = (2 * (6 * tile_f32 // 2 + tile_f32) + 9 * tile_f32 + 6 * 4 * B_BLOCK * B_BLOCK * 4 + 8 * tile_f32 // 16
            + cast_vmem)
    nat_spec = pl.BlockSpec((None, s, V7X_LANES), lambda b, p: (b, 0, p))
    plane_spec = pl.BlockSpec((None, B_PLANES, m_rows, V7X_LANES), lambda b, p: (b, 0, 0, p))
    outs = pl.pallas_call(
        functools.partial(_dilated_kernel, n_cast=len(cast)),
        grid=(bsz, pairs),
        in_specs=[nat_spec, nat_spec, nat_spec, plane_spec, plane_spec, plane_spec] + cast_in,
        out_specs=[pl.BlockSpec((None, s, V7X_LANES), lambda b, p: (b, 0, p))] + cast_out,
        out_shape=[jax.ShapeDtypeStruct((bsz, s, bw), F32)] + cast_shape,
        scratch_shapes=[pltpu.VMEM((len(B_DILATIONS), B_PLANES, m_rows, V7X_LANES), F32)] * 3
                       + [pltpu.VMEM((2 * len(B_DILATIONS), 2 * B_BLOCK, 2 * B_BLOCK), F32)],
        compiler_params=_params(("parallel", "parallel"), vmem),
        name="dilated_attention",
    )(qb, kn, vn, qp, kp, vp, *[arr for arr, _, _ in cast])
    return outs[0], outs[1:]


def _gla_kernel(q_ref, k_ref, v_ref, r_ref, ga_ref, wa_ref, ba_ref, hg_ref, o_ref, st_ref):
    tc = q_ref.shape[0]
    chunks = tc // C_CHUNK

    @pl.when(pl.program_id(1) == 0)
    def _():
        st_ref[...] = jnp.zeros_like(st_ref)

    gate = jnp.dot(ga_ref[...].astype(BF16), wa_ref[...], preferred_element_type=F32) + ba_ref[...]
    log_a = jax.nn.log_sigmoid(gate) / C_TAU
    ci = lax.broadcasted_iota(jnp.int32, (C_CHUNK, C_CHUNK), 0)
    cj = lax.broadcasted_iota(jnp.int32, (C_CHUNK, C_CHUNK), 1)
    causal = ci >= cj
    tri = jnp.where(causal, 1.0, 0.0).astype(BF16)
    rows = [slice(c * C_CHUNK, (c + 1) * C_CHUNK) for c in range(chunks)]
    for hd in range(C_HEADS):
        kl = slice(hd * C_DK, (hd + 1) * C_DK)
        vl = slice(hd * C_DV, (hd + 1) * C_DV)
        la = log_a[:, kl]
        hi = la.astype(BF16)
        rest = la - hi.astype(F32)
        mid = rest.astype(BF16)
        lo = (rest - mid.astype(F32)).astype(BF16)
        pieces = jnp.concatenate([hi, mid, lo], axis=1)
        sums = [jnp.dot(tri, pieces[sl], preferred_element_type=F32) for sl in rows]
        b = jnp.concatenate([s3[:, :C_DK] + s3[:, C_DK:2 * C_DK] + s3[:, 2 * C_DK:] for s3 in sums], axis=0)
        b3 = b.reshape(chunks, C_CHUNK, C_DK)
        b_last = b3[:, C_CHUNK - 1:C_CHUNK, :]
        k = k_ref[:, kl]
        q_t = ((q_ref[:, kl] * (C_DK ** -0.5)) * jnp.exp(b)).astype(BF16)
        k_t = (k * jnp.exp(-b)).astype(BF16)
        k_s = (k.reshape(chunks, C_CHUNK, C_DK) * jnp.exp(b_last - b3)).reshape(tc, C_DK).astype(BF16)
        decay = jnp.exp(b_last)
        o_intra, kv_t = [], []
        for sl in rows:
            v_c = v_ref[sl, vl].astype(BF16)
            attn = lax.dot_general(q_t[sl], k_t[sl], NT_DIMS, preferred_element_type=F32)
            attn = jnp.where(causal, attn, 0.0).astype(BF16)
            o_intra.append(jnp.dot(attn, v_c, preferred_element_type=F32))
            kv_t.append(lax.dot_general(v_c, k_s[sl], TN_DIMS, preferred_element_type=F32))
        st = st_ref[hd]
        entering = []
        for c in range(chunks):
            entering.append(st.astype(BF16))
            st = st * decay[c] + kv_t[c]
        st_ref[hd] = st
        for c, sl in enumerate(rows):
            o = o_intra[c] + lax.dot_general(q_t[sl], entering[c], NT_DIMS, preferred_element_type=F32)
            o_ref[sl, vl] = (_rms_rows(o, hg_ref[...]) * _silu(r_ref[sl, vl])).astype(BF16)


def _gla(z3, ga3, w_a2, b_a, head_g, *, tc):
    bsz, s, _ = z3.shape
    hk, hv = C_HEADS * C_DK, C_HEADS * C_DV
    wa = jnp.pad(w_a2, ((0, V7X_LANES - C_GATE_RANK), (0, 0))).astype(BF16)
    vmem = (2 * (2 * tc * hk * 4 + 3 * tc * hv * 4 + tc * V7X_LANES * 4 + V7X_LANES * hk * 2)
            + hv * C_DK * 4 + 12 * tc * C_DK * 4 + 2 * tc * hk * 4)
    return pl.pallas_call(
        _gla_kernel,
        grid=(bsz, s // tc),
        in_specs=[pl.BlockSpec((None, tc, hk), lambda b, t: (b, t, 0)),
                  pl.BlockSpec((None, tc, hk), lambda b, t: (b, t, 1)),
                  pl.BlockSpec((None, tc, hv), lambda b, t: (b, t, 1)),
                  pl.BlockSpec((None, tc, hv), lambda b, t: (b, t, 2)),
                  pl.BlockSpec((None, tc, V7X_LANES), lambda b, t: (b, t, 0)),
                  pl.BlockSpec((V7X_LANES, hk), lambda b, t: (0, 0)),
                  pl.BlockSpec((1, hk), lambda b, t: (0, 0)),
                  pl.BlockSpec((1, C_DV), lambda b, t: (0, 0))],
        out_specs=pl.BlockSpec((None, tc, hv), lambda b, t: (b, t, 0)),
        out_shape=jax.ShapeDtypeStruct((bsz, s, hv), BF16),
        scratch_shapes=[pltpu.VMEM((C_HEADS, C_DV, C_DK), F32)],
        compiler_params=_params(("parallel", "arbitrary"), vmem),
        name="gla",
    )(z3, z3, z3, z3, ga3, wa, b_a.reshape(1, -1), head_g.reshape(1, C_DV))


def _ffn_kernel(x_ref, a_ref, b_ref, wa_ref, wb_ref, g_ref, wg_ref, wu_ref, cw_ref, cb_ref, wd_ref, *rest,
                tiles_per_seq, tf, n_cast):
    cast_src, o_ref, cast_dst = rest[:n_cast], rest[n_cast], rest[n_cast + 1:2 * n_cast + 1]
    h_ref, act_ref, gate_ref, up_ref = rest[2 * n_cast + 1:]
    tm = x_ref.shape[0]
    halo_rows = V7X_SUBLANES
    for r in range(tm // ROW_SUB):
        rows = slice(r * ROW_SUB, (r + 1) * ROW_SUB)
        x1 = x_ref[rows, :] + jnp.dot(a_ref[rows, :].astype(BF16), wa_ref[...], preferred_element_type=F32)
        x1 = x1 + jnp.dot(b_ref[rows, :].astype(BF16), wb_ref[...], preferred_element_type=F32)
        o_ref[rows, :] = x1
        h_ref[rows, :] = _rms_rows(x1, g_ref[...]).astype(BF16)

    seq_start = pl.program_id(0) % tiles_per_seq == 0

    @pl.when(seq_start)
    def _():
        gate_ref[0:halo_rows, :] = jnp.zeros((halo_rows, gate_ref.shape[1]), F32)

    @pl.when(jnp.logical_not(seq_start))
    def _():
        gate_ref[0:halo_rows, :] = gate_ref[tm:tm + halo_rows, :]

    gate_ref[halo_rows:, :] = jnp.dot(h_ref[...], wg_ref[...], preferred_element_type=F32)
    up_ref[...] = jnp.dot(h_ref[...], wu_ref[...], preferred_element_type=F32)
    for j in range(wg_ref.shape[1] // tf):
        cols = slice(j * tf, (j + 1) * tf)
        conv = cb_ref[:, cols]
        for tap in range(CONV_W):
            conv = conv + gate_ref[pl.ds(halo_rows - (CONV_W - 1) + tap, tm), cols] * cw_ref[tap:tap + 1, cols]
        act_ref[:, cols] = (_silu(conv) * up_ref[:, cols]).astype(BF16)
    o_ref[...] += jnp.dot(act_ref[...], wd_ref[...], preferred_element_type=F32)
    for src, dst in zip(cast_src, cast_dst):
        dst[...] = src[...].astype(BF16)


def _ffn(x, a, a_blk, b, b_blk, w_out_all, lo, g, w_gate_all, w_up_all, conv_w, conv_b, w_down_all, lf,
         cast=(), *, seq, tm, tf):
    n, d = x.shape
    dff = w_gate_all.shape[2]
    kh = w_out_all.shape[1] // 2
    cast_in, cast_out, cast_shape, cast_vmem = _cast_plumbing(cast, lambda i: i)
    vmem = (2 * (2 * tm * d * 4 + 2 * tm * kh * 4 + 2 * kh * d * 2 + 3 * d * dff * 2 + 4 * dff * 4)
            + tm * d * 2 + tm * dff * 2 + (2 * tm + 8) * dff * 4 + 4 * tm * tf * 4 + cast_vmem)
    outs = pl.pallas_call(
        functools.partial(_ffn_kernel, tiles_per_seq=seq // tm, tf=tf, n_cast=len(cast)),
        grid=(n // tm,),
        in_specs=[pl.BlockSpec((tm, d), lambda i: (i, 0)),
                  pl.BlockSpec((tm, kh), lambda i: (i, a_blk)),
                  pl.BlockSpec((tm, kh), lambda i: (i, b_blk)),
                  pl.BlockSpec((None, kh, d), lambda i: (lo, 0, 0)),
                  pl.BlockSpec((None, kh, d), lambda i: (lo, 1, 0)),
                  pl.BlockSpec((1, d), lambda i: (0, 0)),
                  _layer_spec(w_gate_all, lf),
                  _layer_spec(w_up_all, lf),
                  pl.BlockSpec((CONV_W, dff), lambda i: (0, 0)),
                  pl.BlockSpec((1, dff), lambda i: (0, 0)),
                  _layer_spec(w_down_all, lf)] + cast_in,
        out_specs=[pl.BlockSpec((tm, d), lambda i: (i, 0))] + cast_out,
        out_shape=[jax.ShapeDtypeStruct((n, d), F32)] + cast_shape,
        scratch_shapes=[pltpu.VMEM((tm, d), BF16), pltpu.VMEM((tm, dff), BF16),
                        pltpu.VMEM((V7X_SUBLANES + tm, dff), F32), pltpu.VMEM((tm, dff), F32)],
        compiler_params=_params(("arbitrary",), vmem),
        name="conv_ffn",
    )(x, a, b, w_out_all, w_out_all, g.reshape(1, d), w_gate_all, w_up_all, conv_w, conv_b.reshape(1, dff),
      w_down_all, *[arr for arr, _, _ in cast])
    return outs[0], outs[1:]


def kernel(x, norm_mix_g, norm_ffn_g, ev_w_in, ev_a_ln_g, ev_a_ln_b, ev_a_ws, ev_a_bs, ev_q_g, ev_k_g,
           ev_w_out, od_w_in, od_w_a2, od_b_a, od_head_g, od_w_out, ffn_w_gate, ffn_w_up, ffn_conv_w,
           ffn_conv_b, ffn_w_down):
    bsz, seq, d = x.shape
    n = bsz * seq
    depth = norm_mix_g.shape[0]
    main = 2 * C_HEADS * (C_DK + C_DV)
    ev_w_out, od_w_out = ev_w_out.astype(BF16), od_w_out.astype(BF16)
    od_w_side = jnp.pad(od_w_in[:, :, main:], ((0, 0), (0, 0), (0, V7X_LANES - C_GATE_RANK))).astype(BF16)
    ffn_f32 = (ffn_w_gate, ffn_w_up, ffn_w_down)

    def cast_for(layer, steps):
        if layer >= depth:
            return []
        items = [(w, layer) for w in ffn_f32]
        if layer > 0:
            items.append((ev_w_in, layer // 2) if layer % 2 == 0 else (od_w_in, layer // 2))
        return [(arr, idx, _cast_slabs(arr, steps)) for arr, idx in items]

    w_in = ev_w_in[:1].astype(BF16)
    ffn_w = None
    xf = x.reshape(n, d)
    for layer in range(depth):
        if layer % 2 == 0:
            a_out, qb, kn, vn, qp, kp, vp = _even_in(
                xf, norm_mix_g[layer], w_in, 0, ev_a_ln_g[layer // 2], ev_a_ln_b[layer // 2], ev_a_ws[layer // 2],
                ev_a_bs[layer // 2], ev_q_g[layer // 2], ev_k_g[layer // 2], bsz=bsz, tm=EVEN_IN_ROWS)
            in_seq = lambda t: t.reshape(bsz, seq, -1)
            first_cast = cast_for(layer, bsz * (qb.shape[1] // V7X_LANES)) if layer == 0 else []
            b_out, cast_out = _dilated(in_seq(qb), in_seq(kn), in_seq(vn), qp, kp, vp, first_cast)
            if layer == 0:
                ffn_w = cast_out
            mix = (a_out, 0, b_out.reshape(n, -1), 0, ev_w_out, layer // 2)
        else:
            z, ga = _norm_matmul(xf, norm_mix_g[layer], w_in, 0, main, od_w_side[layer // 2],
                                 tm=ODD_IN_ROWS)
            mixed = _gla(z.reshape(bsz, seq, -1), ga.reshape(bsz, seq, -1), od_w_a2[layer // 2], od_b_a[layer // 2],
                         od_head_g[layer // 2], tc=GLA_ROWS).reshape(n, -1)
            mix = (mixed, 0, mixed, 1, od_w_out, layer // 2)
        xf, cast_out = _ffn(xf, *mix, norm_ffn_g[layer], ffn_w[0], ffn_w[1], ffn_conv_w[layer], ffn_conv_b[layer],
                            ffn_w[2], 0, cast_for(layer + 1, n // FFN_ROWS), seq=seq, tm=FFN_ROWS,
                            tf=FFN_COLS)
        if layer + 1 < depth:
            ffn_w, w_in = cast_out[:3], cast_out[3]
    return xf.reshape(bsz, seq, d)
```

```python
import functools

import jax
import jax.numpy as jnp
from jax import lax
from jax.experimental import pallas as pl
from jax.experimental.pallas import tpu as pltpu

A_GROUPS = 8
A_GROUP_DIM = 64
A_CHUNK = 128
B_HEAD_DIM = 64
B_DILATIONS = (1, 4, 16)
B_BLOCK = 128
C_HEADS = 4
C_DK = 128
C_DV = 256
C_GATE_RANK = 16
C_TAU = 16.0
C_CHUNK = 64
CONV_W = 3
EPS = 1e-6
NEG = -1e30

V7X_LANES = 128
V7X_SUBLANES = 8
V7X_VMEM_BUDGET = 56 * 1024 * 1024
BF16_ROWS = 16

EVEN_IN_ROWS = 1024
ODD_IN_ROWS = 512
GLA_ROWS = 512
FFN_ROWS = 512
FFN_COLS = 256

F32 = jnp.float32
BF16 = jnp.bfloat16
NT_DIMS = (((1,), (1,)), ((), ()))
TN_DIMS = (((0,), (0,)), ((), ()))

ROW_SUB = 2 * A_CHUNK
B_PLANES = max(B_DILATIONS)
STAGE_PITCH = 24
B_SCORE_SCALE = B_HEAD_DIM ** -0.5 * 1.4426950408889634


def _params(semantics, vmem_bytes):
    return pltpu.CompilerParams(
        dimension_semantics=semantics,
        vmem_limit_bytes=min(int(vmem_bytes * 1.25) + (4 << 20), V7X_VMEM_BUDGET))


def _layer_spec(w_all, li):
    return pl.BlockSpec((None,) + w_all.shape[1:], lambda i: (li, 0, 0))


def _cast_plumbing(cast, steps, linear_step):
    ins, outs, shapes, vmem = [], [], [], 0
    for arr, layer, rows in cast:
        slabs = steps if rows // steps % BF16_ROWS == 0 else steps // 2
        slab = (None, rows // slabs, arr.shape[2])
        ins.append(pl.BlockSpec(
            slab, lambda *g, layer=layer, last=slabs - 1: (layer, jnp.minimum(linear_step(*g), last), 0)))
        outs.append(pl.BlockSpec(slab, lambda *g, last=slabs - 1: (0, jnp.minimum(linear_step(*g), last), 0)))
        shapes.append(jax.ShapeDtypeStruct((1, rows, arr.shape[2]), BF16))
        vmem += 2 * slab[1] * slab[2] * 6
    return ins, outs, shapes, vmem


def _rms_rows(x, g):
    return x * lax.rsqrt(jnp.mean(x * x, axis=-1, keepdims=True) + EPS) * g


def _gelu(x):
    return 0.5 * x * (1.0 + lax.erf(x * (0.5 ** 0.5)))


def _silu(x):
    return x * jax.nn.sigmoid(x)


def _head_pair_rms(x, g, head0):
    x2 = x * x
    s0 = jnp.sum(jnp.where(head0, x2, 0.0), axis=-1, keepdims=True)
    s1 = jnp.sum(jnp.where(head0, 0.0, x2), axis=-1, keepdims=True)
    ms = jnp.where(head0, s0, s1) * (1.0 / B_HEAD_DIM)
    return x * lax.rsqrt(ms + EPS) * g


def _norm_matmul_kernel(x_ref, g_ref, w_ref, ws_ref, o_ref, os_ref):
    for r in range(x_ref.shape[0] // ROW_SUB):
        rows = slice(r * ROW_SUB, (r + 1) * ROW_SUB)
        h = _rms_rows(x_ref[rows, :], g_ref[...]).astype(BF16)
        o_ref[rows, :] = lax.dot_general(h, w_ref[...], NT_DIMS, preferred_element_type=F32)
        os_ref[rows, :] = jnp.dot(h, ws_ref[...], preferred_element_type=F32)


def _norm_matmul(x, g, wt_all, li, w_side, *, tm):
    n, d = x.shape
    f, fs = wt_all.shape[1], w_side.shape[1]
    vmem = 2 * (tm * d * 4 + d * (f + fs) * 2 + tm * (f + fs) * 4) + 2 * ROW_SUB * f * 4
    return pl.pallas_call(
        _norm_matmul_kernel,
        grid=(n // tm,),
        in_specs=[pl.BlockSpec((tm, d), lambda i: (i, 0)),
                  pl.BlockSpec((1, d), lambda i: (0, 0)),
                  _layer_spec(wt_all, li),
                  pl.BlockSpec((d, fs), lambda i: (0, 0))],
        out_specs=(pl.BlockSpec((tm, f), lambda i: (i, 0)), pl.BlockSpec((tm, fs), lambda i: (i, 0))),
        out_shape=(jax.ShapeDtypeStruct((n, f), F32), jax.ShapeDtypeStruct((n, fs), F32)),
        compiler_params=_params(("parallel",), vmem),
        name="norm_matmul",
    )(x, g.reshape(1, d), wt_all, w_side)


def _even_in_kernel(x_ref, g_ref, w_ref, lng_ref, lnb_ref, ws_ref, bs_ref, qg_ref, kg_ref,
                    a_ref, qb_ref, kn_ref, vn_ref, qp_ref, kp_ref, vp_ref, h_ref, z_ref, stage_ref):
    tm = x_ref.shape[0]
    aw = a_ref.shape[1]
    bw = kn_ref.shape[1]
    tiles = bw // V7X_LANES
    groups = ROW_SUB // B_PLANES
    per_block = B_BLOCK // B_PLANES
    row = lax.broadcasted_iota(jnp.int32, (A_CHUNK, 2 * A_CHUNK), 0)
    col = lax.broadcasted_iota(jnp.int32, (A_CHUNK, 2 * A_CHUNK), 1)
    causal = (col % A_CHUNK) <= row
    lane2 = lax.broadcasted_iota(jnp.int32, (1, 2 * V7X_LANES), 1)
    first_group = (lane2 % V7X_LANES) < A_GROUP_DIM
    head0 = lax.broadcasted_iota(jnp.int32, (1, V7X_LANES), 1) < B_HEAD_DIM
    w_pairs = [jnp.where(causal, ws_ref[t], 0.0).astype(BF16) for t in range(ws_ref.shape[0])]
    for r in range(tm // ROW_SUB):
        rows = slice(r * ROW_SUB, (r + 1) * ROW_SUB)
        h_ref[rows, :] = _rms_rows(x_ref[rows, :], g_ref[...]).astype(BF16)
    z_ref[...] = jnp.dot(h_ref[...], w_ref[...], preferred_element_type=F32)
    for r in range(tm // ROW_SUB):
        base = r * ROW_SUB
        rows = slice(base, base + ROW_SUB)
        stage = stage_ref.at[r % 2]
        z = z_ref.at[rows]

        v = _gelu(z[:, aw:2 * aw])
        vc = v - jnp.mean(v, axis=-1, keepdims=True)
        vn = vc * lax.rsqrt(jnp.mean(vc * vc, axis=-1, keepdims=True) + EPS) * lng_ref[...] + lnb_ref[...]
        for t in range(aw // V7X_LANES):
            lanes = slice(t * V7X_LANES, (t + 1) * V7X_LANES)
            cc = jnp.concatenate([vn[:A_CHUNK, lanes], vn[A_CHUNK:, lanes]], axis=1)
            rhs = jnp.concatenate([jnp.where(first_group, cc, 0.0),
                                   jnp.where(first_group, 0.0, cc)], axis=0).astype(BF16)
            mixed = jnp.dot(w_pairs[t], rhs, preferred_element_type=F32)
            bias = bs_ref[:, lanes]
            for c in range(2):
                chunk = slice(c * A_CHUNK, (c + 1) * A_CHUNK)
                u = _gelu(z[chunk, lanes])
                a_ref[base + c * A_CHUNK:base + (c + 1) * A_CHUNK, lanes] = (
                    u * (mixed[:, c * V7X_LANES:(c + 1) * V7X_LANES] + bias)).astype(BF16)

        for t in range(tiles):
            lanes = slice(t * V7X_LANES, (t + 1) * V7X_LANES)
            q = _head_pair_rms(z[:, 2 * aw + t * V7X_LANES:2 * aw + (t + 1) * V7X_LANES], qg_ref[...], head0)
            q = q * B_SCORE_SCALE
            k = _head_pair_rms(z[:, 2 * aw + bw + t * V7X_LANES:2 * aw + bw + (t + 1) * V7X_LANES],
                               kg_ref[...], head0)
            vb = z[:, 2 * aw + 2 * bw + t * V7X_LANES:2 * aw + 2 * bw + (t + 1) * V7X_LANES]
            kn_ref[rows, lanes] = k.astype(BF16)
            vn_ref[rows, lanes] = vb.astype(BF16)
            for i, val in enumerate((q, k, vb)):
                for grp in range(groups):
                    stage[i * tiles + t, grp * STAGE_PITCH:grp * STAGE_PITCH + B_PLANES, :] = (
                        val[grp * B_PLANES:(grp + 1) * B_PLANES])
            for p in range(B_PLANES):
                sel = pl.ds(p, groups, stride=STAGE_PITCH)
                for i, planes in enumerate((qp_ref, kp_ref, vp_ref)):
                    planes[p, r * groups:(r + 1) * groups, lanes] = stage[i * tiles + t, sel, :].astype(BF16)
            for blk in range(ROW_SUB // B_BLOCK):
                first = blk * per_block * STAGE_PITCH
                slabs = [stage[t, pl.ds(first + p, per_block, stride=STAGE_PITCH), :] for p in range(B_PLANES)]
                qb_ref[base + blk * B_BLOCK:base + (blk + 1) * B_BLOCK, lanes] = (
                    jnp.concatenate(slabs, axis=0).astype(BF16))


def _even_in(x, g, w_all, li, ln_g, ln_b, w_s, b_s, q_g, k_g, *, bsz, tm):
    n, d = x.shape
    seq = n // bsz
    aw = A_GROUPS * A_GROUP_DIM
    f = w_all.shape[2]
    bw = (f - 2 * aw) // 3
    pairs = A_GROUPS // 2
    tiles = seq // tm
    per_plane = tm // B_PLANES
    ws_pairs = w_s.reshape(pairs, 2, A_CHUNK, A_CHUNK).transpose(0, 2, 1, 3).reshape(pairs, A_CHUNK, 2 * A_CHUNK)
    bias = jnp.repeat(b_s.T, A_GROUP_DIM, axis=1)
    qg2 = jnp.tile(q_g, 2).reshape(1, V7X_LANES)
    kg2 = jnp.tile(k_g, 2).reshape(1, V7X_LANES)
    nat_spec = pl.BlockSpec((tm, bw), lambda i: (i, 0))
    plane_spec = pl.BlockSpec((None, B_PLANES, per_plane, bw), lambda i: (i // tiles, 0, i % tiles, 0))
    nat_shape = jax.ShapeDtypeStruct((n, bw), BF16)
    plane_shape = jax.ShapeDtypeStruct((bsz, B_PLANES, seq // B_PLANES, bw), BF16)
    stage_rows = ROW_SUB // B_PLANES * STAGE_PITCH
    vmem = (2 * (tm * d * 4 + d * f * 2 + tm * aw * 2 + 6 * tm * bw * 2)
            + tm * d * 2 + tm * f * 4 + 2 * 3 * stage_rows * bw * 4
            + 2 * ROW_SUB * (2 * aw + 3 * bw) * 4)
    return pl.pallas_call(
        _even_in_kernel,
        grid=(n // tm,),
        in_specs=[pl.BlockSpec((tm, d), lambda i: (i, 0)),
                  pl.BlockSpec((1, d), lambda i: (0, 0)),
                  _layer_spec(w_all, li),
                  pl.BlockSpec((1, aw), lambda i: (0, 0)),
                  pl.BlockSpec((1, aw), lambda i: (0, 0)),
                  pl.BlockSpec((pairs, A_CHUNK, 2 * A_CHUNK), lambda i: (0, 0, 0)),
                  pl.BlockSpec((A_CHUNK, aw), lambda i: (0, 0)),
                  pl.BlockSpec((1, V7X_LANES), lambda i: (0, 0)),
                  pl.BlockSpec((1, V7X_LANES), lambda i: (0, 0))],
        out_specs=(pl.BlockSpec((tm, aw), lambda i: (i, 0)), nat_spec, nat_spec, nat_spec,
                   plane_spec, plane_spec, plane_spec),
        out_shape=(jax.ShapeDtypeStruct((n, aw), BF16), nat_shape, nat_shape, nat_shape,
                   plane_shape, plane_shape, plane_shape),
        scratch_shapes=[pltpu.VMEM((tm, d), BF16), pltpu.VMEM((tm, f), F32),
                        pltpu.VMEM((2, 3 * (bw // V7X_LANES), stage_rows, V7X_LANES), F32)],
        compiler_params=_params(("parallel",), vmem),
        name="even_in",
    )(x, g.reshape(1, d), w_all, ln_g.reshape(1, aw), ln_b.reshape(1, aw), ws_pairs, bias, qg2, kg2)


def _dilated_kernel(qb_ref, kn_ref, vn_ref, qp_ref, kp_ref, vp_ref, *rest, n_cast):
    cast_src, o_ref, cast_dst = rest[:n_cast], rest[n_cast], rest[n_cast + 1:2 * n_cast + 1]
    ob_ref, mb_ref, db_ref, mask_ref = rest[2 * n_cast + 1:]
    for src, dst in zip(cast_src, cast_dst):
        dst[...] = src[...].astype(BF16)
    m_rows = qp_ref.shape[1]
    blk2 = 2 * B_BLOCK
    head0 = lax.broadcasted_iota(jnp.int32, (1, V7X_LANES), 1) < B_HEAD_DIM

    rowi = lax.broadcasted_iota(jnp.int32, (blk2, blk2), 0) % B_BLOCK
    coli = lax.broadcasted_iota(jnp.int32, (blk2, blk2), 1)
    is_cur = coli >= B_BLOCK
    colj = coli % B_BLOCK
    for bi, d in enumerate(B_DILATIONS):
        planes = B_PLANES // d
        mb = B_BLOCK // planes
        i_pos = (rowi % mb) * planes + rowi // mb
        j_pos = colj if d == 1 else (colj % mb) * planes + colj // mb
        band = jnp.where(jnp.where(is_cur, i_pos - j_pos, j_pos - i_pos) >= 0, 0.0, NEG)
        mask_ref[2 * bi] = jnp.where(is_cur, band, NEG)
        mask_ref[2 * bi + 1] = band

    ones = jnp.ones((blk2, V7X_LANES), BF16)

    def attend(qb, kcat, vcat, bias):
        zero = jnp.zeros_like(qb)
        q2 = jnp.concatenate([jnp.where(head0, qb, zero), jnp.where(head0, zero, qb)], axis=0)
        s = lax.dot_general(q2, kcat, NT_DIMS, preferred_element_type=F32) + bias
        m = jnp.max(s, axis=-1, keepdims=True)
        p = jnp.exp2(s - m).astype(BF16)
        pv = jnp.dot(p, jnp.concatenate([vcat, ones], axis=1), preferred_element_type=F32)
        top, bot = pv[:B_BLOCK], pv[B_BLOCK:]
        return (jnp.where(head0, top[:, :V7X_LANES], bot[:, :V7X_LANES]),
                jnp.where(head0, m[:B_BLOCK], m[B_BLOCK:]),
                jnp.where(head0, top[:, V7X_LANES:], bot[:, V7X_LANES:]))

    def token_block(n, carry):
        per_block = B_BLOCK // B_PLANES
        off = pl.multiple_of(n * B_BLOCK, B_BLOCK)
        off_prev = pl.multiple_of(jnp.maximum(n - 1, 0) * B_BLOCK, B_BLOCK)
        off_plane = pl.multiple_of(n * per_block, per_block)
        kcat = jnp.concatenate([kn_ref[pl.ds(off_prev, B_BLOCK), :], kn_ref[pl.ds(off, B_BLOCK), :]], axis=0)
        vcat = jnp.concatenate([vn_ref[pl.ds(off_prev, B_BLOCK), :], vn_ref[pl.ds(off, B_BLOCK), :]], axis=0)
        o, m, den = attend(qb_ref[pl.ds(off, B_BLOCK), :], kcat, vcat, mask_ref[jnp.minimum(n, 1)])
        for p in range(B_PLANES):
            for ref, val in ((ob_ref, o), (mb_ref, m), (db_ref, den)):
                ref[0, p, pl.ds(off_plane, per_block), :] = val[p * per_block:(p + 1) * per_block]
        return carry

    lax.fori_loop(0, kn_ref.shape[0] // B_BLOCK, token_block, 0, unroll=True)

    for bi, d in enumerate(B_DILATIONS):
        if d == 1:
            continue
        planes = B_PLANES // d
        mb = B_BLOCK // planes
        nb = m_rows // mb

        def plane_block(blk, carry, bi=bi, d=d, planes=planes, mb=mb, nb=nb):
            r = blk // nb
            n = blk % nb
            off = pl.multiple_of(n * mb, mb)
            off_prev = pl.multiple_of(jnp.maximum(n - 1, 0) * mb, mb)

            def gather(ref, offs):
                return jnp.concatenate([ref[r + d * a, pl.ds(o, mb), :] for o in offs for a in range(planes)],
                                       axis=0)

            o, m, den = attend(gather(qp_ref, (off,)), gather(kp_ref, (off_prev, off)),
                               gather(vp_ref, (off_prev, off)), mask_ref[2 * bi + jnp.minimum(n, 1)])
            for a in range(planes):
                for ref, val in ((ob_ref, o), (mb_ref, m), (db_ref, den)):
                    ref[bi, r + d * a, pl.ds(off, mb), :] = val[a * mb:(a + 1) * mb]
            return carry

        lax.fori_loop(0, d * nb, plane_block, 0, unroll=True)

    branches = range(len(B_DILATIONS))
    for r in range(B_PLANES):
        mx = functools.reduce(jnp.maximum, [mb_ref[bi, r] for bi in branches])
        es = [jnp.exp2(mb_ref[bi, r] - mx) for bi in branches]
        num = functools.reduce(lambda a, b: a + b, [es[bi] * ob_ref[bi, r] for bi in branches])
        den = functools.reduce(lambda a, b: a + b, [es[bi] * db_ref[bi, r] for bi in branches])
        o_ref[pl.ds(r, m_rows, stride=B_PLANES), :] = num / den


def _dilated(qb, kn, vn, qp, kp, vp, cast=()):
    bsz, s, bw = kn.shape
    pairs = bw // V7X_LANES
    cast_in, cast_out, cast_shape, cast_vmem = _cast_plumbing(cast, bsz * pairs, lambda b, p: b * pairs + p)
    m_rows = s // B_PLANES
    tile_f32 = s * V7X_LANES * 4
    vmem = (2 * (6 * tile_f32 // 2 + tile_f32) + 9 * tile_f32 + 6 * 4 * B_BLOCK * B_BLOCK * 4 + 8 * tile_f32 // 16
            + cast_vmem)
    nat_spec = pl.BlockSpec((None, s, V7X_LANES), lambda b, p: (b, 0, p))
    plane_spec = pl.BlockSpec((None, B_PLANES, m_rows, V7X_LANES), lambda b, p: (b, 0, 0, p))
    outs = pl.pallas_call(
        functools.partial(_dilated_kernel, n_cast=len(cast)),
        grid=(bsz, pairs),
        in_specs=[nat_spec, nat_spec, nat_spec, plane_spec, plane_spec, plane_spec] + cast_in,
        out_specs=[pl.BlockSpec((None, s, V7X_LANES), lambda b, p: (b, 0, p))] + cast_out,
        out_shape=[jax.ShapeDtypeStruct((bsz, s, bw), F32)] + cast_shape,
        scratch_shapes=[pltpu.VMEM((len(B_DILATIONS), B_PLANES, m_rows, V7X_LANES), F32)] * 3
                       + [pltpu.VMEM((2 * len(B_DILATIONS), 2 * B_BLOCK, 2 * B_BLOCK), F32)],
        compiler_params=_params(("parallel", "parallel"), vmem),
        name="dilated_attention",
    )(qb, kn, vn, qp, kp, vp, *[arr for arr, _, _ in cast])
    return outs[0], outs[1:]


def _gla_kernel(q_ref, k_ref, v_ref, r_ref, ga_ref, wa_ref, ba_ref, hg_ref, o_ref, st_ref):
    tc = q_ref.shape[0]
    chunks = tc // C_CHUNK

    @pl.when(pl.program_id(1) == 0)
    def _():
        st_ref[...] = jnp.zeros_like(st_ref)

    gate = jnp.dot(ga_ref[...].astype(BF16), wa_ref[...], preferred_element_type=F32) + ba_ref[...]
    log_a = jax.nn.log_sigmoid(gate) / C_TAU
    ci = lax.broadcasted_iota(jnp.int32, (C_CHUNK, C_CHUNK), 0)
    cj = lax.broadcasted_iota(jnp.int32, (C_CHUNK, C_CHUNK), 1)
    causal = ci >= cj
    tri = jnp.where(causal, 1.0, 0.0).astype(BF16)
    rows = [slice(c * C_CHUNK, (c + 1) * C_CHUNK) for c in range(chunks)]
    for hd in range(C_HEADS):
        kl = slice(hd * C_DK, (hd + 1) * C_DK)
        vl = slice(hd * C_DV, (hd + 1) * C_DV)
        la = log_a[:, kl]
        hi = la.astype(BF16)
        rest = la - hi.astype(F32)
        mid = rest.astype(BF16)
        lo = (rest - mid.astype(F32)).astype(BF16)
        pieces = jnp.concatenate([hi, mid, lo], axis=1)
        sums = [jnp.dot(tri, pieces[sl], preferred_element_type=F32) for sl in rows]
        b = jnp.concatenate([s3[:, :C_DK] + s3[:, C_DK:2 * C_DK] + s3[:, 2 * C_DK:] for s3 in sums], axis=0)
        b3 = b.reshape(chunks, C_CHUNK, C_DK)
        b_last = b3[:, C_CHUNK - 1:C_CHUNK, :]
        k = k_ref[:, kl]
        q_t = ((q_ref[:, kl] * (C_DK ** -0.5)) * jnp.exp(b)).astype(BF16)
        k_t = (k * jnp.exp(-b)).astype(BF16)
        k_s = (k.reshape(chunks, C_CHUNK, C_DK) * jnp.exp(b_last - b3)).reshape(tc, C_DK).astype(BF16)
        decay = jnp.exp(b_last)
        o_intra, kv_t = [], []
        for sl in rows:
            v_c = v_ref[sl, vl].astype(BF16)
            attn = lax.dot_general(q_t[sl], k_t[sl], NT_DIMS, preferred_element_type=F32)
            attn = jnp.where(causal, attn, 0.0).astype(BF16)
            o_intra.append(jnp.dot(attn, v_c, preferred_element_type=F32))
            kv_t.append(lax.dot_general(v_c, k_s[sl], TN_DIMS, preferred_element_type=F32))
        st = st_ref[hd]
        entering = []
        for c in range(chunks):
            entering.append(st.astype(BF16))
            st = st * decay[c] + kv_t[c]
        st_ref[hd] = st
        for c, sl in enumerate(rows):
            o = o_intra[c] + lax.dot_general(q_t[sl], entering[c], NT_DIMS, preferred_element_type=F32)
            o_ref[sl, vl] = (_rms_rows(o, hg_ref[...]) * _silu(r_ref[sl, vl])).astype(BF16)


def _gla(z3, ga3, w_a2, b_a, head_g, *, tc):
    bsz, s, _ = z3.shape
    hk, hv = C_HEADS * C_DK, C_HEADS * C_DV
    wa = jnp.pad(w_a2, ((0, V7X_LANES - C_GATE_RANK), (0, 0))).astype(BF16)
    vmem = (2 * (2 * tc * hk * 4 + 3 * tc * hv * 4 + tc * V7X_LANES * 4 + V7X_LANES * hk * 2)
            + hv * C_DK * 4 + 12 * tc * C_DK * 4 + 2 * tc * hk * 4)
    return pl.pallas_call(
        _gla_kernel,
        grid=(bsz, s // tc),
        in_specs=[pl.BlockSpec((None, tc, hk), lambda b, t: (b, t, 0)),
                  pl.BlockSpec((None, tc, hk), lambda b, t: (b, t, 1)),
                  pl.BlockSpec((None, tc, hv), lambda b, t: (b, t, 1)),
                  pl.BlockSpec((None, tc, hv), lambda b, t: (b, t, 2)),
                  pl.BlockSpec((None, tc, V7X_LANES), lambda b, t: (b, t, 0)),
                  pl.BlockSpec((V7X_LANES, hk), lambda b, t: (0, 0)),
                  pl.BlockSpec((1, hk), lambda b, t: (0, 0)),
                  pl.BlockSpec((1, C_DV), lambda b, t: (0, 0))],
        out_specs=pl.BlockSpec((None, tc, hv), lambda b, t: (b, t, 0)),
        out_shape=jax.ShapeDtypeStruct((bsz, s, hv), BF16),
        scratch_shapes=[pltpu.VMEM((C_HEADS, C_DV, C_DK), F32)],
        compiler_params=_params(("parallel", "arbitrary"), vmem),
        name="gla",
    )(z3, z3, z3, z3, ga3, wa, b_a.reshape(1, -1), head_g.reshape(1, C_DV))


def _ffn_kernel(x_ref, a_ref, b_ref, wa_ref, wb_ref, g_ref, wg_ref, wu_ref, cw_ref, cb_ref, wd_ref, *rest,
                tiles_per_seq, tf, n_cast):
    cast_src, o_ref, cast_dst = rest[:n_cast], rest[n_cast], rest[n_cast + 1:2 * n_cast + 1]
    h_ref, act_ref, gate_ref, up_ref = rest[2 * n_cast + 1:]
    tm = x_ref.shape[0]
    halo_rows = V7X_SUBLANES
    for r in range(tm // ROW_SUB):
        rows = slice(r * ROW_SUB, (r + 1) * ROW_SUB)
        x1 = x_ref[rows, :] + jnp.dot(a_ref[rows, :].astype(BF16), wa_ref[...], preferred_element_type=F32)
        x1 = x1 + jnp.dot(b_ref[rows, :].astype(BF16), wb_ref[...], preferred_element_type=F32)
        o_ref[rows, :] = x1
        h_ref[rows, :] = _rms_rows(x1, g_ref[...]).astype(BF16)

    seq_start = pl.program_id(0) % tiles_per_seq == 0

    @pl.when(seq_start)
    def _():
        gate_ref[0:halo_rows, :] = jnp.zeros((halo_rows, gate_ref.shape[1]), F32)

    @pl.when(jnp.logical_not(seq_start))
    def _():
        gate_ref[0:halo_rows, :] = gate_ref[tm:tm + halo_rows, :]

    gate_ref[halo_rows:, :] = jnp.dot(h_ref[...], wg_ref[...], preferred_element_type=F32)
    up_ref[...] = jnp.dot(h_ref[...], wu_ref[...], preferred_element_type=F32)
    for j in range(wg_ref.shape[1] // tf):
        cols = slice(j * tf, (j + 1) * tf)
        conv = cb_ref[:, cols]
        for tap in range(CONV_W):
            conv = conv + gate_ref[pl.ds(halo_rows - (CONV_W - 1) + tap, tm), cols] * cw_ref[tap:tap + 1, cols]
        act_ref[:, cols] = (_silu(conv) * up_ref[:, cols]).astype(BF16)
    o_ref[...] += jnp.dot(act_ref[...], wd_ref[...], preferred_element_type=F32)
    for src, dst in zip(cast_src, cast_dst):
        dst[...] = src[...].astype(BF16)


def _ffn(x, a, a_blk, b, b_blk, w_out_all, lo, g, w_gate_all, w_up_all, conv_w, conv_b, w_down_all, lf,
         cast=(), *, seq, tm, tf):
    n, d = x.shape
    dff = w_gate_all.shape[2]
    kh = w_out_all.shape[1] // 2
    cast_in, cast_out, cast_shape, cast_vmem = _cast_plumbing(cast, n // tm, lambda i: i)
    vmem = (2 * (2 * tm * d * 4 + 2 * tm * kh * 4 + 2 * kh * d * 2 + 3 * d * dff * 2 + 4 * dff * 4)
            + tm * d * 2 + tm * dff * 2 + (2 * tm + 8) * dff * 4 + 4 * tm * tf * 4 + cast_vmem)
    outs = pl.pallas_call(
        functools.partial(_ffn_kernel, tiles_per_seq=seq // tm, tf=tf, n_cast=len(cast)),
        grid=(n // tm,),
        in_specs=[pl.BlockSpec((tm, d), lambda i: (i, 0)),
                  pl.BlockSpec((tm, kh), lambda i: (i, a_blk)),
                  pl.BlockSpec((tm, kh), lambda i: (i, b_blk)),
                  pl.BlockSpec((None, kh, d), lambda i: (lo, 0, 0)),
                  pl.BlockSpec((None, kh, d), lambda i: (lo, 1, 0)),
                  pl.BlockSpec((1, d), lambda i: (0, 0)),
                  _layer_spec(w_gate_all, lf),
                  _layer_spec(w_up_all, lf),
                  pl.BlockSpec((CONV_W, dff), lambda i: (0, 0)),
                  pl.BlockSpec((1, dff), lambda i: (0, 0)),
                  _layer_spec(w_down_all, lf)] + cast_in,
        out_specs=[pl.BlockSpec((tm, d), lambda i: (i, 0))] + cast_out,
        out_shape=[jax.ShapeDtypeStruct((n, d), F32)] + cast_shape,
        scratch_shapes=[pltpu.VMEM((tm, d), BF16), pltpu.VMEM((tm, dff), BF16),
                        pltpu.VMEM((V7X_SUBLANES + tm, dff), F32), pltpu.VMEM((tm, dff), F32)],
        compiler_params=_params(("arbitrary",), vmem),
        name="conv_ffn",
    )(x, a, b, w_out_all, w_out_all, g.reshape(1, d), w_gate_all, w_up_all, conv_w, conv_b.reshape(1, dff),
      w_down_all, *[arr for arr, _, _ in cast])
    return outs[0], outs[1:]


def kernel(x, norm_mix_g, norm_ffn_g, ev_w_in, ev_a_ln_g, ev_a_ln_b, ev_a_ws, ev_a_bs, ev_q_g, ev_k_g,
           ev_w_out, od_w_in, od_w_a2, od_b_a, od_head_g, od_w_out, ffn_w_gate, ffn_w_up, ffn_conv_w,
           ffn_conv_b, ffn_w_down):
    bsz, seq, d = x.shape
    n = bsz * seq
    depth = norm_mix_g.shape[0]
    main = 2 * C_HEADS * (C_DK + C_DV)
    ev_w_out, od_w_out = ev_w_out.astype(BF16), od_w_out.astype(BF16)
    od_w_side = jnp.pad(od_w_in[:, :, main:], ((0, 0), (0, 0), (0, V7X_LANES - C_GATE_RANK))).astype(BF16)
    ffn_f32 = (ffn_w_gate, ffn_w_up, ffn_w_down)
    od_w_in_t = jnp.swapaxes(od_w_in, 1, 2)

    def cast_for(layer):
        if layer >= depth:
            return []
        items = [(w, layer, w.shape[1]) for w in ffn_f32]
        if layer > 0:
            items.append((ev_w_in, layer // 2, d) if layer % 2 == 0 else (od_w_in_t, layer // 2, main))
        return items

    w_in = ev_w_in[:1].astype(BF16)
    ffn_w = None
    xf = x.reshape(n, d)
    for layer in range(depth):
        if layer % 2 == 0:
            a_out, qb, kn, vn, qp, kp, vp = _even_in(
                xf, norm_mix_g[layer], w_in, 0, ev_a_ln_g[layer // 2], ev_a_ln_b[layer // 2], ev_a_ws[layer // 2],
                ev_a_bs[layer // 2], ev_q_g[layer // 2], ev_k_g[layer // 2], bsz=bsz, tm=EVEN_IN_ROWS)
            in_seq = lambda t: t.reshape(bsz, seq, -1)
            first_cast = cast_for(layer) if layer == 0 else []
            b_out, cast_out = _dilated(in_seq(qb), in_seq(kn), in_seq(vn), qp, kp, vp, first_cast)
            if layer == 0:
                ffn_w = cast_out
            mix = (a_out, 0, b_out.reshape(n, -1), 0, ev_w_out, layer // 2)
        else:
            z, ga = _norm_matmul(xf, norm_mix_g[layer], w_in, 0, od_w_side[layer // 2],
                                 tm=ODD_IN_ROWS)
            mixed = _gla(z.reshape(bsz, seq, -1), ga.reshape(bsz, seq, -1), od_w_a2[layer // 2], od_b_a[layer // 2],
                         od_head_g[layer // 2], tc=GLA_ROWS).reshape(n, -1)
            mix = (mixed, 0, mixed, 1, od_w_out, layer // 2)
        xf, cast_out = _ffn(xf, *mix, norm_ffn_g[layer], ffn_w[0], ffn_w[1], ffn_conv_w[layer], ffn_conv_b[layer],
                            ffn_w[2], 0, cast_for(layer + 1), seq=seq, tm=FFN_ROWS,
                            tf=FFN_COLS)
        if layer + 1 < depth:
            ffn_w, w_in = cast_out[:3], cast_out[3]
    return xf.reshape(bsz, seq, d)
```

```python
import functools

import jax
import jax.numpy as jnp
from jax import lax
from jax.experimental import pallas as pl
from jax.experimental.pallas import tpu as pltpu

A_GROUPS = 8
A_GROUP_DIM = 64
A_CHUNK = 128
B_HEAD_DIM = 64
B_DILATIONS = (1, 4, 16)
B_BLOCK = 128
C_HEADS = 4
C_DK = 128
C_DV = 256
C_GATE_RANK = 16
C_TAU = 16.0
C_CHUNK = 64
CONV_W = 3
EPS = 1e-6
NEG = -1e30

V7X_LANES = 128
V7X_SUBLANES = 8
V7X_VMEM_BUDGET = 56 * 1024 * 1024
BF16_ROWS = 16

EVEN_IN_ROWS = 1024
ODD_IN_ROWS = 512
GLA_ROWS = 512
FFN_ROWS = 512
FFN_COLS = 256

F32 = jnp.float32
BF16 = jnp.bfloat16
NT_DIMS = (((1,), (1,)), ((), ()))
TN_DIMS = (((0,), (0,)), ((), ()))

ROW_SUB = 2 * A_CHUNK
B_PLANES = max(B_DILATIONS)
STAGE_PITCH = 24
B_SCORE_SCALE = B_HEAD_DIM ** -0.5 * 1.4426950408889634


def _params(semantics, vmem_bytes):
    return pltpu.CompilerParams(
        dimension_semantics=semantics,
        vmem_limit_bytes=min(int(vmem_bytes * 1.25) + (4 << 20), V7X_VMEM_BUDGET))


def _layer_spec(w_all, li):
    return pl.BlockSpec((None,) + w_all.shape[1:], lambda i: (li, 0, 0))


def _cast_plumbing(cast, steps, linear_step):
    ins, outs, shapes, vmem = [], [], [], 0
    for arr, layer, rows in cast:
        slabs = steps if rows // steps % BF16_ROWS == 0 else steps // 2
        slab = (None, rows // slabs, arr.shape[2])
        ins.append(pl.BlockSpec(
            slab, lambda *g, layer=layer, last=slabs - 1: (layer, jnp.minimum(linear_step(*g), last), 0)))
        outs.append(pl.BlockSpec(slab, lambda *g, last=slabs - 1: (0, jnp.minimum(linear_step(*g), last), 0)))
        shapes.append(jax.ShapeDtypeStruct((1, rows, arr.shape[2]), BF16))
        vmem += 2 * slab[1] * slab[2] * 6
    return ins, outs, shapes, vmem


def _rms_rows(x, g):
    return x * lax.rsqrt(jnp.mean(x * x, axis=-1, keepdims=True) + EPS) * g


def _gelu(x):
    return 0.5 * x * (1.0 + lax.erf(x * (0.5 ** 0.5)))


def _silu(x):
    return x * jax.nn.sigmoid(x)


def _head_pair_rms(x, g, head0):
    x2 = x * x
    s0 = jnp.sum(jnp.where(head0, x2, 0.0), axis=-1, keepdims=True)
    s1 = jnp.sum(jnp.where(head0, 0.0, x2), axis=-1, keepdims=True)
    ms = jnp.where(head0, s0, s1) * (1.0 / B_HEAD_DIM)
    return x * lax.rsqrt(ms + EPS) * g


def _norm_matmul_kernel(x_ref, g_ref, w_ref, ws_ref, o_ref, os_ref):
    for r in range(x_ref.shape[0] // ROW_SUB):
        rows = slice(r * ROW_SUB, (r + 1) * ROW_SUB)
        h = _rms_rows(x_ref[rows, :], g_ref[...]).astype(BF16)
        o_ref[rows, :] = lax.dot_general(h, w_ref[...], NT_DIMS, preferred_element_type=F32)
        os_ref[rows, :] = jnp.dot(h, ws_ref[...], preferred_element_type=F32)


def _norm_matmul(x, g, wt_all, li, w_side, *, tm):
    n, d = x.shape
    f, fs = wt_all.shape[1], w_side.shape[1]
    vmem = 2 * (tm * d * 4 + d * (f + fs) * 2 + tm * (f + fs) * 4) + 2 * ROW_SUB * f * 4
    return pl.pallas_call(
        _norm_matmul_kernel,
        grid=(n // tm,),
        in_specs=[pl.BlockSpec((tm, d), lambda i: (i, 0)),
                  pl.BlockSpec((1, d), lambda i: (0, 0)),
                  _layer_spec(wt_all, li),
                  pl.BlockSpec((d, fs), lambda i: (0, 0))],
        out_specs=(pl.BlockSpec((tm, f), lambda i: (i, 0)), pl.BlockSpec((tm, fs), lambda i: (i, 0))),
        out_shape=(jax.ShapeDtypeStruct((n, f), F32), jax.ShapeDtypeStruct((n, fs), F32)),
        compiler_params=_params(("parallel",), vmem),
        name="norm_matmul",
    )(x, g.reshape(1, d), wt_all, w_side)


def _even_in_kernel(x_ref, g_ref, w_ref, lng_ref, lnb_ref, ws_ref, bs_ref, qg_ref, kg_ref,
                    a_ref, qb_ref, kn_ref, vn_ref, qp_ref, kp_ref, vp_ref, h_ref, z_ref, stage_ref):
    tm = x_ref.shape[0]
    aw = a_ref.shape[1]
    bw = kn_ref.shape[1]
    tiles = bw // V7X_LANES
    groups = ROW_SUB // B_PLANES
    per_block = B_BLOCK // B_PLANES
    row = lax.broadcasted_iota(jnp.int32, (A_CHUNK, 2 * A_CHUNK), 0)
    col = lax.broadcasted_iota(jnp.int32, (A_CHUNK, 2 * A_CHUNK), 1)
    causal = (col % A_CHUNK) <= row
    lane2 = lax.broadcasted_iota(jnp.int32, (1, 2 * V7X_LANES), 1)
    first_group = (lane2 % V7X_LANES) < A_GROUP_DIM
    head0 = lax.broadcasted_iota(jnp.int32, (1, V7X_LANES), 1) < B_HEAD_DIM
    w_pairs = [jnp.where(causal, ws_ref[t], 0.0).astype(BF16) for t in range(ws_ref.shape[0])]
    for r in range(tm // ROW_SUB):
        rows = slice(r * ROW_SUB, (r + 1) * ROW_SUB)
        h_ref[rows, :] = _rms_rows(x_ref[rows, :], g_ref[...]).astype(BF16)
    z_ref[...] = jnp.dot(h_ref[...], w_ref[...], preferred_element_type=F32)
    for r in range(tm // ROW_SUB):
        base = r * ROW_SUB
        rows = slice(base, base + ROW_SUB)
        stage = stage_ref.at[r % 2]
        z = z_ref.at[rows]

        v = _gelu(z[:, aw:2 * aw])
        vc = v - jnp.mean(v, axis=-1, keepdims=True)
        vn = vc * lax.rsqrt(jnp.mean(vc * vc, axis=-1, keepdims=True) + EPS) * lng_ref[...] + lnb_ref[...]
        for t in range(aw // V7X_LANES):
            lanes = slice(t * V7X_LANES, (t + 1) * V7X_LANES)
            cc = jnp.concatenate([vn[:A_CHUNK, lanes], vn[A_CHUNK:, lanes]], axis=1)
            rhs = jnp.concatenate([jnp.where(first_group, cc, 0.0),
                                   jnp.where(first_group, 0.0, cc)], axis=0).astype(BF16)
            mixed = jnp.dot(w_pairs[t], rhs, preferred_element_type=F32)
            bias = bs_ref[:, lanes]
            for c in range(2):
                chunk = slice(c * A_CHUNK, (c + 1) * A_CHUNK)
                u = _gelu(z[chunk, lanes])
                a_ref[base + c * A_CHUNK:base + (c + 1) * A_CHUNK, lanes] = (
                    u * (mixed[:, c * V7X_LANES:(c + 1) * V7X_LANES] + bias)).astype(BF16)

        for t in range(tiles):
            lanes = slice(t * V7X_LANES, (t + 1) * V7X_LANES)
            q = _head_pair_rms(z[:, 2 * aw + t * V7X_LANES:2 * aw + (t + 1) * V7X_LANES], qg_ref[...], head0)
            q = q * B_SCORE_SCALE
            k = _head_pair_rms(z[:, 2 * aw + bw + t * V7X_LANES:2 * aw + bw + (t + 1) * V7X_LANES],
                               kg_ref[...], head0)
            vb = z[:, 2 * aw + 2 * bw + t * V7X_LANES:2 * aw + 2 * bw + (t + 1) * V7X_LANES]
            kn_ref[rows, lanes] = k.astype(BF16)
            vn_ref[rows, lanes] = vb.astype(BF16)
            for i, val in enumerate((q, k, vb)):
                for grp in range(groups):
                    stage[i * tiles + t, grp * STAGE_PITCH:grp * STAGE_PITCH + B_PLANES, :] = (
                        val[grp * B_PLANES:(grp + 1) * B_PLANES])
            for p in range(B_PLANES):
                sel = pl.ds(p, groups, stride=STAGE_PITCH)
                for i, planes in enumerate((qp_ref, kp_ref, vp_ref)):
                    planes[p, r * groups:(r + 1) * groups, lanes] = stage[i * tiles + t, sel, :].astype(BF16)
            for blk in range(ROW_SUB // B_BLOCK):
                first = blk * per_block * STAGE_PITCH
                slabs = [stage[t, pl.ds(first + p, per_block, stride=STAGE_PITCH), :] for p in range(B_PLANES)]
                qb_ref[base + blk * B_BLOCK:base + (blk + 1) * B_BLOCK, lanes] = (
                    jnp.concatenate(slabs, axis=0).astype(BF16))


def _even_in(x, g, w_all, li, ln_g, ln_b, w_s, b_s, q_g, k_g, *, bsz, tm):
    n, d = x.shape
    seq = n // bsz
    aw = A_GROUPS * A_GROUP_DIM
    f = w_all.shape[2]
    bw = (f - 2 * aw) // 3
    pairs = A_GROUPS // 2
    tiles = seq // tm
    per_plane = tm // B_PLANES
    ws_pairs = w_s.reshape(pairs, 2, A_CHUNK, A_CHUNK).transpose(0, 2, 1, 3).reshape(pairs, A_CHUNK, 2 * A_CHUNK)
    bias = jnp.repeat(b_s.T, A_GROUP_DIM, axis=1)
    qg2 = jnp.tile(q_g, 2).reshape(1, V7X_LANES)
    kg2 = jnp.tile(k_g, 2).reshape(1, V7X_LANES)
    nat_spec = pl.BlockSpec((tm, bw), lambda i: (i, 0))
    plane_spec = pl.BlockSpec((None, B_PLANES, per_plane, bw), lambda i: (i // tiles, 0, i % tiles, 0))
    nat_shape = jax.ShapeDtypeStruct((n, bw), BF16)
    plane_shape = jax.ShapeDtypeStruct((bsz, B_PLANES, seq // B_PLANES, bw), BF16)
    stage_rows = ROW_SUB // B_PLANES * STAGE_PITCH
    vmem = (2 * (tm * d * 4 + d * f * 2 + tm * aw * 2 + 6 * tm * bw * 2)
            + tm * d * 2 + tm * f * 4 + 2 * 3 * stage_rows * bw * 4
            + 2 * ROW_SUB * (2 * aw + 3 * bw) * 4)
    return pl.pallas_call(
        _even_in_kernel,
        grid=(n // tm,),
        in_specs=[pl.BlockSpec((tm, d), lambda i: (i, 0)),
                  pl.BlockSpec((1, d), lambda i: (0, 0)),
                  _layer_spec(w_all, li),
                  pl.BlockSpec((1, aw), lambda i: (0, 0)),
                  pl.BlockSpec((1, aw), lambda i: (0, 0)),
                  pl.BlockSpec((pairs, A_CHUNK, 2 * A_CHUNK), lambda i: (0, 0, 0)),
                  pl.BlockSpec((A_CHUNK, aw), lambda i: (0, 0)),
                  pl.BlockSpec((1, V7X_LANES), lambda i: (0, 0)),
                  pl.BlockSpec((1, V7X_LANES), lambda i: (0, 0))],
        out_specs=(pl.BlockSpec((tm, aw), lambda i: (i, 0)), nat_spec, nat_spec, nat_spec,
                   plane_spec, plane_spec, plane_spec),
        out_shape=(jax.ShapeDtypeStruct((n, aw), BF16), nat_shape, nat_shape, nat_shape,
                   plane_shape, plane_shape, plane_shape),
        scratch_shapes=[pltpu.VMEM((tm, d), BF16), pltpu.VMEM((tm, f), F32),
                        pltpu.VMEM((2, 3 * (bw // V7X_LANES), stage_rows, V7X_LANES), F32)],
        compiler_params=_params(("parallel",), vmem),
        name="even_in",
    )(x, g.reshape(1, d), w_all, ln_g.reshape(1, aw), ln_b.reshape(1, aw), ws_pairs, bias, qg2, kg2)


def _dilated_kernel(qb_ref, kn_ref, vn_ref, qp_ref, kp_ref, vp_ref, *rest, n_cast):
    cast_src, o_ref, cast_dst = rest[:n_cast], rest[n_cast], rest[n_cast + 1:2 * n_cast + 1]
    ob_ref, mb_ref, db_ref, mask_ref = rest[2 * n_cast + 1:]
    for src, dst in zip(cast_src, cast_dst):
        dst[...] = src[...].astype(BF16)
    m_rows = qp_ref.shape[1]
    blk2 = 2 * B_BLOCK
    head0 = lax.broadcasted_iota(jnp.int32, (1, V7X_LANES), 1) < B_HEAD_DIM

    rowi = lax.broadcasted_iota(jnp.int32, (blk2, blk2), 0) % B_BLOCK
    coli = lax.broadcasted_iota(jnp.int32, (blk2, blk2), 1)
    is_cur = coli >= B_BLOCK
    colj = coli % B_BLOCK
    for bi, d in enumerate(B_DILATIONS):
        planes = B_PLANES // d
        mb = B_BLOCK // planes
        i_pos = (rowi % mb) * planes + rowi // mb
        j_pos = colj if d == 1 else (colj % mb) * planes + colj // mb
        band = jnp.where(jnp.where(is_cur, i_pos - j_pos, j_pos - i_pos) >= 0, 0.0, NEG)
        mask_ref[2 * bi] = jnp.where(is_cur, band, NEG)
        mask_ref[2 * bi + 1] = band

    ones = jnp.ones((blk2, V7X_LANES), BF16)

    def attend(qb, kcat, vcat, bias):
        zero = jnp.zeros_like(qb)
        q2 = jnp.concatenate([jnp.where(head0, qb, zero), jnp.where(head0, zero, qb)], axis=0)
        s = lax.dot_general(q2, kcat, NT_DIMS, preferred_element_type=F32) + bias
        m = jnp.max(s, axis=-1, keepdims=True)
        p = jnp.exp2(s - m).astype(BF16)
        pv = jnp.dot(p, jnp.concatenate([vcat, ones], axis=1), preferred_element_type=F32)
        top, bot = pv[:B_BLOCK], pv[B_BLOCK:]
        return (jnp.where(head0, top[:, :V7X_LANES], bot[:, :V7X_LANES]),
                jnp.where(head0, m[:B_BLOCK], m[B_BLOCK:]),
                jnp.where(head0, top[:, V7X_LANES:], bot[:, V7X_LANES:]))

    def token_block(n, carry):
        per_block = B_BLOCK // B_PLANES
        off = pl.multiple_of(n * B_BLOCK, B_BLOCK)
        off_prev = pl.multiple_of(jnp.maximum(n - 1, 0) * B_BLOCK, B_BLOCK)
        off_plane = pl.multiple_of(n * per_block, per_block)
        kcat = jnp.concatenate([kn_ref[pl.ds(off_prev, B_BLOCK), :], kn_ref[pl.ds(off, B_BLOCK), :]], axis=0)
        vcat = jnp.concatenate([vn_ref[pl.ds(off_prev, B_BLOCK), :], vn_ref[pl.ds(off, B_BLOCK), :]], axis=0)
        o, m, den = attend(qb_ref[pl.ds(off, B_BLOCK), :], kcat, vcat, mask_ref[jnp.minimum(n, 1)])
        for p in range(B_PLANES):
            for ref, val in ((ob_ref, o), (mb_ref, m), (db_ref, den)):
                ref[0, p, pl.ds(off_plane, per_block), :] = val[p * per_block:(p + 1) * per_block]
        return carry

    lax.fori_loop(0, kn_ref.shape[0] // B_BLOCK, token_block, 0, unroll=True)

    for bi, d in enumerate(B_DILATIONS):
        if d == 1:
            continue
        planes = B_PLANES // d
        mb = B_BLOCK // planes
        nb = m_rows // mb

        def plane_block(blk, carry, bi=bi, d=d, planes=planes, mb=mb, nb=nb):
            r = blk // nb
            n = blk % nb
            off = pl.multiple_of(n * mb, mb)
            off_prev = pl.multiple_of(jnp.maximum(n - 1, 0) * mb, mb)

            def gather(ref, offs):
                return jnp.concatenate([ref[r + d * a, pl.ds(o, mb), :] for o in offs for a in range(planes)],
                                       axis=0)

            o, m, den = attend(gather(qp_ref, (off,)), gather(kp_ref, (off_prev, off)),
                               gather(vp_ref, (off_prev, off)), mask_ref[2 * bi + jnp.minimum(n, 1)])
            for a in range(planes):
                for ref, val in ((ob_ref, o), (mb_ref, m), (db_ref, den)):
                    ref[bi, r + d * a, pl.ds(off, mb), :] = val[a * mb:(a + 1) * mb]
            return carry

        lax.fori_loop(0, d * nb, plane_block, 0, unroll=True)

    branches = range(len(B_DILATIONS))
    for r in range(B_PLANES):
        mx = functools.reduce(jnp.maximum, [mb_ref[bi, r] for bi in branches])
        es = [jnp.exp2(mb_ref[bi, r] - mx) for bi in branches]
        num = functools.reduce(lambda a, b: a + b, [es[bi] * ob_ref[bi, r] for bi in branches])
        den = functools.reduce(lambda a, b: a + b, [es[bi] * db_ref[bi, r] for bi in branches])
        o_ref[pl.ds(r, m_rows, stride=B_PLANES), :] = num / den


def _dilated(qb, kn, vn, qp, kp, vp, cast=()):
    bsz, s, bw = kn.shape
    pairs = bw // V7X_LANES
    cast_in, cast_out, cast_shape, cast_vmem = _cast_plumbing(cast, bsz * pairs, lambda b, p: b * pairs + p)
    m_rows = s // B_PLANES
    tile_f32 = s * V7X_LANES * 4
    vmem = (2 * (6 * tile_f32 // 2 + tile_f32) + 9 * tile_f32 + 6 * 4 * B_BLOCK * B_BLOCK * 4 + 8 * tile_f32 // 16
            + cast_vmem)
    nat_spec = pl.BlockSpec((None, s, V7X_LANES), lambda b, p: (b, 0, p))
    plane_spec = pl.BlockSpec((None, B_PLANES, m_rows, V7X_LANES), lambda b, p: (b, 0, 0, p))
    outs = pl.pallas_call(
        functools.partial(_dilated_kernel, n_cast=len(cast)),
        grid=(bsz, pairs),
        in_specs=[nat_spec, nat_spec, nat_spec, plane_spec, plane_spec, plane_spec] + cast_in,
        out_specs=[pl.BlockSpec((None, s, V7X_LANES), lambda b, p: (b, 0, p))] + cast_out,
        out_shape=[jax.ShapeDtypeStruct((bsz, s, bw), F32)] + cast_shape,
        scratch_shapes=[pltpu.VMEM((len(B_DILATIONS), B_PLANES, m_rows, V7X_LANES), F32)] * 3
                       + [pltpu.VMEM((2 * len(B_DILATIONS), 2 * B_BLOCK, 2 * B_BLOCK), F32)],
        compiler_params=_params(("parallel", "parallel"), vmem),
        name="dilated_attention",
    )(qb, kn, vn, qp, kp, vp, *[arr for arr, _, _ in cast])
    return outs[0], outs[1:]


def _gla_kernel(q_ref, k_ref, v_ref, r_ref, ga_ref, wa_ref, ba_ref, hg_ref, o_ref, st_ref):
    tc = q_ref.shape[0]
    chunks = tc // C_CHUNK

    @pl.when(pl.program_id(1) == 0)
    def _():
        st_ref[...] = jnp.zeros_like(st_ref)

    gate = jnp.dot(ga_ref[...].astype(BF16), wa_ref[...], preferred_element_type=F32) + ba_ref[...]
    log_a = jax.nn.log_sigmoid(gate) / C_TAU
    ci = lax.broadcasted_iota(jnp.int32, (C_CHUNK, C_CHUNK), 0)
    cj = lax.broadcasted_iota(jnp.int32, (C_CHUNK, C_CHUNK), 1)
    causal = ci >= cj
    tri = jnp.where(causal, 1.0, 0.0).astype(BF16)
    rows = [slice(c * C_CHUNK, (c + 1) * C_CHUNK) for c in range(chunks)]
    for hd in range(C_HEADS):
        kl = slice(hd * C_DK, (hd + 1) * C_DK)
        vl = slice(hd * C_DV, (hd + 1) * C_DV)
        la = log_a[:, kl]
        hi = la.astype(BF16)
        rest = la - hi.astype(F32)
        mid = rest.astype(BF16)
        lo = (rest - mid.astype(F32)).astype(BF16)
        pieces = jnp.concatenate([hi, mid, lo], axis=1)
        sums = [jnp.dot(tri, pieces[sl], preferred_element_type=F32) for sl in rows]
        b = jnp.concatenate([s3[:, :C_DK] + s3[:, C_DK:2 * C_DK] + s3[:, 2 * C_DK:] for s3 in sums], axis=0)
        b3 = b.reshape(chunks, C_CHUNK, C_DK)
        b_last = b3[:, C_CHUNK - 1:C_CHUNK, :]
        k = k_ref[:, kl]
        q_t = ((q_ref[:, kl] * (C_DK ** -0.5)) * jnp.exp(b)).astype(BF16)
        k_t = (k * jnp.exp(-b)).astype(BF16)
        k_s = (k.reshape(chunks, C_CHUNK, C_DK) * jnp.exp(b_last - b3)).reshape(tc, C_DK).astype(BF16)
        decay = jnp.exp(b_last)
        o_intra, kv_t = [], []
        for sl in rows:
            v_c = v_ref[sl, vl].astype(BF16)
            attn = lax.dot_general(q_t[sl], k_t[sl], NT_DIMS, preferred_element_type=F32)
            attn = jnp.where(causal, attn, 0.0).astype(BF16)
            o_intra.append(jnp.dot(attn, v_c, preferred_element_type=F32))
            kv_t.append(lax.dot_general(v_c, k_s[sl], TN_DIMS, preferred_element_type=F32))
        st = st_ref[hd]
        entering = []
        for c in range(chunks):
            entering.append(st.astype(BF16))
            st = st * decay[c] + kv_t[c]
        st_ref[hd] = st
        for c, sl in enumerate(rows):
            o = o_intra[c] + lax.dot_general(q_t[sl], entering[c], NT_DIMS, preferred_element_type=F32)
            o_ref[sl, vl] = (_rms_rows(o, hg_ref[...]) * _silu(r_ref[sl, vl])).astype(BF16)


def _gla(z3, ga3, w_a2, b_a, head_g, *, tc):
    bsz, s, _ = z3.shape
    hk, hv = C_HEADS * C_DK, C_HEADS * C_DV
    wa = jnp.pad(w_a2, ((0, V7X_LANES - C_GATE_RANK), (0, 0))).astype(BF16)
    vmem = (2 * (2 * tc * hk * 4 + 3 * tc * hv * 4 + tc * V7X_LANES * 4 + V7X_LANES * hk * 2)
            + hv * C_DK * 4 + 12 * tc * C_DK * 4 + 2 * tc * hk * 4)
    return pl.pallas_call(
        _gla_kernel,
        grid=(bsz, s // tc),
        in_specs=[pl.BlockSpec((None, tc, hk), lambda b, t: (b, t, 0)),
                  pl.BlockSpec((None, tc, hk), lambda b, t: (b, t, 1)),
                  pl.BlockSpec((None, tc, hv), lambda b, t: (b, t, 1)),
                  pl.BlockSpec((None, tc, hv), lambda b, t: (b, t, 2)),
                  pl.BlockSpec((None, tc, V7X_LANES), lambda b, t: (b, t, 0)),
                  pl.BlockSpec((V7X_LANES, hk), lambda b, t: (0, 0)),
                  pl.BlockSpec((1, hk), lambda b, t: (0, 0)),
                  pl.BlockSpec((1, C_DV), lambda b, t: (0, 0))],
        out_specs=pl.BlockSpec((None, tc, hv), lambda b, t: (b, t, 0)),
        out_shape=jax.ShapeDtypeStruct((bsz, s, hv), BF16),
        scratch_shapes=[pltpu.VMEM((C_HEADS, C_DV, C_DK), F32)],
        compiler_params=_params(("parallel", "arbitrary"), vmem),
        name="gla",
    )(z3, z3, z3, z3, ga3, wa, b_a.reshape(1, -1), head_g.reshape(1, C_DV))


def _ffn_kernel(x_ref, a_ref, b_ref, wa_ref, wb_ref, g_ref, wg_ref, wu_ref, cw_ref, cb_ref, wd_ref, *rest,
                tiles_per_seq, tf, n_cast):
    cast_src, o_ref, cast_dst = rest[:n_cast], rest[n_cast], rest[n_cast + 1:2 * n_cast + 1]
    h_ref, act_ref, gate_ref, up_ref = rest[2 * n_cast + 1:]
    tm = x_ref.shape[0]
    halo_rows = V7X_SUBLANES
    seq_start = pl.program_id(0) % tiles_per_seq == 0

    @pl.when(seq_start)
    def _():
        gate_ref[0:halo_rows, :] = jnp.zeros((halo_rows, gate_ref.shape[1]), F32)

    @pl.when(jnp.logical_not(seq_start))
    def _():
        gate_ref[0:halo_rows, :] = gate_ref[tm:tm + halo_rows, :]

    for r in range(tm // ROW_SUB):
        rows = slice(r * ROW_SUB, (r + 1) * ROW_SUB)
        x1 = x_ref[rows, :] + jnp.dot(a_ref[rows, :].astype(BF16), wa_ref[...], preferred_element_type=F32)
        x1 = x1 + jnp.dot(b_ref[rows, :].astype(BF16), wb_ref[...], preferred_element_type=F32)
        o_ref[rows, :] = x1
        h_ref[rows, :] = _rms_rows(x1, g_ref[...]).astype(BF16)

    for r in range(tm // ROW_SUB):
        rows = slice(r * ROW_SUB, (r + 1) * ROW_SUB)
        gate_ref[halo_rows + r * ROW_SUB:halo_rows + (r + 1) * ROW_SUB, :] = jnp.dot(
            h_ref[rows, :], wg_ref[...], preferred_element_type=F32)
    up_ref[...] = jnp.dot(h_ref[...], wu_ref[...], preferred_element_type=F32)
    for j in range(wg_ref.shape[1] // tf):
        cols = slice(j * tf, (j + 1) * tf)
        conv = cb_ref[:, cols]
        for tap in range(CONV_W):
            conv = conv + gate_ref[pl.ds(halo_rows - (CONV_W - 1) + tap, tm), cols] * cw_ref[tap:tap + 1, cols]
        act_ref[:, cols] = (_silu(conv) * up_ref[:, cols]).astype(BF16)
    o_ref[...] += jnp.dot(act_ref[...], wd_ref[...], preferred_element_type=F32)
    for src, dst in zip(cast_src, cast_dst):
        dst[...] = src[...].astype(BF16)


def _ffn(x, a, a_blk, b, b_blk, w_out_all, lo, g, w_gate_all, w_up_all, conv_w, conv_b, w_down_all, lf,
         cast=(), *, seq, tm, tf):
    n, d = x.shape
    dff = w_gate_all.shape[2]
    kh = w_out_all.shape[1] // 2
    cast_in, cast_out, cast_shape, cast_vmem = _cast_plumbing(cast, n // tm, lambda i: i)
    vmem = (2 * (2 * tm * d * 4 + 2 * tm * kh * 4 + 2 * kh * d * 2 + 3 * d * dff * 2 + 4 * dff * 4)
            + tm * d * 2 + tm * dff * 2 + (2 * tm + 8) * dff * 4 + 4 * tm * tf * 4 + cast_vmem)
    outs = pl.pallas_call(
        functools.partial(_ffn_kernel, tiles_per_seq=seq // tm, tf=tf, n_cast=len(cast)),
        grid=(n // tm,),
        in_specs=[pl.BlockSpec((tm, d), lambda i: (i, 0)),
                  pl.BlockSpec((tm, kh), lambda i: (i, a_blk)),
                  pl.BlockSpec((tm, kh), lambda i: (i, b_blk)),
                  pl.BlockSpec((None, kh, d), lambda i: (lo, 0, 0)),
                  pl.BlockSpec((None, kh, d), lambda i: (lo, 1, 0)),
                  pl.BlockSpec((1, d), lambda i: (0, 0)),
                  _layer_spec(w_gate_all, lf),
                  _layer_spec(w_up_all, lf),
                  pl.BlockSpec((CONV_W, dff), lambda i: (0, 0)),
                  pl.BlockSpec((1, dff), lambda i: (0, 0)),
                  _layer_spec(w_down_all, lf)] + cast_in,
        out_specs=[pl.BlockSpec((tm, d), lambda i: (i, 0))] + cast_out,
        out_shape=[jax.ShapeDtypeStruct((n, d), F32)] + cast_shape,
        scratch_shapes=[pltpu.VMEM((tm, d), BF16), pltpu.VMEM((tm, dff), BF16),
                        pltpu.VMEM((V7X_SUBLANES + tm, dff), F32), pltpu.VMEM((tm, dff), F32)],
        compiler_params=_params(("arbitrary",), vmem),
        name="conv_ffn",
    )(x, a, b, w_out_all, w_out_all, g.reshape(1, d), w_gate_all, w_up_all, conv_w, conv_b.reshape(1, dff),
      w_down_all, *[arr for arr, _, _ in cast])
    return outs[0], outs[1:]


def kernel(x, norm_mix_g, norm_ffn_g, ev_w_in, ev_a_ln_g, ev_a_ln_b, ev_a_ws, ev_a_bs, ev_q_g, ev_k_g,
           ev_w_out, od_w_in, od_w_a2, od_b_a, od_head_g, od_w_out, ffn_w_gate, ffn_w_up, ffn_conv_w,
           ffn_conv_b, ffn_w_down):
    bsz, seq, d = x.shape
    n = bsz * seq
    depth = norm_mix_g.shape[0]
    main = 2 * C_HEADS * (C_DK + C_DV)
    ev_w_out, od_w_out = ev_w_out.astype(BF16), od_w_out.astype(BF16)
    od_w_side = jnp.pad(od_w_in[:, :, main:], ((0, 0), (0, 0), (0, V7X_LANES - C_GATE_RANK))).astype(BF16)
    ffn_f32 = (ffn_w_gate, ffn_w_up, ffn_w_down)
    od_w_in_t = jnp.swapaxes(od_w_in, 1, 2)

    def cast_for(layer):
        if layer >= depth:
            return []
        items = [(w, layer, w.shape[1]) for w in ffn_f32]
        if layer > 0:
            items.append((ev_w_in, layer // 2, d) if layer % 2 == 0 else (od_w_in_t, layer // 2, main))
        return items

    w_in = ev_w_in[:1].astype(BF16)
    ffn_w = None
    xf = x.reshape(n, d)
    for layer in range(depth):
        if layer % 2 == 0:
            a_out, qb, kn, vn, qp, kp, vp = _even_in(
                xf, norm_mix_g[layer], w_in, 0, ev_a_ln_g[layer // 2], ev_a_ln_b[layer // 2], ev_a_ws[layer // 2],
                ev_a_bs[layer // 2], ev_q_g[layer // 2], ev_k_g[layer // 2], bsz=bsz, tm=EVEN_IN_ROWS)
            in_seq = lambda t: t.reshape(bsz, seq, -1)
            first_cast = cast_for(layer) if layer == 0 else []
            b_out, cast_out = _dilated(in_seq(qb), in_seq(kn), in_seq(vn), qp, kp, vp, first_cast)
            if layer == 0:
                ffn_w = cast_out
            mix = (a_out, 0, b_out.reshape(n, -1), 0, ev_w_out, layer // 2)
        else:
            z, ga = _norm_matmul(xf, norm_mix_g[layer], w_in, 0, od_w_side[layer // 2],
                                 tm=ODD_IN_ROWS)
            mixed = _gla(z.reshape(bsz, seq, -1), ga.reshape(bsz, seq, -1), od_w_a2[layer // 2], od_b_a[layer // 2],
                         od_head_g[layer // 2], tc=GLA_ROWS).reshape(n, -1)
            mix = (mixed, 0, mixed, 1, od_w_out, layer // 2)
        xf, cast_out = _ffn(xf, *mix, norm_ffn_g[layer], ffn_w[0], ffn_w[1], ffn_conv_w[layer], ffn_conv_b[layer],
                            ffn_w[2], 0, cast_for(layer + 1), seq=seq, tm=FFN_ROWS,
                            tf=FFN_COLS)
        if layer + 1 < depth:
            ffn_w, w_in = cast_out[:3], cast_out[3]
    return xf.reshape(bsz, seq, d)
```

```python
import functools

import jax
import jax.numpy as jnp
from jax import lax
from jax.experimental import pallas as pl
from jax.experimental.pallas import tpu as pltpu

A_GROUPS = 8
A_GROUP_DIM = 64
A_CHUNK = 128
B_HEAD_DIM = 64
B_DILATIONS = (1, 4, 16)
B_BLOCK = 128
C_HEADS = 4
C_DK = 128
C_DV = 256
C_GATE_RANK = 16
C_TAU = 16.0
C_CHUNK = 64
CONV_W = 3
EPS = 1e-6
NEG = -1e30

V7X_LANES = 128
V7X_SUBLANES = 8
V7X_VMEM_BUDGET = 56 * 1024 * 1024
BF16_ROWS = 16

EVEN_IN_ROWS = 1024
GLA_ROWS = 512
FFN_ROWS = 512
FFN_COLS = 256

F32 = jnp.float32
BF16 = jnp.bfloat16
NT_DIMS = (((1,), (1,)), ((), ()))
TN_DIMS = (((0,), (0,)), ((), ()))

ROW_SUB = 2 * A_CHUNK
B_PLANES = max(B_DILATIONS)
STAGE_PITCH = 24
B_SCORE_SCALE = B_HEAD_DIM ** -0.5 * 1.4426950408889634


def _params(semantics, vmem_bytes):
    return pltpu.CompilerParams(
        dimension_semantics=semantics,
        vmem_limit_bytes=min(int(vmem_bytes * 1.25) + (4 << 20), V7X_VMEM_BUDGET))


def _layer_spec(w_all, li):
    return pl.BlockSpec((None,) + w_all.shape[1:], lambda *_: (li, 0, 0))


def _cast_plumbing(cast, steps, linear_step):
    ins, outs, shapes, vmem = [], [], [], 0
    for arr, layer, rows in cast:
        slabs = steps if rows // steps % BF16_ROWS == 0 else steps // 2
        slab = (None, rows // slabs, arr.shape[2])
        ins.append(pl.BlockSpec(
            slab, lambda *g, layer=layer, last=slabs - 1: (layer, jnp.minimum(linear_step(*g), last), 0)))
        outs.append(pl.BlockSpec(slab, lambda *g, last=slabs - 1: (0, jnp.minimum(linear_step(*g), last), 0)))
        shapes.append(jax.ShapeDtypeStruct((1, rows, arr.shape[2]), BF16))
        vmem += 2 * slab[1] * slab[2] * 6
    return ins, outs, shapes, vmem


def _rms_rows(x, g):
    return x * lax.rsqrt(jnp.mean(x * x, axis=-1, keepdims=True) + EPS) * g


def _gelu(x):
    return 0.5 * x * (1.0 + lax.erf(x * (0.5 ** 0.5)))


def _silu(x):
    return x * jax.nn.sigmoid(x)


def _head_pair_rms(x, g, head0):
    x2 = x * x
    s0 = jnp.sum(jnp.where(head0, x2, 0.0), axis=-1, keepdims=True)
    s1 = jnp.sum(jnp.where(head0, 0.0, x2), axis=-1, keepdims=True)
    ms = jnp.where(head0, s0, s1) * (1.0 / B_HEAD_DIM)
    return x * lax.rsqrt(ms + EPS) * g


def _even_in_kernel(x_ref, g_ref, w_ref, lng_ref, lnb_ref, ws_ref, bs_ref, qg_ref, kg_ref,
                    a_ref, qb_ref, kn_ref, vn_ref, qp_ref, kp_ref, vp_ref, h_ref, z_ref, stage_ref):
    tm = x_ref.shape[0]
    aw = a_ref.shape[1]
    bw = kn_ref.shape[1]
    tiles = bw // V7X_LANES
    groups = ROW_SUB // B_PLANES
    per_block = B_BLOCK // B_PLANES
    row = lax.broadcasted_iota(jnp.int32, (A_CHUNK, 2 * A_CHUNK), 0)
    col = lax.broadcasted_iota(jnp.int32, (A_CHUNK, 2 * A_CHUNK), 1)
    causal = (col % A_CHUNK) <= row
    lane2 = lax.broadcasted_iota(jnp.int32, (1, 2 * V7X_LANES), 1)
    first_group = (lane2 % V7X_LANES) < A_GROUP_DIM
    head0 = lax.broadcasted_iota(jnp.int32, (1, V7X_LANES), 1) < B_HEAD_DIM
    w_pairs = [jnp.where(causal, ws_ref[t], 0.0).astype(BF16) for t in range(ws_ref.shape[0])]
    for r in range(tm // ROW_SUB):
        rows = slice(r * ROW_SUB, (r + 1) * ROW_SUB)
        h_ref[rows, :] = _rms_rows(x_ref[rows, :], g_ref[...]).astype(BF16)
    z_ref[...] = jnp.dot(h_ref[...], w_ref[...], preferred_element_type=F32)
    for r in range(tm // ROW_SUB):
        base = r * ROW_SUB
        rows = slice(base, base + ROW_SUB)
        stage = stage_ref.at[r % 2]
        z = z_ref.at[rows]

        v = _gelu(z[:, aw:2 * aw])
        vc = v - jnp.mean(v, axis=-1, keepdims=True)
        vn = vc * lax.rsqrt(jnp.mean(vc * vc, axis=-1, keepdims=True) + EPS) * lng_ref[...] + lnb_ref[...]
        for t in range(aw // V7X_LANES):
            lanes = slice(t * V7X_LANES, (t + 1) * V7X_LANES)
            cc = jnp.concatenate([vn[:A_CHUNK, lanes], vn[A_CHUNK:, lanes]], axis=1)
            rhs = jnp.concatenate([jnp.where(first_group, cc, 0.0),
                                   jnp.where(first_group, 0.0, cc)], axis=0).astype(BF16)
            mixed = jnp.dot(w_pairs[t], rhs, preferred_element_type=F32)
            bias = bs_ref[:, lanes]
            for c in range(2):
                chunk = slice(c * A_CHUNK, (c + 1) * A_CHUNK)
                u = _gelu(z[chunk, lanes])
                a_ref[base + c * A_CHUNK:base + (c + 1) * A_CHUNK, lanes] = (
                    u * (mixed[:, c * V7X_LANES:(c + 1) * V7X_LANES] + bias)).astype(BF16)

        for t in range(tiles):
            lanes = slice(t * V7X_LANES, (t + 1) * V7X_LANES)
            q = _head_pair_rms(z[:, 2 * aw + t * V7X_LANES:2 * aw + (t + 1) * V7X_LANES], qg_ref[...], head0)
            q = q * B_SCORE_SCALE
            k = _head_pair_rms(z[:, 2 * aw + bw + t * V7X_LANES:2 * aw + bw + (t + 1) * V7X_LANES],
                               kg_ref[...], head0)
            vb = z[:, 2 * aw + 2 * bw + t * V7X_LANES:2 * aw + 2 * bw + (t + 1) * V7X_LANES]
            kn_ref[rows, lanes] = k.astype(BF16)
            vn_ref[rows, lanes] = vb.astype(BF16)
            for i, val in enumerate((q, k, vb)):
                for grp in range(groups):
                    stage[i * tiles + t, grp * STAGE_PITCH:grp * STAGE_PITCH + B_PLANES, :] = (
                        val[grp * B_PLANES:(grp + 1) * B_PLANES])
            for p in range(B_PLANES):
                sel = pl.ds(p, groups, stride=STAGE_PITCH)
                for i, planes in enumerate((qp_ref, kp_ref, vp_ref)):
                    planes[p, r * groups:(r + 1) * groups, lanes] = stage[i * tiles + t, sel, :].astype(BF16)
            for blk in range(ROW_SUB // B_BLOCK):
                first = blk * per_block * STAGE_PITCH
                slabs = [stage[t, pl.ds(first + p, per_block, stride=STAGE_PITCH), :] for p in range(B_PLANES)]
                qb_ref[base + blk * B_BLOCK:base + (blk + 1) * B_BLOCK, lanes] = (
                    jnp.concatenate(slabs, axis=0).astype(BF16))


def _even_in(x, g, w_all, li, ln_g, ln_b, w_s, b_s, q_g, k_g, *, bsz, tm):
    n, d = x.shape
    seq = n // bsz
    aw = A_GROUPS * A_GROUP_DIM
    f = w_all.shape[2]
    bw = (f - 2 * aw) // 3
    pairs = A_GROUPS // 2
    tiles = seq // tm
    per_plane = tm // B_PLANES
    ws_pairs = w_s.reshape(pairs, 2, A_CHUNK, A_CHUNK).transpose(0, 2, 1, 3).reshape(pairs, A_CHUNK, 2 * A_CHUNK)
    bias = jnp.repeat(b_s.T, A_GROUP_DIM, axis=1)
    qg2 = jnp.tile(q_g, 2).reshape(1, V7X_LANES)
    kg2 = jnp.tile(k_g, 2).reshape(1, V7X_LANES)
    nat_spec = pl.BlockSpec((tm, bw), lambda i: (i, 0))
    plane_spec = pl.BlockSpec((None, B_PLANES, per_plane, bw), lambda i: (i // tiles, 0, i % tiles, 0))
    nat_shape = jax.ShapeDtypeStruct((n, bw), BF16)
    plane_shape = jax.ShapeDtypeStruct((bsz, B_PLANES, seq // B_PLANES, bw), BF16)
    stage_rows = ROW_SUB // B_PLANES * STAGE_PITCH
    vmem = (2 * (tm * d * 4 + d * f * 2 + tm * aw * 2 + 6 * tm * bw * 2)
            + tm * d * 2 + tm * f * 4 + 2 * 3 * stage_rows * bw * 4
            + 2 * ROW_SUB * (2 * aw + 3 * bw) * 4)
    return pl.pallas_call(
        _even_in_kernel,
        grid=(n // tm,),
        in_specs=[pl.BlockSpec((tm, d), lambda i: (i, 0)),
                  pl.BlockSpec((1, d), lambda i: (0, 0)),
                  _layer_spec(w_all, li),
                  pl.BlockSpec((1, aw), lambda i: (0, 0)),
                  pl.BlockSpec((1, aw), lambda i: (0, 0)),
                  pl.BlockSpec((pairs, A_CHUNK, 2 * A_CHUNK), lambda i: (0, 0, 0)),
                  pl.BlockSpec((A_CHUNK, aw), lambda i: (0, 0)),
                  pl.BlockSpec((1, V7X_LANES), lambda i: (0, 0)),
                  pl.BlockSpec((1, V7X_LANES), lambda i: (0, 0))],
        out_specs=(pl.BlockSpec((tm, aw), lambda i: (i, 0)), nat_spec, nat_spec, nat_spec,
                   plane_spec, plane_spec, plane_spec),
        out_shape=(jax.ShapeDtypeStruct((n, aw), BF16), nat_shape, nat_shape, nat_shape,
                   plane_shape, plane_shape, plane_shape),
        scratch_shapes=[pltpu.VMEM((tm, d), BF16), pltpu.VMEM((tm, f), F32),
                        pltpu.VMEM((2, 3 * (bw // V7X_LANES), stage_rows, V7X_LANES), F32)],
        compiler_params=_params(("parallel",), vmem),
        name="even_in",
    )(x, g.reshape(1, d), w_all, ln_g.reshape(1, aw), ln_b.reshape(1, aw), ws_pairs, bias, qg2, kg2)


def _dilated_kernel(qb_ref, kn_ref, vn_ref, qp_ref, kp_ref, vp_ref, *rest, n_cast):
    cast_src, o_ref, cast_dst = rest[:n_cast], rest[n_cast], rest[n_cast + 1:2 * n_cast + 1]
    ob_ref, mb_ref, db_ref, mask_ref = rest[2 * n_cast + 1:]
    for src, dst in zip(cast_src, cast_dst):
        dst[...] = src[...].astype(BF16)
    m_rows = qp_ref.shape[1]
    blk2 = 2 * B_BLOCK
    head0 = lax.broadcasted_iota(jnp.int32, (1, V7X_LANES), 1) < B_HEAD_DIM

    rowi = lax.broadcasted_iota(jnp.int32, (blk2, blk2), 0) % B_BLOCK
    coli = lax.broadcasted_iota(jnp.int32, (blk2, blk2), 1)
    is_cur = coli >= B_BLOCK
    colj = coli % B_BLOCK
    for bi, d in enumerate(B_DILATIONS):
        planes = B_PLANES // d
        mb = B_BLOCK // planes
        i_pos = (rowi % mb) * planes + rowi // mb
        j_pos = colj if d == 1 else (colj % mb) * planes + colj // mb
        band = jnp.where(jnp.where(is_cur, i_pos - j_pos, j_pos - i_pos) >= 0, 0.0, NEG)
        mask_ref[2 * bi] = jnp.where(is_cur, band, NEG)
        mask_ref[2 * bi + 1] = band

    ones = jnp.ones((blk2, V7X_LANES), BF16)

    def attend(qb, kcat, vcat, bias):
        zero = jnp.zeros_like(qb)
        q2 = jnp.concatenate([jnp.where(head0, qb, zero), jnp.where(head0, zero, qb)], axis=0)
        s = lax.dot_general(q2, kcat, NT_DIMS, preferred_element_type=F32) + bias
        m = jnp.max(s, axis=-1, keepdims=True)
        p = jnp.exp2(s - m).astype(BF16)
        pv = jnp.dot(p, jnp.concatenate([vcat, ones], axis=1), preferred_element_type=F32)
        top, bot = pv[:B_BLOCK], pv[B_BLOCK:]
        return (jnp.where(head0, top[:, :V7X_LANES], bot[:, :V7X_LANES]),
                jnp.where(head0, m[:B_BLOCK], m[B_BLOCK:]),
                jnp.where(head0, top[:, V7X_LANES:], bot[:, V7X_LANES:]))

    def token_block(n, carry):
        per_block = B_BLOCK // B_PLANES
        off = pl.multiple_of(n * B_BLOCK, B_BLOCK)
        off_prev = pl.multiple_of(jnp.maximum(n - 1, 0) * B_BLOCK, B_BLOCK)
        off_plane = pl.multiple_of(n * per_block, per_block)
        kcat = jnp.concatenate([kn_ref[pl.ds(off_prev, B_BLOCK), :], kn_ref[pl.ds(off, B_BLOCK), :]], axis=0)
        vcat = jnp.concatenate([vn_ref[pl.ds(off_prev, B_BLOCK), :], vn_ref[pl.ds(off, B_BLOCK), :]], axis=0)
        o, m, den = attend(qb_ref[pl.ds(off, B_BLOCK), :], kcat, vcat, mask_ref[jnp.minimum(n, 1)])
        for p in range(B_PLANES):
            for ref, val in ((ob_ref, o), (mb_ref, m), (db_ref, den)):
                ref[0, p, pl.ds(off_plane, per_block), :] = val[p * per_block:(p + 1) * per_block]
        return carry

    lax.fori_loop(0, kn_ref.shape[0] // B_BLOCK, token_block, 0, unroll=True)

    for bi, d in enumerate(B_DILATIONS):
        if d == 1:
            continue
        planes = B_PLANES // d
        mb = B_BLOCK // planes
        nb = m_rows // mb

        def plane_block(blk, carry, bi=bi, d=d, planes=planes, mb=mb, nb=nb):
            r = blk // nb
            n = blk % nb
            off = pl.multiple_of(n * mb, mb)
            off_prev = pl.multiple_of(jnp.maximum(n - 1, 0) * mb, mb)

            def gather(ref, offs):
                return jnp.concatenate([ref[r + d * a, pl.ds(o, mb), :] for o in offs for a in range(planes)],
                                       axis=0)

            o, m, den = attend(gather(qp_ref, (off,)), gather(kp_ref, (off_prev, off)),
                               gather(vp_ref, (off_prev, off)), mask_ref[2 * bi + jnp.minimum(n, 1)])
            for a in range(planes):
                for ref, val in ((ob_ref, o), (mb_ref, m), (db_ref, den)):
                    ref[bi, r + d * a, pl.ds(off, mb), :] = val[a * mb:(a + 1) * mb]
            return carry

        lax.fori_loop(0, d * nb, plane_block, 0, unroll=True)

    branches = range(len(B_DILATIONS))
    for r in range(B_PLANES):
        mx = functools.reduce(jnp.maximum, [mb_ref[bi, r] for bi in branches])
        es = [jnp.exp2(mb_ref[bi, r] - mx) for bi in branches]
        num = functools.reduce(lambda a, b: a + b, [es[bi] * ob_ref[bi, r] for bi in branches])
        den = functools.reduce(lambda a, b: a + b, [es[bi] * db_ref[bi, r] for bi in branches])
        o_ref[pl.ds(r, m_rows, stride=B_PLANES), :] = num / den


def _dilated(qb, kn, vn, qp, kp, vp, cast=()):
    bsz, s, bw = kn.shape
    pairs = bw // V7X_LANES
    cast_in, cast_out, cast_shape, cast_vmem = _cast_plumbing(cast, bsz * pairs, lambda b, p: b * pairs + p)
    m_rows = s // B_PLANES
    tile_f32 = s * V7X_LANES * 4
    vmem = (2 * (6 * tile_f32 // 2 + tile_f32) + 9 * tile_f32 + 6 * 4 * B_BLOCK * B_BLOCK * 4 + 8 * tile_f32 // 16
            + cast_vmem)
    nat_spec = pl.BlockSpec((None, s, V7X_LANES), lambda b, p: (b, 0, p))
    plane_spec = pl.BlockSpec((None, B_PLANES, m_rows, V7X_LANES), lambda b, p: (b, 0, 0, p))
    outs = pl.pallas_call(
        functools.partial(_dilated_kernel, n_cast=len(cast)),
        grid=(bsz, pairs),
        in_specs=[nat_spec, nat_spec, nat_spec, plane_spec, plane_spec, plane_spec] + cast_in,
        out_specs=[pl.BlockSpec((None, s, V7X_LANES), lambda b, p: (b, 0, p))] + cast_out,
        out_shape=[jax.ShapeDtypeStruct((bsz, s, bw), F32)] + cast_shape,
        scratch_shapes=[pltpu.VMEM((len(B_DILATIONS), B_PLANES, m_rows, V7X_LANES), F32)] * 3
                       + [pltpu.VMEM((2 * len(B_DILATIONS), 2 * B_BLOCK, 2 * B_BLOCK), F32)],
        compiler_params=_params(("parallel", "parallel"), vmem),
        name="dilated_attention",
    )(qb, kn, vn, qp, kp, vp, *[arr for arr, _, _ in cast])
    return outs[0], outs[1:]


def _gla_kernel(x_ref, g_ref, w_ref, ws_ref, wa_ref, ba_ref, hg_ref, o_ref, st_ref, h_ref, z_ref):
    tc = x_ref.shape[0]
    chunks = tc // C_CHUNK
    hk, hv = C_HEADS * C_DK, C_HEADS * C_DV
    q_ref, k_ref = z_ref.at[:, 0:hk], z_ref.at[:, hk:2 * hk]
    v_ref, r_ref = z_ref.at[:, 2 * hk:2 * hk + hv], z_ref.at[:, 2 * hk + hv:2 * hk + 2 * hv]

    @pl.when(pl.program_id(1) == 0)
    def _():
        st_ref[...] = jnp.zeros_like(st_ref)

    for r in range(tc // ROW_SUB):
        rows = slice(r * ROW_SUB, (r + 1) * ROW_SUB)
        h_ref[rows, :] = _rms_rows(x_ref[rows, :], g_ref[...]).astype(BF16)
    ga = jnp.dot(h_ref[...], ws_ref[...], preferred_element_type=F32)
    gate = jnp.dot(ga.astype(BF16), wa_ref[...], preferred_element_type=F32) + ba_ref[...]
    for pair in range(C_HEADS // 2):
        for base, width in ((0, C_DK), (hk, C_DK), (2 * hk, C_DV), (2 * hk + hv, C_DV)):
            cols = slice(base + pair * 2 * width, base + (pair + 1) * 2 * width)
            z_ref[:, cols] = lax.dot_general(h_ref[...], w_ref[cols, :], NT_DIMS, preferred_element_type=F32)
    log_a = jax.nn.log_sigmoid(gate) / C_TAU
    ci = lax.broadcasted_iota(jnp.int32, (C_CHUNK, C_CHUNK), 0)
    cj = lax.broadcasted_iota(jnp.int32, (C_CHUNK, C_CHUNK), 1)
    causal = ci >= cj
    tri = jnp.where(causal, 1.0, 0.0).astype(BF16)
    rows = [slice(c * C_CHUNK, (c + 1) * C_CHUNK) for c in range(chunks)]
    for hd in range(C_HEADS):
        kl = slice(hd * C_DK, (hd + 1) * C_DK)
        vl = slice(hd * C_DV, (hd + 1) * C_DV)
        la = log_a[:, kl]
        hi = la.astype(BF16)
        rest = la - hi.astype(F32)
        mid = rest.astype(BF16)
        lo = (rest - mid.astype(F32)).astype(BF16)
        pieces = jnp.concatenate([hi, mid, lo], axis=1)
        sums = [jnp.dot(tri, pieces[sl], preferred_element_type=F32) for sl in rows]
        b = jnp.concatenate([s3[:, :C_DK] + s3[:, C_DK:2 * C_DK] + s3[:, 2 * C_DK:] for s3 in sums], axis=0)
        b3 = b.reshape(chunks, C_CHUNK, C_DK)
        b_last = b3[:, C_CHUNK - 1:C_CHUNK, :]
        k = k_ref[:, kl]
        q_t = ((q_ref[:, kl] * (C_DK ** -0.5)) * jnp.exp(b)).astype(BF16)
        k_t = (k * jnp.exp(-b)).astype(BF16)
        k_s = (k.reshape(chunks, C_CHUNK, C_DK) * jnp.exp(b_last - b3)).reshape(tc, C_DK).astype(BF16)
        decay = jnp.exp(b_last)
        o_intra, kv_t = [], []
        for sl in rows:
            v_c = v_ref[sl, vl].astype(BF16)
            attn = lax.dot_general(q_t[sl], k_t[sl], NT_DIMS, preferred_element_type=F32)
            attn = jnp.where(causal, attn, 0.0).astype(BF16)
            o_intra.append(jnp.dot(attn, v_c, preferred_element_type=F32))
            kv_t.append(lax.dot_general(v_c, k_s[sl], TN_DIMS, preferred_element_type=F32))
        st = st_ref[hd]
        entering = []
        for c in range(chunks):
            entering.append(st.astype(BF16))
            st = st * decay[c] + kv_t[c]
        st_ref[hd] = st
        for c, sl in enumerate(rows):
            o = o_intra[c] + lax.dot_general(q_t[sl], entering[c], NT_DIMS, preferred_element_type=F32)
            o_ref[sl, vl] = (_rms_rows(o, hg_ref[...]) * _silu(r_ref[sl, vl])).astype(BF16)


def _gla(x, g, wt_all, li, w_side, w_a2, b_a, head_g, *, bsz, tc):
    n, d = x.shape
    tiles = n // bsz // tc
    f = wt_all.shape[1]
    hk, hv = C_HEADS * C_DK, C_HEADS * C_DV
    wa = jnp.pad(w_a2, ((0, V7X_LANES - C_GATE_RANK), (0, 0))).astype(BF16)
    vmem = (2 * (tc * d * 4 + f * d * 2 + d * V7X_LANES * 2 + tc * hv * 2 + V7X_LANES * hk * 2)
            + tc * d * 2 + tc * f * 4 + hv * C_DK * 4 + 12 * tc * C_DK * 4 + 2 * tc * hk * 4)
    return pl.pallas_call(
        _gla_kernel,
        grid=(bsz, tiles),
        in_specs=[pl.BlockSpec((tc, d), lambda b, t: (b * tiles + t, 0)),
                  pl.BlockSpec((1, d), lambda b, t: (0, 0)),
                  _layer_spec(wt_all, li),
                  pl.BlockSpec((d, V7X_LANES), lambda b, t: (0, 0)),
                  pl.BlockSpec((V7X_LANES, hk), lambda b, t: (0, 0)),
                  pl.BlockSpec((1, hk), lambda b, t: (0, 0)),
                  pl.BlockSpec((1, C_DV), lambda b, t: (0, 0))],
        out_specs=pl.BlockSpec((tc, hv), lambda b, t: (b * tiles + t, 0)),
        out_shape=jax.ShapeDtypeStruct((n, hv), BF16),
        scratch_shapes=[pltpu.VMEM((C_HEADS, C_DV, C_DK), F32), pltpu.VMEM((tc, d), BF16),
                        pltpu.VMEM((tc, f), F32)],
        compiler_params=_params(("parallel", "arbitrary"), vmem),
        name="gla",
    )(x, g.reshape(1, d), wt_all, w_side, wa, b_a.reshape(1, -1), head_g.reshape(1, C_DV))


def _ffn_kernel(x_ref, a_ref, b_ref, wa_ref, wb_ref, g_ref, wg_ref, wu_ref, cw_ref, cb_ref, wd_ref, *rest,
                tiles_per_seq, tf, n_cast):
    cast_src, o_ref, cast_dst = rest[:n_cast], rest[n_cast], rest[n_cast + 1:2 * n_cast + 1]
    h_ref, act_ref, gate_ref, up_ref = rest[2 * n_cast + 1:]
    tm = x_ref.shape[0]
    halo_rows = V7X_SUBLANES
    seq_start = pl.program_id(0) % tiles_per_seq == 0

    @pl.when(seq_start)
    def _():
        gate_ref[0:halo_rows, :] = jnp.zeros((halo_rows, gate_ref.shape[1]), F32)

    @pl.when(jnp.logical_not(seq_start))
    def _():
        gate_ref[0:halo_rows, :] = gate_ref[tm:tm + halo_rows, :]

    for r in range(tm // ROW_SUB):
        rows = slice(r * ROW_SUB, (r + 1) * ROW_SUB)
        x1 = x_ref[rows, :] + jnp.dot(a_ref[rows, :].astype(BF16), wa_ref[...], preferred_element_type=F32)
        x1 = x1 + jnp.dot(b_ref[rows, :].astype(BF16), wb_ref[...], preferred_element_type=F32)
        o_ref[rows, :] = x1
        h_ref[rows, :] = _rms_rows(x1, g_ref[...]).astype(BF16)

    for r in range(tm // ROW_SUB):
        rows = slice(r * ROW_SUB, (r + 1) * ROW_SUB)
        gate_ref[halo_rows + r * ROW_SUB:halo_rows + (r + 1) * ROW_SUB, :] = jnp.dot(
            h_ref[rows, :], wg_ref[...], preferred_element_type=F32)
    up_ref[...] = jnp.dot(h_ref[...], wu_ref[...], preferred_element_type=F32)
    for j in range(wg_ref.shape[1] // tf):
        cols = slice(j * tf, (j + 1) * tf)
        conv = cb_ref[:, cols]
        for tap in range(CONV_W):
            conv = conv + gate_ref[pl.ds(halo_rows - (CONV_W - 1) + tap, tm), cols] * cw_ref[tap:tap + 1, cols]
        act_ref[:, cols] = (_silu(conv) * up_ref[:, cols]).astype(BF16)
    o_ref[...] += jnp.dot(act_ref[...], wd_ref[...], preferred_element_type=F32)
    for src, dst in zip(cast_src, cast_dst):
        dst[...] = src[...].astype(BF16)


def _ffn(x, a, a_blk, b, b_blk, w_out_all, lo, g, w_gate_all, w_up_all, conv_w, conv_b, w_down_all, lf,
         cast=(), *, seq, tm, tf):
    n, d = x.shape
    dff = w_gate_all.shape[2]
    kh = w_out_all.shape[1] // 2
    cast_in, cast_out, cast_shape, cast_vmem = _cast_plumbing(cast, n // tm, lambda i: i)
    vmem = (2 * (2 * tm * d * 4 + 2 * tm * kh * 4 + 2 * kh * d * 2 + 3 * d * dff * 2 + 4 * dff * 4)
            + tm * d * 2 + tm * dff * 2 + (2 * tm + 8) * dff * 4 + 4 * tm * tf * 4 + cast_vmem)
    outs = pl.pallas_call(
        functools.partial(_ffn_kernel, tiles_per_seq=seq // tm, tf=tf, n_cast=len(cast)),
        grid=(n // tm,),
        in_specs=[pl.BlockSpec((tm, d), lambda i: (i, 0)),
                  pl.BlockSpec((tm, kh), lambda i: (i, a_blk)),
                  pl.BlockSpec((tm, kh), lambda i: (i, b_blk)),
                  pl.BlockSpec((None, kh, d), lambda i: (lo, 0, 0)),
                  pl.BlockSpec((None, kh, d), lambda i: (lo, 1, 0)),
                  pl.BlockSpec((1, d), lambda i: (0, 0)),
                  _layer_spec(w_gate_all, lf),
                  _layer_spec(w_up_all, lf),
                  pl.BlockSpec((CONV_W, dff), lambda i: (0, 0)),
                  pl.BlockSpec((1, dff), lambda i: (0, 0)),
                  _layer_spec(w_down_all, lf)] + cast_in,
        out_specs=[pl.BlockSpec((tm, d), lambda i: (i, 0))] + cast_out,
        out_shape=[jax.ShapeDtypeStruct((n, d), F32)] + cast_shape,
        scratch_shapes=[pltpu.VMEM((tm, d), BF16), pltpu.VMEM((tm, dff), BF16),
                        pltpu.VMEM((V7X_SUBLANES + tm, dff), F32), pltpu.VMEM((tm, dff), F32)],
        compiler_params=_params(("arbitrary",), vmem),
        name="conv_ffn",
    )(x, a, b, w_out_all, w_out_all, g.reshape(1, d), w_gate_all, w_up_all, conv_w, conv_b.reshape(1, dff),
      w_down_all, *[arr for arr, _, _ in cast])
    return outs[0], outs[1:]


def kernel(x, norm_mix_g, norm_ffn_g, ev_w_in, ev_a_ln_g, ev_a_ln_b, ev_a_ws, ev_a_bs, ev_q_g, ev_k_g,
           ev_w_out, od_w_in, od_w_a2, od_b_a, od_head_g, od_w_out, ffn_w_gate, ffn_w_up, ffn_conv_w,
           ffn_conv_b, ffn_w_down):
    bsz, seq, d = x.shape
    n = bsz * seq
    depth = norm_mix_g.shape[0]
    main = 2 * C_HEADS * (C_DK + C_DV)
    ev_w_out, od_w_out = ev_w_out.astype(BF16), od_w_out.astype(BF16)
    od_w_side = jnp.pad(od_w_in[:, :, main:], ((0, 0), (0, 0), (0, V7X_LANES - C_GATE_RANK))).astype(BF16)
    ffn_f32 = (ffn_w_gate, ffn_w_up, ffn_w_down)
    od_w_in_t = jnp.swapaxes(od_w_in, 1, 2)

    def cast_for(layer):
        if layer >= depth:
            return []
        items = [(w, layer, w.shape[1]) for w in ffn_f32]
        if layer > 0:
            items.append((ev_w_in, layer // 2, d) if layer % 2 == 0 else (od_w_in_t, layer // 2, main))
        return items

    w_in = ev_w_in[:1].astype(BF16)
    ffn_w = None
    xf = x.reshape(n, d)
    for layer in range(depth):
        if layer % 2 == 0:
            a_out, qb, kn, vn, qp, kp, vp = _even_in(
                xf, norm_mix_g[layer], w_in, 0, ev_a_ln_g[layer // 2], ev_a_ln_b[layer // 2], ev_a_ws[layer // 2],
                ev_a_bs[layer // 2], ev_q_g[layer // 2], ev_k_g[layer // 2], bsz=bsz, tm=EVEN_IN_ROWS)
            in_seq = lambda t: t.reshape(bsz, seq, -1)
            first_cast = cast_for(layer) if layer == 0 else []
            b_out, cast_out = _dilated(in_seq(qb), in_seq(kn), in_seq(vn), qp, kp, vp, first_cast)
            if layer == 0:
                ffn_w = cast_out
            mix = (a_out, 0, b_out.reshape(n, -1), 0, ev_w_out, layer // 2)
        else:
            mixed = _gla(xf, norm_mix_g[layer], w_in, 0, od_w_side[layer // 2], od_w_a2[layer // 2],
                         od_b_a[layer // 2], od_head_g[layer // 2], bsz=bsz, tc=GLA_ROWS)
            mix = (mixed, 0, mixed, 1, od_w_out, layer // 2)
        xf, cast_out = _ffn(xf, *mix, norm_ffn_g[layer], ffn_w[0], ffn_w[1], ffn_conv_w[layer], ffn_conv_b[layer],
                            ffn_w[2], 0, cast_for(layer + 1), seq=seq, tm=FFN_ROWS,
                            tf=FFN_COLS)
        if layer + 1 < depth:
            ffn_w, w_in = cast_out[:3], cast_out[3]
    return xf.reshape(bsz, seq, d)
```
